```python
import math
import jax, jax.numpy as jnp
from jax import lax
import numpy as np

D_MODEL = 4096
BATCH = 2
SEQ = 8192
DEPTH = 2

D_FF = 11008
RMS_EPS = 1e-6
MLA_HEADS = 32
Q_LORA = 1024
KV_LORA = 512
QK_NOPE = 128
QK_ROPE = 64
QK_HEAD = QK_NOPE + QK_ROPE
V_HEAD = 128
ROPE_THETA = 10000.0
DIL_GROUPS = ((128, 1), (512, 4), (2048, 16))
N_GROUPS = 3
DIL_HEADS = 32
DIL_HEAD_DIM = 128
NUM_BUCKETS = 32
MAX_DISTANCE = 2048
Q_BLOCK = 128
N_A_LAYERS = DEPTH // 2
N_B_LAYERS = DEPTH - N_A_LAYERS

kernel_name = "yoco_mla_dilated_macaron_trunk"


def rms_norm(x, g):
    xf = x.astype(jnp.float32)
    y = xf * lax.rsqrt(jnp.mean(xf * xf, axis=-1, keepdims=True) + RMS_EPS)
    return (y * g.astype(jnp.float32)).astype(x.dtype)


def swiglu(x, wg, wu, wd):
    return (jax.nn.silu(x @ wg) * (x @ wu)) @ wd


def rope(x, pos):
    half = x.shape[-1] // 2
    inv = ROPE_THETA ** (-jnp.arange(half, dtype=jnp.float32) / half)
    ang = pos.astype(jnp.float32)[:, None] * inv[None, :]
    cos = jnp.cos(ang)[None, :, None, :]
    sin = jnp.sin(ang)[None, :, None, :]
    xf = x.astype(jnp.float32)
    x1, x2 = xf[..., :half], xf[..., half:]
    return jnp.concatenate([x1 * cos - x2 * sin, x2 * cos + x1 * sin], axis=-1).astype(x.dtype)


def causal_block_attention(q, k, v, scale):
    B, S, H, Dq = q.shape
    nq = S // Q_BLOCK
    qb = q.reshape(B, nq, Q_BLOCK, H, Dq).transpose(1, 0, 2, 3, 4)
    kpos = jnp.arange(S)

    def one_block(args):
        qblk, i = args
        s = jnp.einsum('bqhd,bkhd->bhqk', qblk, k, preferred_element_type=jnp.float32) * scale
        qpos = i * Q_BLOCK + jnp.arange(Q_BLOCK)
        s = jnp.where(kpos[None, :] <= qpos[:, None], s, -jnp.inf)
        p = jax.nn.softmax(s, axis=-1)
        return jnp.einsum('bhqk,bkhd->bqhd', p.astype(v.dtype), v)

    o = lax.map(one_block, (qb, jnp.arange(nq)))
    return o.transpose(1, 0, 2, 3, 4).reshape(B, S, H, v.shape[-1])


def mla_attention(xn, wdq, q_lora_norm, wuq, wdkv, kv_lora_norm, wukv, q_norm, k_norm, wo):
    B, S, _ = xn.shape
    pos = jnp.arange(S)
    cq = rms_norm(xn @ wdq, q_lora_norm)
    q = rms_norm((cq @ wuq).reshape(B, S, MLA_HEADS, QK_HEAD), q_norm)
    q = jnp.concatenate([q[..., :QK_NOPE], rope(q[..., QK_NOPE:], pos)], axis=-1)
    ckv = xn @ wdkv
    c_kv = rms_norm(ckv[..., :KV_LORA], kv_lora_norm)
    k_rope = ckv[..., KV_LORA:]
    kv = (c_kv @ wukv).reshape(B, S, MLA_HEADS, QK_NOPE + V_HEAD)
    k_nope, v = kv[..., :QK_NOPE], kv[..., QK_NOPE:]
    k = jnp.concatenate([k_nope, jnp.broadcast_to(k_rope[:, :, None, :], (B, S, MLA_HEADS, QK_ROPE))], axis=-1)
    k = rms_norm(k, k_norm)
    k = jnp.concatenate([k[..., :QK_NOPE], rope(k[..., QK_NOPE:], pos)], axis=-1)
    o = causal_block_attention(q, k, v, QK_HEAD ** -0.5)
    return o.reshape(B, S, MLA_HEADS * V_HEAD) @ wo


def t5_causal_bucket(dist):
    max_exact = NUM_BUCKETS // 2
    n = jnp.maximum(dist, 0)
    nf = jnp.maximum(n, 1).astype(jnp.float32)
    large = max_exact + (jnp.log(nf / max_exact) / math.log(MAX_DISTANCE / max_exact)
                         * (NUM_BUCKETS - max_exact)).astype(jnp.int32)
    large = jnp.minimum(large, NUM_BUCKETS - 1)
    return jnp.where(n < max_exact, n, large)


def dilated_group(q, k, v, bias_tab, window, dilation):
    B, S, H, Dh = q.shape
    span = dilation * Q_BLOCK
    Sp = -(-S // span) * span
    pad = Sp - S
    L = Sp // dilation
    nb = L // Q_BLOCK

    def to_blocks(t):
        t = jnp.pad(t, ((0, 0), (0, pad), (0, 0), (0, 0)))
        t = t.reshape(B, L, dilation, H, Dh).transpose(0, 2, 1, 3, 4)
        return t.reshape(B, dilation, nb, Q_BLOCK, H, Dh)

    def with_prev(t):
        prev = jnp.pad(t, ((0, 0), (0, 0), (1, 0), (0, 0), (0, 0), (0, 0)))[:, :, :-1]
        return jnp.concatenate([prev, t], axis=3)

    qb = to_blocks(q)
    kk = with_prev(to_blocks(k))
    vv = with_prev(to_blocks(v))
    r = jnp.arange(Q_BLOCK)[:, None]
    c = jnp.arange(2 * Q_BLOCK)[None, :]
    steps = Q_BLOCK + r - c
    band = (steps >= 0) & (steps <= window // dilation)
    first = (jnp.arange(nb)[:, None, None] > 0) | (c[None] >= Q_BLOCK)
    valid = band[None] & first
    bias = bias_tab[t5_causal_bucket(steps * dilation)]
    bias = bias.transpose(2, 0, 1).astype(jnp.float32)
    s = jnp.einsum('brnqhd,brnkhd->brnhqk', qb, kk,
                   preferred_element_type=jnp.float32) * (Dh ** -0.5) + bias
    s = jnp.where(valid[None, None, :, None], s, -jnp.inf)
    m = jnp.max(s, axis=-1, keepdims=True)
    p = jnp.exp(s - m)
    den = jnp.sum(p, axis=-1, keepdims=True)
    o = jnp.einsum('brnhqk,brnkhd->brnqhd', (p / den).astype(v.dtype), vv)
    lse = (m + jnp.log(den))[..., 0].transpose(0, 1, 2, 4, 3)

    def from_blocks(t):
        t = t.reshape((B, dilation, L) + t.shape[4:]).swapaxes(1, 2)
        return t.reshape((B, Sp) + t.shape[3:])[:, :S]

    return from_blocks(o), from_blocks(lse)


def shared_kv(h, kv_src_norm, w_kv_shared, k_norm_shared):
    B, S, _ = h.shape
    kv = (rms_norm(h, kv_src_norm) @ w_kv_shared).reshape(B, S, 2, N_GROUPS, DIL_HEADS, DIL_HEAD_DIM)
    k = rms_norm(kv[:, :, 0], k_norm_shared[:, None, :])
    v = kv[:, :, 1]
    return k, v


def dilated_attention(xn, wq, q_norm, k, v, rel_bias, wo):
    B, S, _ = xn.shape
    q = rms_norm((xn @ wq).reshape(B, S, N_GROUPS, DIL_HEADS, DIL_HEAD_DIM), q_norm[:, None, :])
    outs, lses = [], []
    for g, (window, dilation) in enumerate(DIL_GROUPS):
        o_g, l_g = dilated_group(q[:, :, g], k[:, :, g], v[:, :, g],
                                 rel_bias[:, g * DIL_HEADS:(g + 1) * DIL_HEADS], window, dilation)
        outs.append(o_g)
        lses.append(l_g)
    wts = jax.nn.softmax(jnp.stack(lses, axis=0), axis=0)
    o = jnp.einsum('gbsh,gbshd->bshd', wts.astype(xn.dtype), jnp.stack(outs, axis=0))
    return o.reshape(B, S, DIL_HEADS * DIL_HEAD_DIM) @ wo


def _w(k, shape, fan_in):
    return jax.random.normal(k, shape, jnp.float32) * (fan_in ** -0.5)


def _g(k, shape):
    return 1.0 + 0.01 * jax.random.normal(k, shape, jnp.float32)


def setup_inputs(seed: int = 0) -> dict:
    key = jax.random.key(seed)
    ks = jax.random.split(key, 24)
    NA, NB = N_A_LAYERS, N_B_LAYERS
    dil_w = N_GROUPS * DIL_HEADS * DIL_HEAD_DIM
    return {
        "x": jax.random.normal(ks[0], (BATCH, SEQ, D_MODEL), jnp.float32),
        "ffn_norm": _g(ks[1], (DEPTH, 2, D_MODEL)),
        "ffn_wg": _w(ks[2], (DEPTH, 2, D_MODEL, D_FF), D_MODEL),
        "ffn_wu": _w(ks[3], (DEPTH, 2, D_MODEL, D_FF), D_MODEL),
        "ffn_wd": _w(ks[4], (DEPTH, 2, D_FF, D_MODEL), D_FF),
        "attn_norm": _g(ks[5], (DEPTH, D_MODEL)),
        "mla_wdq": _w(ks[6], (NA, D_MODEL, Q_LORA), D_MODEL),
        "mla_q_lora_norm": _g(ks[7], (NA, Q_LORA)),
        "mla_wuq": _w(ks[8], (NA, Q_LORA, MLA_HEADS * QK_HEAD), Q_LORA),
        "mla_wdkv": _w(ks[9], (NA, D_MODEL, KV_LORA + QK_ROPE), D_MODEL),
        "mla_kv_lora_norm": _g(ks[10], (NA, KV_LORA)),
        "mla_wukv": _w(ks[11], (NA, KV_LORA, MLA_HEADS * (QK_NOPE + V_HEAD)), KV_LORA),
        "mla_q_norm": _g(ks[12], (NA, QK_HEAD)),
        "mla_k_norm": _g(ks[13], (NA, QK_HEAD)),
        "mla_wo": _w(ks[14], (NA, MLA_HEADS * V_HEAD, D_MODEL), MLA_HEADS * V_HEAD),
        "kv_src_norm": _g(ks[15], (D_MODEL,)),
        "w_kv_shared": _w(ks[16], (D_MODEL, 2 * dil_w), D_MODEL),
        "k_norm_shared": _g(ks[17], (N_GROUPS, DIL_HEAD_DIM)),
        "rel_bias": 0.5 * jax.random.normal(ks[18], (NUM_BUCKETS, N_GROUPS * DIL_HEADS), jnp.float32),
        "dil_wq": _w(ks[19], (NB, D_MODEL, dil_w), D_MODEL),
        "dil_q_norm": _g(ks[20], (NB, N_GROUPS, DIL_HEAD_DIM)),
        "dil_wo": _w(ks[21], (NB, DIL_HEADS * DIL_HEAD_DIM, D_MODEL), DIL_HEADS * DIL_HEAD_DIM),
    }


def reference(x, ffn_norm, ffn_wg, ffn_wu, ffn_wd, attn_norm, mla_wdq, mla_q_lora_norm, mla_wuq,
              mla_wdkv, mla_kv_lora_norm, mla_wukv, mla_q_norm, mla_k_norm, mla_wo,
              kv_src_norm, w_kv_shared, k_norm_shared, rel_bias, dil_wq, dil_q_norm, dil_wo):
    h = x
    k_sh, v_sh = None, None
    for l in range(DEPTH):
        if l == N_A_LAYERS:
            k_sh, v_sh = shared_kv(h, kv_src_norm, w_kv_shared, k_norm_shared)
        h = h + 0.5 * swiglu(rms_norm(h, ffn_norm[l, 0]), ffn_wg[l, 0], ffn_wu[l, 0], ffn_wd[l, 0])
        xn = rms_norm(h, attn_norm[l])
        if l < N_A_LAYERS:
            a = l
            h = h + mla_attention(xn, mla_wdq[a], mla_q_lora_norm[a], mla_wuq[a], mla_wdkv[a],
                                  mla_kv_lora_norm[a], mla_wukv[a], mla_q_norm[a], mla_k_norm[a], mla_wo[a])
        else:
            b = l - N_A_LAYERS
            h = h + dilated_attention(xn, dil_wq[b], dil_q_norm[b], k_sh, v_sh, rel_bias, dil_wo[b])
        h = h + 0.5 * swiglu(rms_norm(h, ffn_norm[l, 1]), ffn_wg[l, 1], ffn_wu[l, 1], ffn_wd[l, 1])
    return h
```

```python
import functools
import math

import jax
import jax.numpy as jnp
import numpy as np
from jax import lax
from jax.experimental import pallas as pl
from jax.experimental.pallas import tpu as pltpu

F32 = jnp.float32
BF16 = jnp.bfloat16

RMS_EPS = 1e-6
MLA_HEADS = 32
QK_NOPE = 128
QK_ROPE = 64
QK_HEAD = QK_NOPE + QK_ROPE
V_HEAD = 128
ROPE_THETA = 10000.0
DIL_GROUPS = ((128, 1), (512, 4), (2048, 16))
N_GROUPS = 3
DIL_HEADS = 32
DIL_HEAD_DIM = 128
NUM_BUCKETS = 32
MAX_DISTANCE = 2048
Q_BLOCK = 128

LANES = 128
QK_PAD = 2 * LANES
V7X_VMEM_BYTES = 64 * 1024 * 1024
VMEM_HEADROOM_BYTES = 8 * 1024 * 1024
MASK_VALUE = -1e30


def _nbytes(shape, dtype):
    return int(np.prod(shape)) * jnp.dtype(dtype).itemsize


def _params(semantics, block_bytes, scratch_bytes=0):
    need = 2 * block_bytes + scratch_bytes + VMEM_HEADROOM_BYTES
    limit = min(max(need, 32 * 1024 * 1024), V7X_VMEM_BYTES - 4 * 1024 * 1024)
    return pltpu.CompilerParams(dimension_semantics=semantics, vmem_limit_bytes=int(limit))


def _tile(dim, pref):
    if dim <= pref:
        return dim
    t = pref
    while dim % t:
        t //= 2
    return t


def _rmsnorm_kernel(x_ref, g_ref, *o_refs):
    x = x_ref[...]
    y = x * lax.rsqrt(jnp.mean(x * x, axis=-1, keepdims=True) + RMS_EPS)
    for i, o_ref in enumerate(o_refs):
        o_ref[...] = (y * g_ref[i:i + 1, :]).astype(o_ref.dtype)


def _rmsnorm(x, gains, tm=512):
    m, d = x.shape
    n = gains.shape[0]
    tm = _tile(m, tm)
    blocks = _nbytes((tm, d), F32) + n * _nbytes((tm, d), BF16) + _nbytes((n, d), F32)
    return pl.pallas_call(
        _rmsnorm_kernel,
        grid=(m // tm,),
        in_specs=[pl.BlockSpec((tm, d), lambda i: (i, 0)),
                  pl.BlockSpec((n, d), lambda i: (0, 0))],
        out_specs=[pl.BlockSpec((tm, d), lambda i: (i, 0)) for _ in range(n)],
        out_shape=[jax.ShapeDtypeStruct((m, d), BF16) for _ in range(n)],
        compiler_params=_params(("parallel",), blocks),
        name="rmsnorm",
    )(x, gains)


def _dot(a, b):
    return jnp.dot(a, b, preferred_element_type=F32)


def _swiglu_kernel(a_ref, w_ref, o_ref, *, half):
    gu = _dot(a_ref[...], w_ref[...])
    g = gu[:, :half]
    u = gu[:, half:]
    o_ref[...] = (g * jax.nn.sigmoid(g) * u).astype(o_ref.dtype)


def _mm_swiglu(a, wgu, half, tm=1024):
    m, k = a.shape
    n2 = wgu.shape[1]
    tm = _tile(m, tm)
    blocks = _nbytes((tm, k), BF16) + _nbytes((k, 2 * half), BF16) + _nbytes((tm, half), BF16)
    return pl.pallas_call(
        functools.partial(_swiglu_kernel, half=half),
        grid=(m // tm, n2 // (2 * half)),
        in_specs=[pl.BlockSpec((tm, k), lambda i, j: (i, 0)),
                  pl.BlockSpec((k, 2 * half), lambda i, j: (0, j))],
        out_specs=pl.BlockSpec((tm, half), lambda i, j: (i, j)),
        out_shape=jax.ShapeDtypeStruct((m, n2 // 2), BF16),
        compiler_params=_params(("parallel", "parallel"), blocks,
                                scratch_bytes=_nbytes((tm, 2 * half), F32)),
        name="ffn_gate_up",
    )(a, wgu)


def _residual_kernel(a_ref, b_ref, r_ref, o_ref, acc_ref, *, scale, nk):
    k = pl.program_id(2)
    part = _dot(a_ref[...], b_ref[...])
    if nk == 1:
        o_ref[...] = r_ref[...] + scale * part
        return

    @pl.when(k == 0)
    def _():
        acc_ref[...] = part

    @pl.when(jnp.logical_and(k > 0, k < nk - 1))
    def _():
        acc_ref[...] += part

    @pl.when(k == nk - 1)
    def _():
        o_ref[...] = r_ref[...] + scale * (acc_ref[...] + part)


def _mm_residual(a, b, res, scale, tm=1024, tn=1024, tk_max=4096):
    m, kdim = a.shape
    n = b.shape[1]
    tm, tn = _tile(m, tm), _tile(n, tn)
    nk = 1
    while kdim // nk > tk_max or kdim % nk or (kdim // nk) % LANES:
        nk += 1
    tk = kdim // nk
    blocks = (_nbytes((tm, tk), BF16) + _nbytes((tk, tn), BF16) + 2 * _nbytes((tm, tn), F32))
    return pl.pallas_call(
        functools.partial(_residual_kernel, scale=scale, nk=nk),
        grid=(m // tm, n // tn, nk),
        in_specs=[pl.BlockSpec((tm, tk), lambda i, j, k: (i, k)),
                  pl.BlockSpec((tk, tn), lambda i, j, k: (k, j)),
                  pl.BlockSpec((tm, tn), lambda i, j, k: (i, j))],
        out_specs=pl.BlockSpec((tm, tn), lambda i, j, k: (i, j)),
        out_shape=jax.ShapeDtypeStruct((m, n), F32),
        scratch_shapes=[pltpu.VMEM((tm, tn), F32)],
        compiler_params=_params(("parallel", "parallel", "arbitrary"), blocks,
                                scratch_bytes=2 * _nbytes((tm, tn), F32)),
        name="matmul_residual",
    )(a, b, res)


def _plain_kernel(a_ref, b_ref, o_ref):
    o_ref[...] = _dot(a_ref[...], b_ref[...]).astype(o_ref.dtype)


def _mm_plain(a, b, out_dtype, tm=1024, tn=1024):
    m, k = a.shape
    n = b.shape[1]
    tm, tn = _tile(m, tm), _tile(n, tn)
    blocks = _nbytes((tm, k), BF16) + _nbytes((k, tn), BF16) + _nbytes((tm, tn), out_dtype)
    return pl.pallas_call(
        _plain_kernel,
        grid=(m // tm, n // tn),
        in_specs=[pl.BlockSpec((tm, k), lambda i, j: (i, 0)),
                  pl.BlockSpec((k, tn), lambda i, j: (0, j))],
        out_specs=pl.BlockSpec((tm, tn), lambda i, j: (i, j)),
        out_shape=jax.ShapeDtypeStruct((m, n), out_dtype),
        compiler_params=_params(("parallel", "parallel"), blocks,
                                scratch_bytes=_nbytes((tm, tn), F32)),
        name="matmul_plain",
    )(a, b)


def _headnorm_kernel(a_ref, b_ref, g_ref, o_ref, *, head_dim):
    x = _dot(a_ref[...], b_ref[...])
    tn = x.shape[1]
    for h in range(tn // head_dim):
        sl = slice(h * head_dim, (h + 1) * head_dim)
        xh = x[:, sl]
        y = xh * lax.rsqrt(jnp.mean(xh * xh, axis=-1, keepdims=True) + RMS_EPS)
        o_ref[:, sl] = (y * g_ref[:, sl]).astype(o_ref.dtype)


def _mm_headnorm(a, b, gain_row, head_dim, tm=1024, tn=1024):
    m, k = a.shape
    n = b.shape[1]
    tm, tn = _tile(m, tm), _tile(n, tn)
    blocks = (_nbytes((tm, k), BF16) + _nbytes((k, tn), BF16) + _nbytes((tm, tn), BF16)
              + _nbytes((8, tn), F32))
    return pl.pallas_call(
        functools.partial(_headnorm_kernel, head_dim=head_dim),
        grid=(m // tm, n // tn),
        in_specs=[pl.BlockSpec((tm, k), lambda i, j: (i, 0)),
                  pl.BlockSpec((k, tn), lambda i, j: (0, j)),
                  pl.BlockSpec((1, tn), lambda i, j: (0, j))],
        out_specs=pl.BlockSpec((tm, tn), lambda i, j: (i, j)),
        out_shape=jax.ShapeDtypeStruct((m, n), BF16),
        compiler_params=_params(("parallel", "parallel"), blocks,
                                scratch_bytes=_nbytes((tm, tn), F32)),
        name="matmul_headnorm",
    )(a, b, gain_row)


def _mla_down_kernel(a_ref, w_ref, gq_ref, gkv_ref, cq_ref, ckv_ref, kr_ref, *, q_lora, kv_lora):
    x = _dot(a_ref[...], w_ref[...])
    xq = x[:, :q_lora]
    cq = xq * lax.rsqrt(jnp.mean(xq * xq, axis=-1, keepdims=True) + RMS_EPS) * gq_ref[...]
    cq_ref[...] = cq.astype(cq_ref.dtype)
    xkv = x[:, q_lora:q_lora + kv_lora]
    ckv = xkv * lax.rsqrt(jnp.mean(xkv * xkv, axis=-1, keepdims=True) + RMS_EPS) * gkv_ref[...]
    ckv_ref[...] = ckv.astype(ckv_ref.dtype)
    kr_ref[...] = x[:, q_lora + kv_lora:]


def _mla_down(a, wcat, gq, gkv, q_lora, kv_lora, tm=512):
    m, k = a.shape
    n = wcat.shape[1]
    tm = _tile(m, tm)
    blocks = (_nbytes((tm, k), BF16) + _nbytes((k, n), BF16) + _nbytes((tm, q_lora), BF16)
              + _nbytes((tm, kv_lora), BF16) + _nbytes((tm, LANES), F32))
    return pl.pallas_call(
        functools.partial(_mla_down_kernel, q_lora=q_lora, kv_lora=kv_lora),
        grid=(m // tm,),
        in_specs=[pl.BlockSpec((tm, k), lambda i: (i, 0)),
                  pl.BlockSpec((k, n), lambda i: (0, 0)),
                  pl.BlockSpec((1, q_lora), lambda i: (0, 0)),
                  pl.BlockSpec((1, kv_lora), lambda i: (0, 0))],
        out_specs=[pl.BlockSpec((tm, q_lora), lambda i: (i, 0)),
                   pl.BlockSpec((tm, kv_lora), lambda i: (i, 0)),
                   pl.BlockSpec((tm, LANES), lambda i: (i, 0))],
        out_shape=[jax.ShapeDtypeStruct((m, q_lora), BF16),
                   jax.ShapeDtypeStruct((m, kv_lora), BF16),
                   jax.ShapeDtypeStruct((m, LANES), F32)],
        compiler_params=_params(("parallel",), blocks, scratch_bytes=_nbytes((tm, n), F32)),
        name="mla_down",
    )(a, wcat, gq, gkv)


def _rope_lane_tile(x, cos, sin_lo, sin_hi):
    half = QK_ROPE // 2
    return (x * cos + pltpu.roll(x, LANES - half, axis=1) * sin_lo
            + pltpu.roll(x, half, axis=1) * sin_hi)


def _mla_q_kernel(a_ref, w_ref, g_ref, cos_ref, slo_ref, shi_ref, o_ref):
    x = _dot(a_ref[...], w_ref[...])
    cos, slo, shi = cos_ref[...], slo_ref[...], shi_ref[...]
    for h in range(x.shape[1] // QK_PAD):
        xh = x[:, h * QK_PAD:(h + 1) * QK_PAD]
        ms = jnp.sum(xh * xh, axis=-1, keepdims=True) * (1.0 / QK_HEAD)
        y = xh * lax.rsqrt(ms + RMS_EPS) * g_ref[...]
        o_ref[:, h * QK_PAD:h * QK_PAD + LANES] = y[:, :LANES].astype(o_ref.dtype)
        o_ref[:, h * QK_PAD + LANES:(h + 1) * QK_PAD] = _rope_lane_tile(
            y[:, LANES:], cos, slo, shi).astype(o_ref.dtype)


def _mla_q(cq, wuq_p, gain_p, tabs, seq, tm=1024, tn=1024):
    m, k = cq.shape
    n = wuq_p.shape[1]
    tm, tn = _tile(seq, tm), _tile(n, tn)
    nsb = seq // tm
    tab_spec = pl.BlockSpec((tm, LANES), lambda i, j: (i % nsb, 0))
    blocks = (_nbytes((tm, k), BF16) + _nbytes((k, tn), BF16) + _nbytes((tm, tn), BF16)
              + 3 * _nbytes((tm, LANES), F32))
    return pl.pallas_call(
        _mla_q_kernel,
        grid=(m // tm, n // tn),
        in_specs=[pl.BlockSpec((tm, k), lambda i, j: (i, 0)),
                  pl.BlockSpec((k, tn), lambda i, j: (0, j)),
                  pl.BlockSpec((1, QK_PAD), lambda i, j: (0, 0)),
                  tab_spec, tab_spec, tab_spec],
        out_specs=pl.BlockSpec((tm, tn), lambda i, j: (i, j)),
        out_shape=jax.ShapeDtypeStruct((m, n), BF16),
        compiler_params=_params(("parallel", "parallel"), blocks,
                                scratch_bytes=_nbytes((tm, tn), F32)),
        name="mla_q_proj",
    )(cq, wuq_p, gain_p, *tabs)


def _mla_kv_kernel(a_ref, w_ref, kr_ref, gn_ref, gr_ref, cos_ref, slo_ref, shi_ref, k_ref, v_ref):
    x = _dot(a_ref[...], w_ref[...])
    kr = kr_ref[...]
    ss_rope = jnp.sum(kr * kr, axis=-1, keepdims=True)
    kr_roped = _rope_lane_tile(kr * gr_ref[...], cos_ref[...], slo_ref[...], shi_ref[...])
    width = QK_NOPE + V_HEAD
    for h in range(x.shape[1] // width):
        kn = x[:, h * width:h * width + QK_NOPE]
        ms = (jnp.sum(kn * kn, axis=-1, keepdims=True) + ss_rope) * (1.0 / QK_HEAD)
        rs = lax.rsqrt(ms + RMS_EPS)
        k_ref[:, h * QK_PAD:h * QK_PAD + LANES] = (kn * rs * gn_ref[...]).astype(k_ref.dtype)
        k_ref[:, h * QK_PAD + LANES:(h + 1) * QK_PAD] = (kr_roped * rs).astype(k_ref.dtype)
        v_ref[:, h * V_HEAD:(h + 1) * V_HEAD] = x[:, h * width + QK_NOPE:(h + 1) * width].astype(v_ref.dtype)


def _mla_kv(ckv, wukv, k_rope, g_nope, g_rope, tabs, seq, tm=1024, tn=1024):
    m, k = ckv.shape
    n = wukv.shape[1]
    width = QK_NOPE + V_HEAD
    tm, tn = _tile(seq, tm), _tile(n, tn)
    hpt = tn // width
    nsb = seq // tm
    tab_spec = pl.BlockSpec((tm, LANES), lambda i, j: (i % nsb, 0))
    blocks = (_nbytes((tm, k), BF16) + _nbytes((k, tn), BF16) + _nbytes((tm, hpt * QK_PAD), BF16)
              + _nbytes((tm, hpt * V_HEAD), BF16) + 4 * _nbytes((tm, LANES), F32))
    return pl.pallas_call(
        _mla_kv_kernel,
        grid=(m // tm, n // tn),
        in_specs=[pl.BlockSpec((tm, k), lambda i, j: (i, 0)),
                  pl.BlockSpec((k, tn), lambda i, j: (0, j)),
                  pl.BlockSpec((tm, LANES), lambda i, j: (i, 0)),
                  pl.BlockSpec((1, LANES), lambda i, j: (0, 0)),
                  pl.BlockSpec((1, LANES), lambda i, j: (0, 0)),
                  tab_spec, tab_spec, tab_spec],
        out_specs=[pl.BlockSpec((tm, hpt * QK_PAD), lambda i, j: (i, j)),
                   pl.BlockSpec((tm, hpt * V_HEAD), lambda i, j: (i, j))],
        out_shape=[jax.ShapeDtypeStruct((m, (n // width) * QK_PAD), BF16),
                   jax.ShapeDtypeStruct((m, (n // width) * V_HEAD), BF16)],
        compiler_params=_params(("parallel", "parallel"), blocks,
                                scratch_bytes=_nbytes((tm, tn), F32)),
        name="mla_kv_proj",
    )(ckv, wukv, k_rope, g_nope, g_rope, *tabs)


def _dot_nt(a, b):
    return lax.dot_general(a, b, (((1,), (1,)), ((), ())), preferred_element_type=F32)


def _flash_kernel(q_ref, k_ref, v_ref, o_ref, m_ref, l_ref, acc_ref, *, scale, blk):
    qi = pl.program_id(2)
    q = q_ref[...]
    m_ref[...] = jnp.full(m_ref.shape, MASK_VALUE, F32)
    l_ref[...] = jnp.zeros(l_ref.shape, F32)
    acc_ref[...] = jnp.zeros(acc_ref.shape, F32)

    def step(j, masked):
        start = pl.multiple_of(j * blk, blk)
        s = _dot_nt(q, k_ref[pl.ds(start, blk), :]) * scale
        if masked:
            row = lax.broadcasted_iota(jnp.int32, s.shape, 0)
            col = lax.broadcasted_iota(jnp.int32, s.shape, 1)
            s = jnp.where(col <= row, s, MASK_VALUE)
        m_old = m_ref[...]
        m_new = jnp.maximum(m_old, jnp.max(s, axis=-1, keepdims=True))
        p = jnp.exp(s - m_new)
        alpha = jnp.exp(m_old - m_new)
        l_ref[...] = alpha * l_ref[...] + jnp.sum(p, axis=-1, keepdims=True)
        acc_ref[...] = alpha * acc_ref[...] + _dot(p.astype(BF16), v_ref[pl.ds(start, blk), :])
        m_ref[...] = m_new

    def body(j, carry):
        step(j, masked=False)
        return carry

    lax.fori_loop(0, qi, body, 0)
    step(qi, masked=True)
    o_ref[...] = (acc_ref[...] / l_ref[...]).astype(o_ref.dtype)


def _mla_flash(q, k, v, scale, blk=512):
    b, s, _ = q.shape
    h = q.shape[2] // QK_PAD
    blk = _tile(s, blk)
    blocks = (_nbytes((blk, QK_PAD), BF16) + _nbytes((s, QK_PAD), BF16) + _nbytes((s, V_HEAD), BF16)
              + _nbytes((blk, V_HEAD), BF16))
    scratch = 2 * _nbytes((blk, LANES), F32) + _nbytes((blk, V_HEAD), F32) + 3 * _nbytes((blk, blk), F32)
    return pl.pallas_call(
        functools.partial(_flash_kernel, scale=scale, blk=blk),
        grid=(b, h, s // blk),
        in_specs=[pl.BlockSpec((None, blk, QK_PAD), lambda bi, hi, qi: (bi, qi, hi)),
                  pl.BlockSpec((None, s, QK_PAD), lambda bi, hi, qi: (bi, 0, hi)),
                  pl.BlockSpec((None, s, V_HEAD), lambda bi, hi, qi: (bi, 0, hi))],
        out_specs=pl.BlockSpec((None, blk, V_HEAD), lambda bi, hi, qi: (bi, qi, hi)),
        out_shape=jax.ShapeDtypeStruct((b, s, h * V_HEAD), BF16),
        scratch_shapes=[pltpu.VMEM((blk, 1), F32), pltpu.VMEM((blk, 1), F32),
                        pltpu.VMEM((blk, V_HEAD), F32)],
        compiler_params=_params(("parallel", "parallel", "arbitrary"), blocks, scratch_bytes=scratch),
        name="mla_flash_attention",
    )(q, k, v)


def _t5_causal_bucket(dist):
    max_exact = NUM_BUCKETS // 2
    n = jnp.maximum(dist, 0)
    nf = jnp.maximum(n, 1).astype(F32)
    large = max_exact + (jnp.log(nf / max_exact) / math.log(MAX_DISTANCE / max_exact)
                         * (NUM_BUCKETS - max_exact)).astype(jnp.int32)
    large = jnp.minimum(large, NUM_BUCKETS - 1)
    return jnp.where(n < max_exact, n, large)


def _band_bias_kernel(tab_ref, bucket_ref, o_ref, *, heads):
    g = pl.program_id(0)
    h = pl.program_id(1)
    bucket = bucket_ref[...]
    acc = jnp.full(bucket.shape, MASK_VALUE, F32)
    for b in range(NUM_BUCKETS):
        acc = jnp.where(bucket == b, tab_ref[b, g * heads + h], acc)
    o_ref[...] = acc


def _band_bias(rel_bias, buckets, heads):
    g = buckets.shape[0]
    blk = buckets.shape[1:]
    blocks = _nbytes(blk, jnp.int32) + _nbytes(blk, F32)
    return pl.pallas_call(
        functools.partial(_band_bias_kernel, heads=heads),
        grid=(g, heads),
        in_specs=[pl.BlockSpec(memory_space=pltpu.SMEM),
                  pl.BlockSpec((None,) + blk, lambda gi, hi: (gi, 0, 0))],
        out_specs=pl.BlockSpec((None, None) + blk, lambda gi, hi: (gi, hi, 0, 0)),
        out_shape=jax.ShapeDtypeStruct((g, heads) + blk, F32),
        compiler_params=_params(("parallel", "parallel"), blocks),
        name="dilated_band_bias",
    )(rel_bias, buckets)


def _dilated_kernel(q_ref, kp_ref, kc_ref, vp_ref, vc_ref, bias_ref, o_ref, lse_ref, *, scale, heads):
    n = pl.program_id(2)
    has_prev = n > 0
    d = DIL_HEAD_DIM
    lane = lax.broadcasted_iota(jnp.int32, (Q_BLOCK, LANES), 1)
    lse_tile = jnp.zeros((Q_BLOCK, LANES), F32)
    for h in range(heads):
        sl = slice(h * d, (h + 1) * d)
        q = q_ref[:, sl]
        s_prev = _dot_nt(q, kp_ref[:, sl]) * scale + bias_ref[h, :, :Q_BLOCK]
        s_prev = jnp.where(has_prev, s_prev, MASK_VALUE)
        s_cur = _dot_nt(q, kc_ref[:, sl]) * scale + bias_ref[h, :, Q_BLOCK:]
        m = jnp.maximum(jnp.max(s_prev, axis=-1, keepdims=True), jnp.max(s_cur, axis=-1, keepdims=True))
        p_prev = jnp.exp(s_prev - m)
        p_cur = jnp.exp(s_cur - m)
        den = jnp.sum(p_prev, axis=-1, keepdims=True) + jnp.sum(p_cur, axis=-1, keepdims=True)
        o = (_dot((p_prev / den).astype(BF16), vp_ref[:, sl])
             + _dot((p_cur / den).astype(BF16), vc_ref[:, sl]))
        o_ref[:, sl] = o.astype(o_ref.dtype)
        lse_tile = jnp.where(lane == h, m + jnp.log(den), lse_tile)
    lse_ref[...] = lse_tile


def _dilated_group(q, k, v, bias_g, g, dilation):
    b, s, width = q.shape
    hd = DIL_HEADS * DIL_HEAD_DIM
    ng = width // hd
    length = s // dilation
    nb = length // Q_BLOCK
    qv = q.reshape(b, length, dilation * width)
    kv_ = k.reshape(b, length, dilation * width)
    vv = v.reshape(b, length, dilation * width)
    cur = lambda bi, r, n: (bi, n, r * ng + g)
    prev = lambda bi, r, n: (bi, jnp.maximum(n - 1, 0), r * ng + g)
    blk = (None, Q_BLOCK, hd)
    blocks = 6 * _nbytes((Q_BLOCK, hd), BF16) + _nbytes(bias_g.shape, F32) + _nbytes((Q_BLOCK, LANES), F32)
    o, lse = pl.pallas_call(
        functools.partial(_dilated_kernel, scale=DIL_HEAD_DIM ** -0.5, heads=DIL_HEADS),
        grid=(b, dilation, nb),
        in_specs=[pl.BlockSpec(blk, cur),
                  pl.BlockSpec(blk, prev), pl.BlockSpec(blk, cur),
                  pl.BlockSpec(blk, prev), pl.BlockSpec(blk, cur),
                  pl.BlockSpec(bias_g.shape, lambda bi, r, n: (0, 0, 0))],
        out_specs=[pl.BlockSpec(blk, lambda bi, r, n: (bi, n, r)),
                   pl.BlockSpec((None, Q_BLOCK, LANES), lambda bi, r, n: (bi, n, r))],
        out_shape=[jax.ShapeDtypeStruct((b, length, dilation * hd), BF16),
                   jax.ShapeDtypeStruct((b, length, dilation * LANES), F32)],
        compiler_params=_params(("parallel", "parallel", "arbitrary"), blocks),
        name=f"dilated_attention_d{dilation}",
    )(qv, kv_, kv_, vv, vv, bias_g)
    return o.reshape(b, s, hd), lse.reshape(b, s, LANES)


def _combine_kernel(o0_ref, o1_ref, o2_ref, l0_ref, l1_ref, l2_ref, o_ref, *, heads):
    l0, l1, l2 = l0_ref[...], l1_ref[...], l2_ref[...]
    m = jnp.maximum(jnp.maximum(l0, l1), l2)
    e0, e1, e2 = jnp.exp(l0 - m), jnp.exp(l1 - m), jnp.exp(l2 - m)
    tot = e0 + e1 + e2
    w0, w1, w2 = e0 / tot, e1 / tot, e2 / tot
    d = DIL_HEAD_DIM
    for h in range(heads):
        sl = slice(h * d, (h + 1) * d)
        o = (w0[:, h:h + 1] * o0_ref[:, sl].astype(F32) + w1[:, h:h + 1] * o1_ref[:, sl].astype(F32)
             + w2[:, h:h + 1] * o2_ref[:, sl].astype(F32))
        o_ref[:, sl] = o.astype(o_ref.dtype)


def _combine_groups(outs, lses, tm=512):
    m, hd = outs[0].shape
    tm = _tile(m, tm)
    blocks = 4 * _nbytes((tm, hd), BF16) + 3 * _nbytes((tm, LANES), F32)
    o_spec = pl.BlockSpec((tm, hd), lambda i: (i, 0))
    l_spec = pl.BlockSpec((tm, LANES), lambda i: (i, 0))
    return pl.pallas_call(
        functools.partial(_combine_kernel, heads=DIL_HEADS),
        grid=(m // tm,),
        in_specs=[o_spec, o_spec, o_spec, l_spec, l_spec, l_spec],
        out_specs=o_spec,
        out_shape=jax.ShapeDtypeStruct((m, hd), BF16),
        compiler_params=_params(("parallel",), blocks),
        name="dilated_combine",
    )(*outs, *lses)


GATE_UP_HALF = 512


def _prep_ffn(wg, wu, wd):
    d, f = wg.shape
    fp = -(-f // GATE_UP_HALF) * GATE_UP_HALF
    pad = fp - f
    g = jnp.pad(wg.astype(BF16), ((0, 0), (0, pad))).reshape(d, fp // GATE_UP_HALF, 1, GATE_UP_HALF)
    u = jnp.pad(wu.astype(BF16), ((0, 0), (0, pad))).reshape(d, fp // GATE_UP_HALF, 1, GATE_UP_HALF)
    wgu = jnp.concatenate([g, u], axis=2).reshape(d, 2 * fp)
    wdp = jnp.pad(wd.astype(BF16), ((0, pad), (0, 0)))
    return wgu, wdp


def _rope_tables(seq):
    half = QK_ROPE // 2
    inv = ROPE_THETA ** (-jnp.arange(half, dtype=F32) / half)
    ang = jnp.arange(seq).astype(F32)[:, None] * inv[None, :]
    cos, sin = jnp.cos(ang), jnp.sin(ang)
    zeros = jnp.zeros_like(cos)
    cos_t = jnp.concatenate([cos, cos, zeros, zeros], axis=1)
    sin_lo = jnp.concatenate([-sin, zeros, zeros, zeros], axis=1)
    sin_hi = jnp.concatenate([zeros, sin, zeros, zeros], axis=1)
    return cos_t, sin_lo, sin_hi


def _band_buckets():
    r = jnp.arange(Q_BLOCK)[:, None]
    c = jnp.arange(2 * Q_BLOCK)[None, :]
    steps = Q_BLOCK + r - c
    out = []
    for window, dilation in DIL_GROUPS:
        band = (steps >= 0) & (steps <= window // dilation)
        out.append(jnp.where(band, _t5_causal_bucket(steps * dilation), -1))
    return jnp.stack(out).astype(jnp.int32)


def _ffn(h, xn, wgu, wdp):
    act = _mm_swiglu(xn, wgu, GATE_UP_HALF)
    return _mm_residual(act, wdp, h, 0.5)


def kernel(x, ffn_norm, ffn_wg, ffn_wu, ffn_wd, attn_norm, mla_wdq, mla_q_lora_norm, mla_wuq, mla_wdkv, mla_kv_lora_norm, mla_wukv, mla_q_norm, mla_k_norm, mla_wo, kv_src_norm, w_kv_shared, k_norm_shared, rel_bias, dil_wq, dil_q_norm, dil_wo):
    bsz, seq, d = x.shape
    m = bsz * seq
    depth = ffn_norm.shape[0]
    n_a = depth // 2
    h = x.reshape(m, d)
    tabs = _rope_tables(seq)
    hd = DIL_HEADS * DIL_HEAD_DIM
    k_sh = v_sh = None
    bias_all = None

    for l in range(depth):
        if l == n_a:
            (xs,) = _rmsnorm(h, kv_src_norm[None, :])
            wk = w_kv_shared[:, :N_GROUPS * hd].astype(BF16)
            wv = w_kv_shared[:, N_GROUPS * hd:].astype(BF16)
            gk = jnp.broadcast_to(k_norm_shared[:, None, :], (N_GROUPS, DIL_HEADS, DIL_HEAD_DIM)).reshape(1, -1)
            k_sh = _mm_headnorm(xs, wk, gk, DIL_HEAD_DIM).reshape(bsz, seq, -1)
            v_sh = _mm_plain(xs, wv, BF16).reshape(bsz, seq, -1)
            bias_all = _band_bias(rel_bias, _band_buckets(), DIL_HEADS)

        wgu, wdp = _prep_ffn(ffn_wg[l, 0], ffn_wu[l, 0], ffn_wd[l, 0])
        (xn,) = _rmsnorm(h, ffn_norm[l, 0][None, :])
        h = _ffn(h, xn, wgu, wdp)

        (xn,) = _rmsnorm(h, attn_norm[l][None, :])
        if l < n_a:
            a = l
            q_lora = mla_wdq.shape[2]
            kv_lora = mla_wdkv.shape[2] - QK_ROPE
            wcat = jnp.concatenate(
                [mla_wdq[a], mla_wdkv[a], jnp.zeros((d, LANES - QK_ROPE), F32)], axis=1).astype(BF16)
            cq, ckv, k_rope = _mla_down(xn, wcat, mla_q_lora_norm[a][None, :],
                                        mla_kv_lora_norm[a][None, :], q_lora, kv_lora)
            wuq_p = jnp.pad(mla_wuq[a].astype(BF16).reshape(q_lora, MLA_HEADS, QK_HEAD),
                            ((0, 0), (0, 0), (0, QK_PAD - QK_HEAD))).reshape(q_lora, MLA_HEADS * QK_PAD)
            gq_p = jnp.pad(mla_q_norm[a], (0, QK_PAD - QK_HEAD))[None, :]
            q = _mla_q(cq, wuq_p, gq_p, tabs, seq)
            g_nope = mla_k_norm[a][:QK_NOPE][None, :]
            g_rope = jnp.pad(mla_k_norm[a][QK_NOPE:], (0, LANES - QK_ROPE))[None, :]
            k, v = _mla_kv(ckv, mla_wukv[a].astype(BF16), k_rope, g_nope, g_rope, tabs, seq)
            o = _mla_flash(q.reshape(bsz, seq, -1), k.reshape(bsz, seq, -1), v.reshape(bsz, seq, -1),
                           QK_HEAD ** -0.5)
            h = _mm_residual(o.reshape(m, -1), mla_wo[a].astype(BF16), h, 1.0)
        else:
            bl = l - n_a
            gq = jnp.broadcast_to(dil_q_norm[bl][:, None, :], (N_GROUPS, DIL_HEADS, DIL_HEAD_DIM)).reshape(1, -1)
            q = _mm_headnorm(xn, dil_wq[bl].astype(BF16), gq, DIL_HEAD_DIM).reshape(bsz, seq, -1)
            outs, lses = [], []
            for g, (_, dilation) in enumerate(DIL_GROUPS):
                o_g, l_g = _dilated_group(q, k_sh, v_sh, bias_all[g], g, dilation)
                outs.append(o_g.reshape(m, hd))
                lses.append(l_g.reshape(m, LANES))
            o = _combine_groups(outs, lses)
            h = _mm_residual(o, dil_wo[bl].astype(BF16), h, 1.0)

        wgu, wdp = _prep_ffn(ffn_wg[l, 1], ffn_wu[l, 1], ffn_wd[l, 1])
        (xn,) = _rmsnorm(h, ffn_norm[l, 1][None, :])
        h = _ffn(h, xn, wgu, wdp)

    return h.reshape(bsz, seq, d)
```

```python
import functools
import math

import jax
import jax.numpy as jnp
import numpy as np
from jax import lax
from jax.experimental import pallas as pl
from jax.experimental.pallas import tpu as pltpu

F32 = jnp.float32
BF16 = jnp.bfloat16

RMS_EPS = 1e-6
MLA_HEADS = 32
QK_NOPE = 128
QK_ROPE = 64
QK_HEAD = QK_NOPE + QK_ROPE
V_HEAD = 128
ROPE_THETA = 10000.0
DIL_GROUPS = ((128, 1), (512, 4), (2048, 16))
N_GROUPS = 3
DIL_HEADS = 32
DIL_HEAD_DIM = 128
NUM_BUCKETS = 32
MAX_DISTANCE = 2048
Q_BLOCK = 128

LANES = 128
QK_PAD = 2 * LANES
V7X_VMEM_BYTES = 64 * 1024 * 1024
VMEM_HEADROOM_BYTES = 8 * 1024 * 1024
MASK_VALUE = -1e30


def _nbytes(shape, dtype):
    return int(np.prod(shape)) * jnp.dtype(dtype).itemsize


def _params(semantics, block_bytes, scratch_bytes=0):
    need = 2 * block_bytes + scratch_bytes + VMEM_HEADROOM_BYTES
    limit = min(max(need, 32 * 1024 * 1024), V7X_VMEM_BYTES - 4 * 1024 * 1024)
    return pltpu.CompilerParams(dimension_semantics=semantics, vmem_limit_bytes=int(limit))


def _tile(dim, pref):
    if dim <= pref:
        return dim
    t = pref
    while dim % t:
        t //= 2
    return t


def _rmsnorm_kernel(x_ref, g_ref, *o_refs):
    x = x_ref[...]
    y = x * lax.rsqrt(jnp.mean(x * x, axis=-1, keepdims=True) + RMS_EPS)
    for i, o_ref in enumerate(o_refs):
        o_ref[...] = (y * g_ref[i:i + 1, :]).astype(o_ref.dtype)


def _rmsnorm(x, gains, tm=512):
    m, d = x.shape
    n = gains.shape[0]
    tm = _tile(m, tm)
    blocks = _nbytes((tm, d), F32) + n * _nbytes((tm, d), BF16) + _nbytes((n, d), F32)
    return pl.pallas_call(
        _rmsnorm_kernel,
        grid=(m // tm,),
        in_specs=[pl.BlockSpec((tm, d), lambda i: (i, 0)),
                  pl.BlockSpec((n, d), lambda i: (0, 0))],
        out_specs=[pl.BlockSpec((tm, d), lambda i: (i, 0)) for _ in range(n)],
        out_shape=[jax.ShapeDtypeStruct((m, d), BF16) for _ in range(n)],
        compiler_params=_params(("parallel",), blocks),
        name="rmsnorm",
    )(x, gains)


CAST_BLOCK_BYTES = 12 * 1024 * 1024


def _cast_kernel(x_ref, o_ref, *, rows_in, cols_in, tr):
    x = x_ref[...].astype(o_ref.dtype)
    if o_ref.shape[1] > cols_in:
        o_ref[:, cols_in:] = jnp.zeros((tr, o_ref.shape[1] - cols_in), o_ref.dtype)
    row = pl.program_id(0) * tr + lax.broadcasted_iota(jnp.int32, x.shape, 0)
    o_ref[:, :cols_in] = jnp.where(row < rows_in, x, jnp.zeros_like(x))


def _cast_bf16(w, lead=(), rows_out=None, cols_out=None):
    r, c = w.shape[-2:]
    rows_out = rows_out or r
    cols_out = cols_out or c
    tr = 8
    while tr * 2 * c * 4 <= CAST_BLOCK_BYTES and r % (tr * 2) == 0 and rows_out % (tr * 2) == 0:
        tr *= 2
    assert r % tr == 0 and rows_out % tr == 0 and c % LANES == 0
    n_in = r // tr
    squeeze = (None,) * len(lead)
    blocks = _nbytes((tr, c), F32) + _nbytes((tr, cols_out), BF16)
    return pl.pallas_call(
        functools.partial(_cast_kernel, rows_in=r, cols_in=c, tr=tr),
        grid=(rows_out // tr,),
        in_specs=[pl.BlockSpec(squeeze + (tr, c), lambda i: tuple(lead) + (jnp.minimum(i, n_in - 1), 0))],
        out_specs=pl.BlockSpec((tr, cols_out), lambda i: (i, 0)),
        out_shape=jax.ShapeDtypeStruct((rows_out, cols_out), BF16),
        compiler_params=_params(("parallel",), blocks),
        name="cast_bf16",
    )(w)


def _dot(a, b):
    return jnp.dot(a, b, preferred_element_type=F32)


def _swiglu_kernel(a_ref, wg_ref, wu_ref, o_ref):
    a = a_ref[...]
    g = _dot(a, wg_ref[...])
    u = _dot(a, wu_ref[...])
    o_ref[...] = (g * jax.nn.sigmoid(g) * u).astype(o_ref.dtype)


def _mm_swiglu(a, wg, wu, tm=1024, tn=512):
    m, k = a.shape
    n = wg.shape[1]
    tm, tn = _tile(m, tm), _tile(n, tn)
    blocks = _nbytes((tm, k), BF16) + 2 * _nbytes((k, tn), BF16) + _nbytes((tm, tn), BF16)
    return pl.pallas_call(
        _swiglu_kernel,
        grid=(m // tm, n // tn),
        in_specs=[pl.BlockSpec((tm, k), lambda i, j: (i, 0)),
                  pl.BlockSpec((k, tn), lambda i, j: (0, j)),
                  pl.BlockSpec((k, tn), lambda i, j: (0, j))],
        out_specs=pl.BlockSpec((tm, tn), lambda i, j: (i, j)),
        out_shape=jax.ShapeDtypeStruct((m, n), BF16),
        compiler_params=_params(("parallel", "parallel"), blocks,
                                scratch_bytes=2 * _nbytes((tm, tn), F32)),
        name="ffn_gate_up",
    )(a, wg, wu)


def _residual_kernel(a_ref, b_ref, r_ref, o_ref, acc_ref, *, scale, nk):
    k = pl.program_id(2)
    part = _dot(a_ref[...], b_ref[...])
    if nk == 1:
        o_ref[...] = r_ref[...] + scale * part
        return

    @pl.when(k == 0)
    def _():
        acc_ref[...] = part

    @pl.when(jnp.logical_and(k > 0, k < nk - 1))
    def _():
        acc_ref[...] += part

    @pl.when(k == nk - 1)
    def _():
        o_ref[...] = r_ref[...] + scale * (acc_ref[...] + part)


def _mm_residual(a, b, res, scale, tm=1024, tn=1024, tk_max=4096):
    m, kdim = a.shape
    n = b.shape[1]
    tm, tn = _tile(m, tm), _tile(n, tn)
    nk = 1
    while kdim // nk > tk_max or kdim % nk or (kdim // nk) % LANES:
        nk += 1
    tk = kdim // nk
    blocks = (_nbytes((tm, tk), BF16) + _nbytes((tk, tn), BF16) + 2 * _nbytes((tm, tn), F32))
    return pl.pallas_call(
        functools.partial(_residual_kernel, scale=scale, nk=nk),
        grid=(m // tm, n // tn, nk),
        in_specs=[pl.BlockSpec((tm, tk), lambda i, j, k: (i, k)),
                  pl.BlockSpec((tk, tn), lambda i, j, k: (k, j)),
                  pl.BlockSpec((tm, tn), lambda i, j, k: (i, j))],
        out_specs=pl.BlockSpec((tm, tn), lambda i, j, k: (i, j)),
        out_shape=jax.ShapeDtypeStruct((m, n), F32),
        scratch_shapes=[pltpu.VMEM((tm, tn), F32)],
        compiler_params=_params(("parallel", "parallel", "arbitrary"), blocks,
                                scratch_bytes=2 * _nbytes((tm, tn), F32)),
        name="matmul_residual",
    )(a, b, res)


def _plain_kernel(a_ref, b_ref, o_ref):
    o_ref[...] = _dot(a_ref[...], b_ref[...]).astype(o_ref.dtype)


def _mm_plain(a, b, out_dtype, n=None, col0=0, tm=1024, tn=1024):
    m, k = a.shape
    n = n or b.shape[1]
    tm, tn = _tile(m, tm), _tile(n, tn)
    assert col0 % tn == 0
    joff = col0 // tn
    blocks = _nbytes((tm, k), BF16) + _nbytes((k, tn), BF16) + _nbytes((tm, tn), out_dtype)
    return pl.pallas_call(
        _plain_kernel,
        grid=(m // tm, n // tn),
        in_specs=[pl.BlockSpec((tm, k), lambda i, j: (i, 0)),
                  pl.BlockSpec((k, tn), lambda i, j: (0, j + joff))],
        out_specs=pl.BlockSpec((tm, tn), lambda i, j: (i, j)),
        out_shape=jax.ShapeDtypeStruct((m, n), out_dtype),
        compiler_params=_params(("parallel", "parallel"), blocks,
                                scratch_bytes=_nbytes((tm, tn), F32)),
        name="matmul_plain",
    )(a, b)


def _headnorm_kernel(a_ref, b_ref, g_ref, o_ref, *, head_dim):
    x = _dot(a_ref[...], b_ref[...])
    tn = x.shape[1]
    for h in range(tn // head_dim):
        sl = slice(h * head_dim, (h + 1) * head_dim)
        xh = x[:, sl]
        y = xh * lax.rsqrt(jnp.mean(xh * xh, axis=-1, keepdims=True) + RMS_EPS)
        o_ref[:, sl] = (y * g_ref[:, sl]).astype(o_ref.dtype)


def _mm_headnorm(a, b, gain_row, head_dim, tm=1024, tn=1024):
    m, k = a.shape
    n = gain_row.shape[1]
    tm, tn = _tile(m, tm), _tile(n, tn)
    blocks = (_nbytes((tm, k), BF16) + _nbytes((k, tn), BF16) + _nbytes((tm, tn), BF16)
              + _nbytes((8, tn), F32))
    return pl.pallas_call(
        functools.partial(_headnorm_kernel, head_dim=head_dim),
        grid=(m // tm, n // tn),
        in_specs=[pl.BlockSpec((tm, k), lambda i, j: (i, 0)),
                  pl.BlockSpec((k, tn), lambda i, j: (0, j)),
                  pl.BlockSpec((1, tn), lambda i, j: (0, j))],
        out_specs=pl.BlockSpec((tm, tn), lambda i, j: (i, j)),
        out_shape=jax.ShapeDtypeStruct((m, n), BF16),
        compiler_params=_params(("parallel", "parallel"), blocks,
                                scratch_bytes=_nbytes((tm, tn), F32)),
        name="matmul_headnorm",
    )(a, b, gain_row)


def _mla_down_kernel(a_ref, w_ref, gq_ref, gkv_ref, cq_ref, ckv_ref, kr_ref, *, q_lora, kv_lora):
    x = _dot(a_ref[...], w_ref[...])
    xq = x[:, :q_lora]
    cq = xq * lax.rsqrt(jnp.mean(xq * xq, axis=-1, keepdims=True) + RMS_EPS) * gq_ref[...]
    cq_ref[...] = cq.astype(cq_ref.dtype)
    xkv = x[:, q_lora:q_lora + kv_lora]
    ckv = xkv * lax.rsqrt(jnp.mean(xkv * xkv, axis=-1, keepdims=True) + RMS_EPS) * gkv_ref[...]
    ckv_ref[...] = ckv.astype(ckv_ref.dtype)
    kr_ref[...] = x[:, q_lora + kv_lora:]


def _mla_down(a, wcat, gq, gkv, q_lora, kv_lora, tm=512):
    m, k = a.shape
    n = wcat.shape[1]
    tm = _tile(m, tm)
    blocks = (_nbytes((tm, k), BF16) + _nbytes((k, n), BF16) + _nbytes((tm, q_lora), BF16)
              + _nbytes((tm, kv_lora), BF16) + _nbytes((tm, LANES), F32))
    return pl.pallas_call(
        functools.partial(_mla_down_kernel, q_lora=q_lora, kv_lora=kv_lora),
        grid=(m // tm,),
        in_specs=[pl.BlockSpec((tm, k), lambda i: (i, 0)),
                  pl.BlockSpec((k, n), lambda i: (0, 0)),
                  pl.BlockSpec((1, q_lora), lambda i: (0, 0)),
                  pl.BlockSpec((1, kv_lora), lambda i: (0, 0))],
        out_specs=[pl.BlockSpec((tm, q_lora), lambda i: (i, 0)),
                   pl.BlockSpec((tm, kv_lora), lambda i: (i, 0)),
                   pl.BlockSpec((tm, LANES), lambda i: (i, 0))],
        out_shape=[jax.ShapeDtypeStruct((m, q_lora), BF16),
                   jax.ShapeDtypeStruct((m, kv_lora), BF16),
                   jax.ShapeDtypeStruct((m, LANES), F32)],
        compiler_params=_params(("parallel",), blocks, scratch_bytes=_nbytes((tm, n), F32)),
        name="mla_down",
    )(a, wcat, gq, gkv)


def _rope_lane_tile(x, cos, sin_lo, sin_hi):
    half = QK_ROPE // 2
    return (x * cos + pltpu.roll(x, LANES - half, axis=1) * sin_lo
            + pltpu.roll(x, half, axis=1) * sin_hi)


def _mla_q_kernel(a_ref, w_ref, g_ref, cos_ref, slo_ref, shi_ref, o_ref):
    x = _dot(a_ref[...], w_ref[...])
    cos, slo, shi = cos_ref[...], slo_ref[...], shi_ref[...]
    for h in range(x.shape[1] // QK_PAD):
        xh = x[:, h * QK_PAD:(h + 1) * QK_PAD]
        ms = jnp.sum(xh * xh, axis=-1, keepdims=True) * (1.0 / QK_HEAD)
        y = xh * lax.rsqrt(ms + RMS_EPS) * g_ref[...]
        o_ref[:, h * QK_PAD:h * QK_PAD + LANES] = y[:, :LANES].astype(o_ref.dtype)
        o_ref[:, h * QK_PAD + LANES:(h + 1) * QK_PAD] = _rope_lane_tile(
            y[:, LANES:], cos, slo, shi).astype(o_ref.dtype)


def _mla_q(cq, wuq_p, gain_p, tabs, seq, tm=1024, tn=1024):
    m, k = cq.shape
    n = wuq_p.shape[1]
    tm, tn = _tile(seq, tm), _tile(n, tn)
    nsb = seq // tm
    tab_spec = pl.BlockSpec((tm, LANES), lambda i, j: (i % nsb, 0))
    blocks = (_nbytes((tm, k), BF16) + _nbytes((k, tn), BF16) + _nbytes((tm, tn), BF16)
              + 3 * _nbytes((tm, LANES), F32))
    return pl.pallas_call(
        _mla_q_kernel,
        grid=(m // tm, n // tn),
        in_specs=[pl.BlockSpec((tm, k), lambda i, j: (i, 0)),
                  pl.BlockSpec((k, tn), lambda i, j: (0, j)),
                  pl.BlockSpec((1, QK_PAD), lambda i, j: (0, 0)),
                  tab_spec, tab_spec, tab_spec],
        out_specs=pl.BlockSpec((tm, tn), lambda i, j: (i, j)),
        out_shape=jax.ShapeDtypeStruct((m, n), BF16),
        compiler_params=_params(("parallel", "parallel"), blocks,
                                scratch_bytes=_nbytes((tm, tn), F32)),
        name="mla_q_proj",
    )(cq, wuq_p, gain_p, *tabs)


FLASH_BQ = 1024
FLASH_BK = 512


def _mla_kv_kernel(a_ref, w_ref, kr_ref, gn_ref, gr_ref, cos_ref, slo_ref, shi_ref, k_ref, vt_ref, *, bk):
    x = _dot(a_ref[...], w_ref[...])
    kr = kr_ref[...]
    ss_rope = jnp.sum(kr * kr, axis=-1, keepdims=True)
    kr_roped = _rope_lane_tile(kr * gr_ref[...], cos_ref[...], slo_ref[...], shi_ref[...])
    width = QK_NOPE + V_HEAD
    for h in range(x.shape[1] // width):
        kn = x[:, h * width:h * width + QK_NOPE]
        ms = (jnp.sum(kn * kn, axis=-1, keepdims=True) + ss_rope) * (1.0 / QK_HEAD)
        rs = lax.rsqrt(ms + RMS_EPS)
        k_ref[:, h * QK_PAD:h * QK_PAD + LANES] = (kn * rs * gn_ref[...]).astype(k_ref.dtype)
        k_ref[:, h * QK_PAD + LANES:(h + 1) * QK_PAD] = (kr_roped * rs).astype(k_ref.dtype)
        for c in range(x.shape[0] // bk):
            v = x[c * bk:(c + 1) * bk, h * width + QK_NOPE:(h + 1) * width]
            vt_ref[h, c] = v.T.astype(vt_ref.dtype)


def _mla_kv(ckv, wukv, k_rope, g_nope, g_rope, tabs, bsz, seq, bk, tm=1024, tn=1024):
    m, k = ckv.shape
    n = wukv.shape[1]
    width = QK_NOPE + V_HEAD
    tm, tn = _tile(seq, tm), _tile(n, tn)
    assert tm % bk == 0
    hpt = tn // width
    nsb = seq // tm
    tab_spec = pl.BlockSpec((tm, LANES), lambda i, j: (i % nsb, 0))
    blocks = (_nbytes((tm, k), BF16) + _nbytes((k, tn), BF16) + _nbytes((tm, hpt * QK_PAD), BF16)
              + _nbytes((tm, hpt * V_HEAD), BF16) + 4 * _nbytes((tm, LANES), F32))
    return pl.pallas_call(
        functools.partial(_mla_kv_kernel, bk=bk),
        grid=(m // tm, n // tn),
        in_specs=[pl.BlockSpec((tm, k), lambda i, j: (i, 0)),
                  pl.BlockSpec((k, tn), lambda i, j: (0, j)),
                  pl.BlockSpec((tm, LANES), lambda i, j: (i, 0)),
                  pl.BlockSpec((1, LANES), lambda i, j: (0, 0)),
                  pl.BlockSpec((1, LANES), lambda i, j: (0, 0)),
                  tab_spec, tab_spec, tab_spec],
        out_specs=[pl.BlockSpec((tm, hpt * QK_PAD), lambda i, j: (i, j)),
                   pl.BlockSpec((None, hpt, tm // bk, V_HEAD, bk), lambda i, j: (i // nsb, j, i % nsb, 0, 0))],
        out_shape=[jax.ShapeDtypeStruct((m, (n // width) * QK_PAD), BF16),
                   jax.ShapeDtypeStruct((bsz, n // width, seq // bk, V_HEAD, bk), BF16)],
        compiler_params=_params(("parallel", "parallel"), blocks,
                                scratch_bytes=2 * _nbytes((tm, tn), F32)),
        name="mla_kv_proj",
    )(ckv, wukv, k_rope, g_nope, g_rope, *tabs)


def _dot_nt(a, b):
    return lax.dot_general(a, b, (((1,), (1,)), ((), ())), preferred_element_type=F32)


def _flash_kernel(q_ref, k_ref, vt_ref, o_ref, m_ref, l_ref, acc_ref, s_ref, *, bq, bk):
    qi = pl.program_id(2)
    m_ref[...] = jnp.full(m_ref.shape, MASK_VALUE, F32)
    l_ref[...] = jnp.zeros(l_ref.shape, F32)
    acc_ref[...] = jnp.zeros(acc_ref.shape, F32)

    def scores(j, slot):
        kb = k_ref[pl.ds(pl.multiple_of(j * bk, bk), bk), :]
        s_ref[slot] = _dot_nt(kb, q_ref[...])

    def softmax_pv(j, slot, masked):
        st = s_ref[slot]
        if masked:
            kpos = j * bk + lax.broadcasted_iota(jnp.int32, st.shape, 0)
            qpos = qi * bq + lax.broadcasted_iota(jnp.int32, st.shape, 1)
            st = jnp.where(kpos <= qpos, st, MASK_VALUE)
        m_old = m_ref[...]
        m_new = jnp.maximum(m_old, jnp.max(st, axis=0, keepdims=True))
        p = jnp.exp2(st - m_new)
        alpha = jnp.exp2(m_old - m_new)
        l_ref[...] = alpha * l_ref[...] + jnp.sum(p, axis=0, keepdims=True)
        acc_ref[...] = alpha * acc_ref[...] + _dot(vt_ref[j], p.astype(BF16))
        m_ref[...] = m_new

    def body(i, carry):
        scores(2 * i + 1, 1)
        softmax_pv(2 * i, 0, masked=False)
        scores(2 * i + 2, 0)
        softmax_pv(2 * i + 1, 1, masked=False)
        return carry

    scores(0, 0)
    lax.fori_loop(0, qi, body, 0)
    scores(2 * qi + 1, 1)
    softmax_pv(2 * qi, 0, masked=True)
    softmax_pv(2 * qi + 1, 1, masked=True)
    o_ref[...] = (acc_ref[...] / l_ref[...]).T.astype(o_ref.dtype)


def _mla_flash(q, k, vt, bq, bk):
    b, s, _ = q.shape
    h = q.shape[2] // QK_PAD
    assert s % bq == 0 and bq == 2 * bk
    blocks = (_nbytes((bq, QK_PAD), BF16) + _nbytes((s, QK_PAD), BF16) + _nbytes((s, V_HEAD), BF16)
              + _nbytes((bq, V_HEAD), BF16))
    scratch = 2 * _nbytes((8, bq), F32) + _nbytes((V_HEAD, bq), F32) + 6 * _nbytes((bk, bq), F32)
    return pl.pallas_call(
        functools.partial(_flash_kernel, bq=bq, bk=bk),
        grid=(b, h, s // bq),
        in_specs=[pl.BlockSpec((None, bq, QK_PAD), lambda bi, hi, qi: (bi, qi, hi)),
                  pl.BlockSpec((None, s, QK_PAD), lambda bi, hi, qi: (bi, 0, hi)),
                  pl.BlockSpec((None, None, s // bk, V_HEAD, bk), lambda bi, hi, qi: (bi, hi, 0, 0, 0))],
        out_specs=pl.BlockSpec((None, bq, V_HEAD), lambda bi, hi, qi: (bi, qi, hi)),
        out_shape=jax.ShapeDtypeStruct((b, s, h * V_HEAD), BF16),
        scratch_shapes=[pltpu.VMEM((1, bq), F32), pltpu.VMEM((1, bq), F32),
                        pltpu.VMEM((V_HEAD, bq), F32), pltpu.VMEM((2, bk, bq), F32)],
        compiler_params=_params(("parallel", "parallel", "arbitrary"), blocks, scratch_bytes=scratch),
        name="mla_flash_attention",
    )(q, k, vt)


def _t5_causal_bucket(dist):
    max_exact = NUM_BUCKETS // 2
    n = jnp.maximum(dist, 0)
    nf = jnp.maximum(n, 1).astype(F32)
    large = max_exact + (jnp.log(nf / max_exact) / math.log(MAX_DISTANCE / max_exact)
                         * (NUM_BUCKETS - max_exact)).astype(jnp.int32)
    large = jnp.minimum(large, NUM_BUCKETS - 1)
    return jnp.where(n < max_exact, n, large)


def _band_bias_kernel(tab_ref, bucket_ref, o_ref, *, heads):
    g = pl.program_id(0)
    h = pl.program_id(1)
    bucket = bucket_ref[...]
    acc = jnp.full(bucket.shape, MASK_VALUE, F32)
    for b in range(NUM_BUCKETS):
        acc = jnp.where(bucket == b, tab_ref[b, g * heads + h], acc)
    o_ref[...] = acc


def _band_bias(rel_bias, buckets, heads):
    g = buckets.shape[0]
    blk = buckets.shape[1:]
    blocks = _nbytes(blk, jnp.int32) + _nbytes(blk, F32)
    return pl.pallas_call(
        functools.partial(_band_bias_kernel, heads=heads),
        grid=(g, heads),
        in_specs=[pl.BlockSpec(memory_space=pltpu.SMEM),
                  pl.BlockSpec((None,) + blk, lambda gi, hi: (gi, 0, 0))],
        out_specs=pl.BlockSpec((None, None) + blk, lambda gi, hi: (gi, hi, 0, 0)),
        out_shape=jax.ShapeDtypeStruct((g, heads) + blk, F32),
        compiler_params=_params(("parallel", "parallel"), blocks),
        name="dilated_band_bias",
    )(rel_bias, buckets)


DIL_HEAD_GROUP = 4


def _dilated_kernel(q_ref, kp_ref, kc_ref, vp_ref, vc_ref, bias_ref, o_ref, lse_ref, s_ref, *, scale, heads):
    n = pl.program_id(2)
    has_prev = n > 0
    d = DIL_HEAD_DIM
    hg = s_ref.shape[1]
    lane = lax.broadcasted_iota(jnp.int32, (Q_BLOCK, LANES), 1)
    col = lax.broadcasted_iota(jnp.int32, (Q_BLOCK, 2 * Q_BLOCK), 1)
    keep = jnp.logical_or(has_prev, col >= Q_BLOCK)

    def scores(grp, slot):
        for i in range(hg):
            sl = slice((grp * hg + i) * d, (grp * hg + i + 1) * d)
            kcat = jnp.concatenate([kp_ref[:, sl], kc_ref[:, sl]], axis=0)
            s_ref[slot, i] = _dot_nt(q_ref[:, sl], kcat)

    def softmax_pv(grp, slot, lse_tile):
        for i in range(hg):
            h = grp * hg + i
            sl = slice(h * d, (h + 1) * d)
            s = jnp.where(keep, s_ref[slot, i] * scale + bias_ref[h], MASK_VALUE)
            m = jnp.max(s, axis=-1, keepdims=True)
            p = jnp.exp(s - m)
            den = jnp.sum(p, axis=-1, keepdims=True)
            vcat = jnp.concatenate([vp_ref[:, sl], vc_ref[:, sl]], axis=0)
            o_ref[:, sl] = _dot((p / den).astype(BF16), vcat).astype(o_ref.dtype)
            lse_tile = jnp.where(lane == h, m + jnp.log(den), lse_tile)
        return lse_tile

    lse_tile = jnp.zeros((Q_BLOCK, LANES), F32)
    n_grp = heads // hg
    scores(0, 0)
    for grp in range(n_grp):
        if grp + 1 < n_grp:
            scores(grp + 1, (grp + 1) % 2)
        lse_tile = softmax_pv(grp, grp % 2, lse_tile)
    lse_ref[...] = lse_tile


def _dilated_group(q, k, v, bias_g, g, dilation):
    b, s, width = q.shape
    hd = DIL_HEADS * DIL_HEAD_DIM
    ng = width // hd
    hg = math.gcd(DIL_HEAD_GROUP, DIL_HEADS)
    length = s // dilation
    nb = length // Q_BLOCK
    qv = q.reshape(b, length, dilation * width)
    kv_ = k.reshape(b, length, dilation * width)
    vv = v.reshape(b, length, dilation * width)
    cur = lambda bi, r, n: (bi, n, r * ng + g)
    prev = lambda bi, r, n: (bi, jnp.maximum(n - 1, 0), r * ng + g)
    blk = (None, Q_BLOCK, hd)
    blocks = 6 * _nbytes((Q_BLOCK, hd), BF16) + _nbytes(bias_g.shape, F32) + _nbytes((Q_BLOCK, LANES), F32)
    o, lse = pl.pallas_call(
        functools.partial(_dilated_kernel, scale=DIL_HEAD_DIM ** -0.5, heads=DIL_HEADS),
        grid=(b, dilation, nb),
        in_specs=[pl.BlockSpec(blk, cur),
                  pl.BlockSpec(blk, prev), pl.BlockSpec(blk, cur),
                  pl.BlockSpec(blk, prev), pl.BlockSpec(blk, cur),
                  pl.BlockSpec(bias_g.shape, lambda bi, r, n: (0, 0, 0))],
        out_specs=[pl.BlockSpec(blk, lambda bi, r, n: (bi, n, r)),
                   pl.BlockSpec((None, Q_BLOCK, LANES), lambda bi, r, n: (bi, n, r))],
        out_shape=[jax.ShapeDtypeStruct((b, length, dilation * hd), BF16),
                   jax.ShapeDtypeStruct((b, length, dilation * LANES), F32)],
        scratch_shapes=[pltpu.VMEM((2, hg, Q_BLOCK, 2 * Q_BLOCK), F32)],
        compiler_params=_params(("parallel", "parallel", "arbitrary"), blocks,
                                scratch_bytes=2 * hg * _nbytes((Q_BLOCK, 2 * Q_BLOCK), F32)),
        name=f"dilated_attention_d{dilation}",
    )(qv, kv_, kv_, vv, vv, bias_g)
    return o.reshape(b, s, hd), lse.reshape(b, s, LANES)


def _combine_kernel(o0_ref, o1_ref, o2_ref, l0_ref, l1_ref, l2_ref, o_ref, *, heads):
    l0, l1, l2 = l0_ref[...], l1_ref[...], l2_ref[...]
    m = jnp.maximum(jnp.maximum(l0, l1), l2)
    e0, e1, e2 = jnp.exp(l0 - m), jnp.exp(l1 - m), jnp.exp(l2 - m)
    tot = e0 + e1 + e2
    w0, w1, w2 = e0 / tot, e1 / tot, e2 / tot
    d = DIL_HEAD_DIM
    for h in range(heads):
        sl = slice(h * d, (h + 1) * d)
        o = (w0[:, h:h + 1] * o0_ref[:, sl].astype(F32) + w1[:, h:h + 1] * o1_ref[:, sl].astype(F32)
             + w2[:, h:h + 1] * o2_ref[:, sl].astype(F32))
        o_ref[:, sl] = o.astype(o_ref.dtype)


def _combine_groups(outs, lses, tm=512):
    m, hd = outs[0].shape
    tm = _tile(m, tm)
    blocks = 4 * _nbytes((tm, hd), BF16) + 3 * _nbytes((tm, LANES), F32)
    o_spec = pl.BlockSpec((tm, hd), lambda i: (i, 0))
    l_spec = pl.BlockSpec((tm, LANES), lambda i: (i, 0))
    return pl.pallas_call(
        functools.partial(_combine_kernel, heads=DIL_HEADS),
        grid=(m // tm,),
        in_specs=[o_spec, o_spec, o_spec, l_spec, l_spec, l_spec],
        out_specs=o_spec,
        out_shape=jax.ShapeDtypeStruct((m, hd), BF16),
        compiler_params=_params(("parallel",), blocks),
        name="dilated_combine",
    )(*outs, *lses)


GATE_UP_TILE = 512


def _rope_tables(seq):
    half = QK_ROPE // 2
    inv = ROPE_THETA ** (-jnp.arange(half, dtype=F32) / half)
    ang = jnp.arange(seq).astype(F32)[:, None] * inv[None, :]
    cos, sin = jnp.cos(ang), jnp.sin(ang)
    zeros = jnp.zeros_like(cos)
    cos_t = jnp.concatenate([cos, cos, zeros, zeros], axis=1)
    sin_lo = jnp.concatenate([-sin, zeros, zeros, zeros], axis=1)
    sin_hi = jnp.concatenate([zeros, sin, zeros, zeros], axis=1)
    return cos_t, sin_lo, sin_hi


def _band_buckets():
    r = jnp.arange(Q_BLOCK)[:, None]
    c = jnp.arange(2 * Q_BLOCK)[None, :]
    steps = Q_BLOCK + r - c
    out = []
    for window, dilation in DIL_GROUPS:
        band = (steps >= 0) & (steps <= window // dilation)
        out.append(jnp.where(band, _t5_causal_bucket(steps * dilation), -1))
    return jnp.stack(out).astype(jnp.int32)


def _ffn(h, xn, ffn_wg, ffn_wu, ffn_wd, lead):
    f = ffn_wg.shape[-1]
    fp = -(-f // GATE_UP_TILE) * GATE_UP_TILE
    wg = _cast_bf16(ffn_wg, lead, cols_out=fp)
    wu = _cast_bf16(ffn_wu, lead, cols_out=fp)
    wd = _cast_bf16(ffn_wd, lead, rows_out=fp)
    act = _mm_swiglu(xn, wg, wu, tn=GATE_UP_TILE)
    return _mm_residual(act, wd, h, 0.5)


def kernel(x, ffn_norm, ffn_wg, ffn_wu, ffn_wd, attn_norm, mla_wdq, mla_q_lora_norm, mla_wuq, mla_wdkv, mla_kv_lora_norm, mla_wukv, mla_q_norm, mla_k_norm, mla_wo, kv_src_norm, w_kv_shared, k_norm_shared, rel_bias, dil_wq, dil_q_norm, dil_wo):
    bsz, seq, d = x.shape
    m = bsz * seq
    depth = ffn_norm.shape[0]
    n_a = depth // 2
    h = x.reshape(m, d)
    tabs = _rope_tables(seq)
    hd = DIL_HEADS * DIL_HEAD_DIM
    k_sh = v_sh = None
    bias_all = None

    for window, dilation in DIL_GROUPS:
        assert seq % (dilation * Q_BLOCK) == 0 and window // dilation <= Q_BLOCK
    bq, bk = min(FLASH_BQ, seq), min(FLASH_BK, seq)

    for l in range(depth):
        if l == n_a:
            (xs,) = _rmsnorm(h, kv_src_norm[None, :])
            wkv = _cast_bf16(w_kv_shared)
            gk = jnp.broadcast_to(k_norm_shared[:, None, :], (N_GROUPS, DIL_HEADS, DIL_HEAD_DIM)).reshape(1, -1)
            k_sh = _mm_headnorm(xs, wkv, gk, DIL_HEAD_DIM).reshape(bsz, seq, -1)
            v_sh = _mm_plain(xs, wkv, BF16, n=N_GROUPS * hd, col0=N_GROUPS * hd).reshape(bsz, seq, -1)
            bias_all = _band_bias(rel_bias, _band_buckets(), DIL_HEADS)

        (xn,) = _rmsnorm(h, ffn_norm[l, 0][None, :])
        h = _ffn(h, xn, ffn_wg, ffn_wu, ffn_wd, (l, 0))

        (xn,) = _rmsnorm(h, attn_norm[l][None, :])
        if l < n_a:
            a = l
            q_lora = mla_wdq.shape[2]
            kv_lora = mla_wdkv.shape[2] - QK_ROPE
            wcat = jnp.concatenate(
                [mla_wdq[a], mla_wdkv[a], jnp.zeros((d, LANES - QK_ROPE), F32)], axis=1).astype(BF16)
            cq, ckv, k_rope = _mla_down(xn, wcat, mla_q_lora_norm[a][None, :],
                                        mla_kv_lora_norm[a][None, :], q_lora, kv_lora)
            wuq_p = jnp.pad(mla_wuq[a].reshape(q_lora, MLA_HEADS, QK_HEAD),
                            ((0, 0), (0, 0), (0, QK_PAD - QK_HEAD))).reshape(q_lora, -1).astype(BF16)
            q_scale = QK_HEAD ** -0.5 * math.log2(math.e)
            gq_p = jnp.pad(mla_q_norm[a] * q_scale, (0, QK_PAD - QK_HEAD))[None, :]
            q = _mla_q(cq, wuq_p, gq_p, tabs, seq)
            g_nope = mla_k_norm[a][:QK_NOPE][None, :]
            g_rope = jnp.pad(mla_k_norm[a][QK_NOPE:], (0, LANES - QK_ROPE))[None, :]
            k, vt = _mla_kv(ckv, _cast_bf16(mla_wukv, (a,)), k_rope, g_nope, g_rope, tabs, bsz, seq, bk)
            o = _mla_flash(q.reshape(bsz, seq, -1), k.reshape(bsz, seq, -1), vt, bq, bk)
            h = _mm_residual(o.reshape(m, -1), _cast_bf16(mla_wo, (a,)), h, 1.0)
        else:
            bl = l - n_a
            gq = jnp.broadcast_to(dil_q_norm[bl][:, None, :], (N_GROUPS, DIL_HEADS, DIL_HEAD_DIM)).reshape(1, -1)
            q = _mm_headnorm(xn, _cast_bf16(dil_wq, (bl,)), gq, DIL_HEAD_DIM).reshape(bsz, seq, -1)
            outs, lses = [], []
            for g, (_, dilation) in enumerate(DIL_GROUPS):
                o_g, l_g = _dilated_group(q, k_sh, v_sh, bias_all[g], g, dilation)
                outs.append(o_g.reshape(m, hd))
                lses.append(l_g.reshape(m, LANES))
            o = _combine_groups(outs, lses)
            h = _mm_residual(o, _cast_bf16(dil_wo, (bl,)), h, 1.0)

        (xn,) = _rmsnorm(h, ffn_norm[l, 1][None, :])
        h = _ffn(h, xn, ffn_wg, ffn_wu, ffn_wd, (l, 1))

    return h.reshape(bsz, seq, d)
```

```python
import functools
import math

import jax
import jax.numpy as jnp
import numpy as np
from jax import lax
from jax.experimental import pallas as pl
from jax.experimental.pallas import tpu as pltpu

F32 = jnp.float32
BF16 = jnp.bfloat16

RMS_EPS = 1e-6
MLA_HEADS = 32
QK_NOPE = 128
QK_ROPE = 64
QK_HEAD = QK_NOPE + QK_ROPE
V_HEAD = 128
ROPE_THETA = 10000.0
DIL_GROUPS = ((128, 1), (512, 4), (2048, 16))
N_GROUPS = 3
DIL_HEADS = 32
DIL_HEAD_DIM = 128
NUM_BUCKETS = 32
MAX_DISTANCE = 2048
Q_BLOCK = 128

LANES = 128
QK_PAD = 2 * LANES
V7X_VMEM_BYTES = 64 * 1024 * 1024
VMEM_HEADROOM_BYTES = 8 * 1024 * 1024
MASK_VALUE = -1e30


def _nbytes(shape, dtype):
    return int(np.prod(shape)) * jnp.dtype(dtype).itemsize


def _params(semantics, block_bytes, scratch_bytes=0):
    need = 2 * block_bytes + scratch_bytes + VMEM_HEADROOM_BYTES
    limit = min(max(need, 32 * 1024 * 1024), V7X_VMEM_BYTES - 4 * 1024 * 1024)
    return pltpu.CompilerParams(dimension_semantics=semantics, vmem_limit_bytes=int(limit))


def _tile(dim, pref):
    if dim <= pref:
        return dim
    t = pref
    while dim % t:
        t //= 2
    return t


def _gather_perm(tm, dil):
    n = tm // dil
    i = np.arange(tm)
    p = np.zeros((tm, tm), np.float32)
    p[i, (i % n) * dil + i // n] = 1.0
    return jnp.asarray(p, BF16)


def _rmsnorm_kernel(x_ref, g_ref, *refs, dilations):
    n_perm = sum(dil > 1 for dil in dilations)
    p_refs, o_refs = refs[:n_perm], refs[n_perm:]
    x = x_ref[...]
    y = x * lax.rsqrt(jnp.mean(x * x, axis=-1, keepdims=True) + RMS_EPS)
    tm = x.shape[0]
    ip = 0
    for i, (o_ref, dil) in enumerate(zip(o_refs, dilations)):
        yg = (y * g_ref[i:i + 1, :]).astype(o_ref.dtype)
        if dil == 1:
            o_ref[...] = yg
        else:
            yp = _dot(p_refs[ip][...], yg).astype(o_ref.dtype)
            ip += 1
            n = tm // dil
            for r in range(dil):
                o_ref[r] = yp[r * n:(r + 1) * n, :]


def _rmsnorm(x, gains, dilations, bsz, seq, tm=256):
    m, d = x.shape
    n = gains.shape[0]
    tm = _tile(seq, tm)
    nsb = seq // tm
    out_specs, out_shape, perms = [], [], []
    for dil in dilations:
        if dil == 1:
            out_specs.append(pl.BlockSpec((tm, d), lambda i: (i, 0)))
            out_shape.append(jax.ShapeDtypeStruct((m, d), BF16))
        else:
            assert tm % (16 * dil) == 0
            perms.append(_gather_perm(tm, dil))
            out_specs.append(pl.BlockSpec((None, dil, tm // dil, d), lambda i: (i // nsb, 0, i % nsb, 0)))
            out_shape.append(jax.ShapeDtypeStruct((bsz, dil, seq // dil, d), BF16))
    blocks = (_nbytes((tm, d), F32) + n * _nbytes((tm, d), BF16) + _nbytes((n, d), F32)
              + len(perms) * _nbytes((tm, tm), BF16))
    return pl.pallas_call(
        functools.partial(_rmsnorm_kernel, dilations=tuple(dilations)),
        grid=(m // tm,),
        in_specs=[pl.BlockSpec((tm, d), lambda i: (i, 0)),
                  pl.BlockSpec((n, d), lambda i: (0, 0))]
                 + [pl.BlockSpec((tm, tm), lambda i: (0, 0)) for _ in perms],
        out_specs=out_specs,
        out_shape=out_shape,
        compiler_params=_params(("parallel",), blocks, scratch_bytes=3 * _nbytes((tm, d), F32)),
        name="rmsnorm",
    )(x, gains, *perms)


CAST_BLOCK_BYTES = 12 * 1024 * 1024


def _cast_kernel(x_ref, o_ref, *, rows_in, cols_in, tr):
    x = x_ref[...].astype(o_ref.dtype)
    if o_ref.shape[1] > cols_in:
        o_ref[:, cols_in:] = jnp.zeros((tr, o_ref.shape[1] - cols_in), o_ref.dtype)
    row = pl.program_id(0) * tr + lax.broadcasted_iota(jnp.int32, x.shape, 0)
    o_ref[:, :cols_in] = jnp.where(row < rows_in, x, jnp.zeros_like(x))


def _cast_bf16(w, lead=(), rows_out=None, cols_out=None):
    r, c = w.shape[-2:]
    rows_out = rows_out or r
    cols_out = cols_out or c
    tr = 8
    while tr * 2 * c * 4 <= CAST_BLOCK_BYTES and r % (tr * 2) == 0 and rows_out % (tr * 2) == 0:
        tr *= 2
    assert r % tr == 0 and rows_out % tr == 0 and c % LANES == 0
    n_in = r // tr
    squeeze = (None,) * len(lead)
    blocks = _nbytes((tr, c), F32) + _nbytes((tr, cols_out), BF16)
    return pl.pallas_call(
        functools.partial(_cast_kernel, rows_in=r, cols_in=c, tr=tr),
        grid=(rows_out // tr,),
        in_specs=[pl.BlockSpec(squeeze + (tr, c), lambda i: tuple(lead) + (jnp.minimum(i, n_in - 1), 0))],
        out_specs=pl.BlockSpec((tr, cols_out), lambda i: (i, 0)),
        out_shape=jax.ShapeDtypeStruct((rows_out, cols_out), BF16),
        compiler_params=_params(("parallel",), blocks),
        name="cast_bf16",
    )(w)


def _dot(a, b):
    return jnp.dot(a, b, preferred_element_type=F32)


def _swiglu_kernel(a_ref, wg_ref, wu_ref, o_ref):
    a = a_ref[...]
    g = _dot(a, wg_ref[...])
    u = _dot(a, wu_ref[...])
    o_ref[...] = (g * jax.nn.sigmoid(g) * u).astype(o_ref.dtype)


def _mm_swiglu(a, wg, wu, tm=1024, tn=512):
    m, k = a.shape
    n = wg.shape[1]
    tm, tn = _tile(m, tm), _tile(n, tn)
    blocks = _nbytes((tm, k), BF16) + 2 * _nbytes((k, tn), BF16) + _nbytes((tm, tn), BF16)
    return pl.pallas_call(
        _swiglu_kernel,
        grid=(m // tm, n // tn),
        in_specs=[pl.BlockSpec((tm, k), lambda i, j: (i, 0)),
                  pl.BlockSpec((k, tn), lambda i, j: (0, j)),
                  pl.BlockSpec((k, tn), lambda i, j: (0, j))],
        out_specs=pl.BlockSpec((tm, tn), lambda i, j: (i, j)),
        out_shape=jax.ShapeDtypeStruct((m, n), BF16),
        compiler_params=_params(("parallel", "parallel"), blocks,
                                scratch_bytes=2 * _nbytes((tm, tn), F32)),
        name="ffn_gate_up",
    )(a, wg, wu)


def _residual_kernel(a_ref, b_ref, r_ref, o_ref, acc_ref, *, scale, nk):
    k = pl.program_id(2)
    part = _dot(a_ref[...], b_ref[...])
    if nk == 1:
        o_ref[...] = r_ref[...] + scale * part
        return

    @pl.when(k == 0)
    def _():
        acc_ref[...] = part

    @pl.when(jnp.logical_and(k > 0, k < nk - 1))
    def _():
        acc_ref[...] += part

    @pl.when(k == nk - 1)
    def _():
        o_ref[...] = r_ref[...] + scale * (acc_ref[...] + part)


def _mm_residual(a, b, res, scale, tm=1024, tn=1024, tk_max=4096):
    m, kdim = a.shape
    n = b.shape[1]
    tm, tn = _tile(m, tm), _tile(n, tn)
    nk = 1
    while kdim // nk > tk_max or kdim % nk or (kdim // nk) % LANES:
        nk += 1
    tk = kdim // nk
    blocks = (_nbytes((tm, tk), BF16) + _nbytes((tk, tn), BF16) + 2 * _nbytes((tm, tn), F32))
    return pl.pallas_call(
        functools.partial(_residual_kernel, scale=scale, nk=nk),
        grid=(m // tm, n // tn, nk),
        in_specs=[pl.BlockSpec((tm, tk), lambda i, j, k: (i, k)),
                  pl.BlockSpec((tk, tn), lambda i, j, k: (k, j)),
                  pl.BlockSpec((tm, tn), lambda i, j, k: (i, j))],
        out_specs=pl.BlockSpec((tm, tn), lambda i, j, k: (i, j)),
        out_shape=jax.ShapeDtypeStruct((m, n), F32),
        scratch_shapes=[pltpu.VMEM((tm, tn), F32)],
        compiler_params=_params(("parallel", "parallel", "arbitrary"), blocks,
                                scratch_bytes=2 * _nbytes((tm, tn), F32)),
        name="matmul_residual",
    )(a, b, res)


def _plain_kernel(a_ref, b_ref, o_ref):
    o_ref[...] = _dot(a_ref[...], b_ref[...]).astype(o_ref.dtype)


def _mm_plain(a, b, out_dtype, n=None, col0=0, tm=1024, tn=1024):
    m, k = a.shape
    n = n or b.shape[1]
    tm, tn = _tile(m, tm), _tile(n, tn)
    assert col0 % tn == 0
    joff = col0 // tn
    blocks = _nbytes((tm, k), BF16) + _nbytes((k, tn), BF16) + _nbytes((tm, tn), out_dtype)
    return pl.pallas_call(
        _plain_kernel,
        grid=(m // tm, n // tn),
        in_specs=[pl.BlockSpec((tm, k), lambda i, j: (i, 0)),
                  pl.BlockSpec((k, tn), lambda i, j: (0, j + joff))],
        out_specs=pl.BlockSpec((tm, tn), lambda i, j: (i, j)),
        out_shape=jax.ShapeDtypeStruct((m, n), out_dtype),
        compiler_params=_params(("parallel", "parallel"), blocks,
                                scratch_bytes=_nbytes((tm, tn), F32)),
        name="matmul_plain",
    )(a, b)


def _headnorm_kernel(a_ref, b_ref, g_ref, o_ref, *, head_dim):
    x = _dot(a_ref[...], b_ref[...])
    tn = x.shape[1]
    for h in range(tn // head_dim):
        sl = slice(h * head_dim, (h + 1) * head_dim)
        xh = x[:, sl]
        y = xh * lax.rsqrt(jnp.mean(xh * xh, axis=-1, keepdims=True) + RMS_EPS)
        o_ref[:, sl] = (y * g_ref[:, sl]).astype(o_ref.dtype)


def _mm_headnorm(a, b, gain_row, head_dim, col0=0, tm=1024, tn=1024):
    m, k = a.shape
    n = gain_row.shape[1]
    tm, tn = _tile(m, tm), _tile(n, tn)
    assert col0 % tn == 0
    joff = col0 // tn
    blocks = (_nbytes((tm, k), BF16) + _nbytes((k, tn), BF16) + _nbytes((tm, tn), BF16)
              + _nbytes((8, tn), F32))
    return pl.pallas_call(
        functools.partial(_headnorm_kernel, head_dim=head_dim),
        grid=(m // tm, n // tn),
        in_specs=[pl.BlockSpec((tm, k), lambda i, j: (i, 0)),
                  pl.BlockSpec((k, tn), lambda i, j: (0, j + joff)),
                  pl.BlockSpec((1, tn), lambda i, j: (0, j))],
        out_specs=pl.BlockSpec((tm, tn), lambda i, j: (i, j)),
        out_shape=jax.ShapeDtypeStruct((m, n), BF16),
        compiler_params=_params(("parallel", "parallel"), blocks,
                                scratch_bytes=_nbytes((tm, tn), F32)),
        name="matmul_headnorm",
    )(a, b, gain_row)


def _mla_down_kernel(a_ref, w_ref, gq_ref, gkv_ref, cq_ref, ckv_ref, kr_ref, *, q_lora, kv_lora):
    x = _dot(a_ref[...], w_ref[...])
    xq = x[:, :q_lora]
    cq = xq * lax.rsqrt(jnp.mean(xq * xq, axis=-1, keepdims=True) + RMS_EPS) * gq_ref[...]
    cq_ref[...] = cq.astype(cq_ref.dtype)
    xkv = x[:, q_lora:q_lora + kv_lora]
    ckv = xkv * lax.rsqrt(jnp.mean(xkv * xkv, axis=-1, keepdims=True) + RMS_EPS) * gkv_ref[...]
    ckv_ref[...] = ckv.astype(ckv_ref.dtype)
    kr_ref[...] = x[:, q_lora + kv_lora:]


def _mla_down(a, wcat, gq, gkv, q_lora, kv_lora, tm=512):
    m, k = a.shape
    n = wcat.shape[1]
    tm = _tile(m, tm)
    blocks = (_nbytes((tm, k), BF16) + _nbytes((k, n), BF16) + _nbytes((tm, q_lora), BF16)
              + _nbytes((tm, kv_lora), BF16) + _nbytes((tm, LANES), F32))
    return pl.pallas_call(
        functools.partial(_mla_down_kernel, q_lora=q_lora, kv_lora=kv_lora),
        grid=(m // tm,),
        in_specs=[pl.BlockSpec((tm, k), lambda i: (i, 0)),
                  pl.BlockSpec((k, n), lambda i: (0, 0)),
                  pl.BlockSpec((1, q_lora), lambda i: (0, 0)),
                  pl.BlockSpec((1, kv_lora), lambda i: (0, 0))],
        out_specs=[pl.BlockSpec((tm, q_lora), lambda i: (i, 0)),
                   pl.BlockSpec((tm, kv_lora), lambda i: (i, 0)),
                   pl.BlockSpec((tm, LANES), lambda i: (i, 0))],
        out_shape=[jax.ShapeDtypeStruct((m, q_lora), BF16),
                   jax.ShapeDtypeStruct((m, kv_lora), BF16),
                   jax.ShapeDtypeStruct((m, LANES), F32)],
        compiler_params=_params(("parallel",), blocks, scratch_bytes=_nbytes((tm, n), F32)),
        name="mla_down",
    )(a, wcat, gq, gkv)


def _rope_lane_tile(x, cos, sin_lo, sin_hi):
    half = QK_ROPE // 2
    return (x * cos + pltpu.roll(x, LANES - half, axis=1) * sin_lo
            + pltpu.roll(x, half, axis=1) * sin_hi)


def _mla_q_kernel(a_ref, w_ref, g_ref, cos_ref, slo_ref, shi_ref, o_ref):
    x = _dot(a_ref[...], w_ref[...])
    cos, slo, shi = cos_ref[...], slo_ref[...], shi_ref[...]
    for h in range(x.shape[1] // QK_PAD):
        xh = x[:, h * QK_PAD:(h + 1) * QK_PAD]
        ms = jnp.sum(xh * xh, axis=-1, keepdims=True) * (1.0 / QK_HEAD)
        y = xh * lax.rsqrt(ms + RMS_EPS) * g_ref[...]
        o_ref[:, h * QK_PAD:h * QK_PAD + LANES] = y[:, :LANES].astype(o_ref.dtype)
        o_ref[:, h * QK_PAD + LANES:(h + 1) * QK_PAD] = _rope_lane_tile(
            y[:, LANES:], cos, slo, shi).astype(o_ref.dtype)


def _mla_q(cq, wuq_p, gain_p, tabs, seq, tm=1024, tn=1024):
    m, k = cq.shape
    n = wuq_p.shape[1]
    tm, tn = _tile(seq, tm), _tile(n, tn)
    nsb = seq // tm
    tab_spec = pl.BlockSpec((tm, LANES), lambda i, j: (i % nsb, 0))
    blocks = (_nbytes((tm, k), BF16) + _nbytes((k, tn), BF16) + _nbytes((tm, tn), BF16)
              + 3 * _nbytes((tm, LANES), F32))
    return pl.pallas_call(
        _mla_q_kernel,
        grid=(m // tm, n // tn),
        in_specs=[pl.BlockSpec((tm, k), lambda i, j: (i, 0)),
                  pl.BlockSpec((k, tn), lambda i, j: (0, j)),
                  pl.BlockSpec((1, QK_PAD), lambda i, j: (0, 0)),
                  tab_spec, tab_spec, tab_spec],
        out_specs=pl.BlockSpec((tm, tn), lambda i, j: (i, j)),
        out_shape=jax.ShapeDtypeStruct((m, n), BF16),
        compiler_params=_params(("parallel", "parallel"), blocks,
                                scratch_bytes=_nbytes((tm, tn), F32)),
        name="mla_q_proj",
    )(cq, wuq_p, gain_p, *tabs)


FLASH_BQ = 1024
FLASH_BK = 512


def _mla_kv_kernel(a_ref, w_ref, kr_ref, gn_ref, gr_ref, cos_ref, slo_ref, shi_ref, k_ref, vt_ref, *, bk):
    x = _dot(a_ref[...], w_ref[...])
    kr = kr_ref[...]
    ss_rope = jnp.sum(kr * kr, axis=-1, keepdims=True)
    kr_roped = _rope_lane_tile(kr * gr_ref[...], cos_ref[...], slo_ref[...], shi_ref[...])
    width = QK_NOPE + V_HEAD
    for h in range(x.shape[1] // width):
        kn = x[:, h * width:h * width + QK_NOPE]
        ms = (jnp.sum(kn * kn, axis=-1, keepdims=True) + ss_rope) * (1.0 / QK_HEAD)
        rs = lax.rsqrt(ms + RMS_EPS)
        k_ref[:, h * QK_PAD:h * QK_PAD + LANES] = (kn * rs * gn_ref[...]).astype(k_ref.dtype)
        k_ref[:, h * QK_PAD + LANES:(h + 1) * QK_PAD] = (kr_roped * rs).astype(k_ref.dtype)
        for c in range(x.shape[0] // bk):
            v = x[c * bk:(c + 1) * bk, h * width + QK_NOPE:(h + 1) * width]
            vt_ref[h, c] = v.T.astype(vt_ref.dtype)


def _mla_kv(ckv, wukv, k_rope, g_nope, g_rope, tabs, bsz, seq, bk, tm=1024, tn=1024):
    m, k = ckv.shape
    n = wukv.shape[1]
    width = QK_NOPE + V_HEAD
    tm, tn = _tile(seq, tm), _tile(n, tn)
    assert tm % bk == 0
    hpt = tn // width
    nsb = seq // tm
    tab_spec = pl.BlockSpec((tm, LANES), lambda i, j: (i % nsb, 0))
    blocks = (_nbytes((tm, k), BF16) + _nbytes((k, tn), BF16) + _nbytes((tm, hpt * QK_PAD), BF16)
              + _nbytes((tm, hpt * V_HEAD), BF16) + 4 * _nbytes((tm, LANES), F32))
    return pl.pallas_call(
        functools.partial(_mla_kv_kernel, bk=bk),
        grid=(m // tm, n // tn),
        in_specs=[pl.BlockSpec((tm, k), lambda i, j: (i, 0)),
                  pl.BlockSpec((k, tn), lambda i, j: (0, j)),
                  pl.BlockSpec((tm, LANES), lambda i, j: (i, 0)),
                  pl.BlockSpec((1, LANES), lambda i, j: (0, 0)),
                  pl.BlockSpec((1, LANES), lambda i, j: (0, 0)),
                  tab_spec, tab_spec, tab_spec],
        out_specs=[pl.BlockSpec((tm, hpt * QK_PAD), lambda i, j: (i, j)),
                   pl.BlockSpec((None, hpt, tm // bk, V_HEAD, bk), lambda i, j: (i // nsb, j, i % nsb, 0, 0))],
        out_shape=[jax.ShapeDtypeStruct((m, (n // width) * QK_PAD), BF16),
                   jax.ShapeDtypeStruct((bsz, n // width, seq // bk, V_HEAD, bk), BF16)],
        compiler_params=_params(("parallel", "parallel"), blocks,
                                scratch_bytes=2 * _nbytes((tm, tn), F32)),
        name="mla_kv_proj",
    )(ckv, wukv, k_rope, g_nope, g_rope, *tabs)


def _dot_nt(a, b):
    return lax.dot_general(a, b, (((1,), (1,)), ((), ())), preferred_element_type=F32)


def _flash_kernel(q_ref, k_ref, vt_ref, o_ref, m_ref, l_ref, acc_ref, s_ref, *, bq, bk):
    qi = pl.program_id(2)
    m_ref[...] = jnp.full(m_ref.shape, MASK_VALUE, F32)
    l_ref[...] = jnp.zeros(l_ref.shape, F32)
    acc_ref[...] = jnp.zeros(acc_ref.shape, F32)

    def scores(j, slot):
        kb = k_ref[pl.ds(pl.multiple_of(j * bk, bk), bk), :]
        s_ref[slot] = _dot_nt(kb, q_ref[...])

    def softmax_pv(j, slot, masked):
        st = s_ref[slot]
        if masked:
            kpos = j * bk + lax.broadcasted_iota(jnp.int32, st.shape, 0)
            qpos = qi * bq + lax.broadcasted_iota(jnp.int32, st.shape, 1)
            st = jnp.where(kpos <= qpos, st, MASK_VALUE)
        m_old = m_ref[...]
        m_new = jnp.maximum(m_old, jnp.max(st, axis=0, keepdims=True))
        p = jnp.exp2(st - m_new)
        alpha = jnp.exp2(m_old - m_new)
        l_ref[...] = alpha * l_ref[...] + jnp.sum(p, axis=0, keepdims=True)
        acc_ref[...] = alpha * acc_ref[...] + _dot(vt_ref[j], p.astype(BF16))
        m_ref[...] = m_new

    def body(i, carry):
        scores(2 * i + 1, 1)
        softmax_pv(2 * i, 0, masked=False)
        scores(2 * i + 2, 0)
        softmax_pv(2 * i + 1, 1, masked=False)
        return carry

    scores(0, 0)
    lax.fori_loop(0, qi, body, 0)
    scores(2 * qi + 1, 1)
    softmax_pv(2 * qi, 0, masked=True)
    softmax_pv(2 * qi + 1, 1, masked=True)
    o_ref[...] = (acc_ref[...] / l_ref[...]).T.astype(o_ref.dtype)


def _mla_flash(q, k, vt, bq, bk):
    b, s, _ = q.shape
    h = q.shape[2] // QK_PAD
    assert s % bq == 0 and bq == 2 * bk
    blocks = (_nbytes((bq, QK_PAD), BF16) + _nbytes((s, QK_PAD), BF16) + _nbytes((s, V_HEAD), BF16)
              + _nbytes((bq, V_HEAD), BF16))
    scratch = 2 * _nbytes((8, bq), F32) + _nbytes((V_HEAD, bq), F32) + 6 * _nbytes((bk, bq), F32)
    return pl.pallas_call(
        functools.partial(_flash_kernel, bq=bq, bk=bk),
        grid=(b, h, s // bq),
        in_specs=[pl.BlockSpec((None, bq, QK_PAD), lambda bi, hi, qi: (bi, qi, hi)),
                  pl.BlockSpec((None, s, QK_PAD), lambda bi, hi, qi: (bi, 0, hi)),
                  pl.BlockSpec((None, None, s // bk, V_HEAD, bk), lambda bi, hi, qi: (bi, hi, 0, 0, 0))],
        out_specs=pl.BlockSpec((None, bq, V_HEAD), lambda bi, hi, qi: (bi, qi, hi)),
        out_shape=jax.ShapeDtypeStruct((b, s, h * V_HEAD), BF16),
        scratch_shapes=[pltpu.VMEM((1, bq), F32), pltpu.VMEM((1, bq), F32),
                        pltpu.VMEM((V_HEAD, bq), F32), pltpu.VMEM((2, bk, bq), F32)],
        compiler_params=_params(("parallel", "parallel", "arbitrary"), blocks, scratch_bytes=scratch),
        name="mla_flash_attention",
    )(q, k, vt)


def _t5_causal_bucket(dist):
    max_exact = NUM_BUCKETS // 2
    n = jnp.maximum(dist, 0)
    nf = jnp.maximum(n, 1).astype(F32)
    large = max_exact + (jnp.log(nf / max_exact) / math.log(MAX_DISTANCE / max_exact)
                         * (NUM_BUCKETS - max_exact)).astype(jnp.int32)
    large = jnp.minimum(large, NUM_BUCKETS - 1)
    return jnp.where(n < max_exact, n, large)


def _band_bias_kernel(tab_ref, bucket_ref, o_ref, *, heads):
    g = pl.program_id(0)
    h = pl.program_id(1)
    bucket = bucket_ref[...]
    acc = jnp.full(bucket.shape, MASK_VALUE, F32)
    for b in range(NUM_BUCKETS):
        acc = jnp.where(bucket == b, tab_ref[b, g * heads + h], acc)
    o_ref[...] = acc


def _band_bias(rel_bias, buckets, heads):
    g = buckets.shape[0]
    blk = buckets.shape[1:]
    blocks = _nbytes(blk, jnp.int32) + _nbytes(blk, F32)
    return pl.pallas_call(
        functools.partial(_band_bias_kernel, heads=heads),
        grid=(g, heads),
        in_specs=[pl.BlockSpec(memory_space=pltpu.SMEM),
                  pl.BlockSpec((None,) + blk, lambda gi, hi: (gi, 0, 0))],
        out_specs=pl.BlockSpec((None, None) + blk, lambda gi, hi: (gi, hi, 0, 0)),
        out_shape=jax.ShapeDtypeStruct((g, heads) + blk, F32),
        compiler_params=_params(("parallel", "parallel"), blocks),
        name="dilated_band_bias",
    )(rel_bias, buckets)


DIL_HEAD_GROUP = 4


def _dilated_kernel(q_ref, kp_ref, kc_ref, vp_ref, vc_ref, bias_ref, o_ref, lse_ref, s_ref, *, scale, heads):
    n = pl.program_id(2)
    has_prev = n > 0
    d = DIL_HEAD_DIM
    hg = s_ref.shape[1]
    lane = lax.broadcasted_iota(jnp.int32, (Q_BLOCK, LANES), 1)
    col = lax.broadcasted_iota(jnp.int32, (Q_BLOCK, 2 * Q_BLOCK), 1)
    keep = jnp.logical_or(has_prev, col >= Q_BLOCK)

    def scores(grp, slot):
        for i in range(hg):
            sl = slice((grp * hg + i) * d, (grp * hg + i + 1) * d)
            kcat = jnp.concatenate([kp_ref[:, sl], kc_ref[:, sl]], axis=0)
            s_ref[slot, i] = _dot_nt(q_ref[:, sl], kcat)

    def softmax_pv(grp, slot, lse_tile):
        for i in range(hg):
            h = grp * hg + i
            sl = slice(h * d, (h + 1) * d)
            s = jnp.where(keep, s_ref[slot, i] * scale + bias_ref[h], MASK_VALUE)
            m = jnp.max(s, axis=-1, keepdims=True)
            p = jnp.exp(s - m)
            den = jnp.sum(p, axis=-1, keepdims=True)
            vcat = jnp.concatenate([vp_ref[:, sl], vc_ref[:, sl]], axis=0)
            o_ref[:, sl] = _dot((p / den).astype(BF16), vcat).astype(o_ref.dtype)
            lse_tile = jnp.where(lane == h, m + jnp.log(den), lse_tile)
        return lse_tile

    lse_tile = jnp.zeros((Q_BLOCK, LANES), F32)
    n_grp = heads // hg
    scores(0, 0)
    for grp in range(n_grp):
        if grp + 1 < n_grp:
            scores(grp + 1, (grp + 1) % 2)
        lse_tile = softmax_pv(grp, grp % 2, lse_tile)
    lse_ref[...] = lse_tile


def _dilated_group(q, k, v, bias_g):
    b, dilation, length, hd = q.shape
    hg = math.gcd(DIL_HEAD_GROUP, DIL_HEADS)
    nb = length // Q_BLOCK
    cur = lambda bi, r, n: (bi, r, n, 0)
    prev = lambda bi, r, n: (bi, r, jnp.maximum(n - 1, 0), 0)
    blk = (None, None, Q_BLOCK, hd)
    blocks = 6 * _nbytes((Q_BLOCK, hd), BF16) + _nbytes(bias_g.shape, F32) + _nbytes((Q_BLOCK, LANES), F32)
    return pl.pallas_call(
        functools.partial(_dilated_kernel, scale=DIL_HEAD_DIM ** -0.5, heads=DIL_HEADS),
        grid=(b, dilation, nb),
        in_specs=[pl.BlockSpec(blk, cur),
                  pl.BlockSpec(blk, prev), pl.BlockSpec(blk, cur),
                  pl.BlockSpec(blk, prev), pl.BlockSpec(blk, cur),
                  pl.BlockSpec(bias_g.shape, lambda bi, r, n: (0, 0, 0))],
        out_specs=[pl.BlockSpec(blk, cur),
                   pl.BlockSpec((None, None, Q_BLOCK, LANES), cur)],
        out_shape=[jax.ShapeDtypeStruct((b, dilation, length, hd), BF16),
                   jax.ShapeDtypeStruct((b, dilation, length, LANES), F32)],
        scratch_shapes=[pltpu.VMEM((2, hg, Q_BLOCK, 2 * Q_BLOCK), F32)],
        compiler_params=_params(("parallel", "parallel", "arbitrary"), blocks,
                                scratch_bytes=2 * hg * _nbytes((Q_BLOCK, 2 * Q_BLOCK), F32)),
        name=f"dilated_attention_d{dilation}",
    )(q, k, k, v, v, bias_g)


def _combine_kernel(*refs, heads, dilations):
    ng = len(dilations)
    n_perm = sum(dil > 1 for dil in dilations)
    o_refs, l_refs = refs[:ng], refs[ng:2 * ng]
    p_refs = refs[2 * ng:2 * ng + n_perm]
    o_ref, lse_scr, o_scr = refs[2 * ng + n_perm:]
    tm = o_ref.shape[0]
    ip = 0
    for g, dil in enumerate(dilations):
        for r in range(dil):
            rows = pl.ds(r, tm // dil, stride=dil) if dil > 1 else slice(None)
            lse_scr[g, rows, :] = l_refs[g][r]
        og = o_refs[g][...].reshape(tm, o_ref.shape[1])
        if dil > 1:
            o_scr[g] = _dot(p_refs[ip][...], og)
            ip += 1
        else:
            o_scr[g] = og.astype(F32)
    lse = [lse_scr[g] for g in range(ng)]
    m = functools.reduce(jnp.maximum, lse)
    e = [jnp.exp(l - m) for l in lse]
    tot = functools.reduce(jnp.add, e)
    w = [x / tot for x in e]
    d = DIL_HEAD_DIM
    for h in range(heads):
        sl = slice(h * d, (h + 1) * d)
        o = functools.reduce(jnp.add, [w[g][:, h:h + 1] * o_scr[g, :, sl] for g in range(ng)])
        o_ref[:, sl] = o.astype(o_ref.dtype)


def _combine_groups(outs, lses, seq, tm=256):
    dilations = tuple(o.shape[1] for o in outs)
    bsz, hd = outs[0].shape[0], outs[0].shape[3]
    ng = len(outs)
    tm = _tile(seq, tm)
    nsb = seq // tm
    m = bsz * seq
    idx = lambda i: (i // nsb, 0, i % nsb, 0)
    o_specs = [pl.BlockSpec((None, dil, tm // dil, hd), idx) for dil in dilations]
    l_specs = [pl.BlockSpec((None, dil, tm // dil, LANES), idx) for dil in dilations]
    for dil in dilations:
        assert tm % (16 * dil) == 0
    perms = [_gather_perm(tm, dil).T for dil in dilations if dil > 1]
    p_specs = [pl.BlockSpec((tm, tm), lambda i: (0, 0)) for _ in perms]
    blocks = ((ng + 1) * _nbytes((tm, hd), BF16) + ng * _nbytes((tm, LANES), F32)
              + len(perms) * _nbytes((tm, tm), BF16))
    scratch = ng * (_nbytes((tm, LANES), F32) + _nbytes((tm, hd), F32))
    return pl.pallas_call(
        functools.partial(_combine_kernel, heads=DIL_HEADS, dilations=dilations),
        grid=(m // tm,),
        in_specs=o_specs + l_specs + p_specs,
        out_specs=pl.BlockSpec((tm, hd), lambda i: (i, 0)),
        out_shape=jax.ShapeDtypeStruct((m, hd), BF16),
        scratch_shapes=[pltpu.VMEM((ng, tm, LANES), F32), pltpu.VMEM((ng, tm, hd), F32)],
        compiler_params=_params(("parallel",), blocks, scratch_bytes=2 * scratch),
        name="dilated_combine",
    )(*outs, *lses, *perms)


GATE_UP_TILE = 512


def _rope_tables(seq):
    half = QK_ROPE // 2
    inv = ROPE_THETA ** (-jnp.arange(half, dtype=F32) / half)
    ang = jnp.arange(seq).astype(F32)[:, None] * inv[None, :]
    cos, sin = jnp.cos(ang), jnp.sin(ang)
    zeros = jnp.zeros_like(cos)
    cos_t = jnp.concatenate([cos, cos, zeros, zeros], axis=1)
    sin_lo = jnp.concatenate([-sin, zeros, zeros, zeros], axis=1)
    sin_hi = jnp.concatenate([zeros, sin, zeros, zeros], axis=1)
    return cos_t, sin_lo, sin_hi


def _band_buckets():
    r = jnp.arange(Q_BLOCK)[:, None]
    c = jnp.arange(2 * Q_BLOCK)[None, :]
    steps = Q_BLOCK + r - c
    out = []
    for window, dilation in DIL_GROUPS:
        band = (steps >= 0) & (steps <= window // dilation)
        out.append(jnp.where(band, _t5_causal_bucket(steps * dilation), -1))
    return jnp.stack(out).astype(jnp.int32)


def _ffn(h, xn, ffn_wg, ffn_wu, ffn_wd, lead):
    f = ffn_wg.shape[-1]
    fp = -(-f // GATE_UP_TILE) * GATE_UP_TILE
    wg = _cast_bf16(ffn_wg, lead, cols_out=fp)
    wu = _cast_bf16(ffn_wu, lead, cols_out=fp)
    wd = _cast_bf16(ffn_wd, lead, rows_out=fp)
    act = _mm_swiglu(xn, wg, wu, tn=GATE_UP_TILE)
    return _mm_residual(act, wd, h, 0.5)


def kernel(x, ffn_norm, ffn_wg, ffn_wu, ffn_wd, attn_norm, mla_wdq, mla_q_lora_norm, mla_wuq, mla_wdkv, mla_kv_lora_norm, mla_wukv, mla_q_norm, mla_k_norm, mla_wo, kv_src_norm, w_kv_shared, k_norm_shared, rel_bias, dil_wq, dil_q_norm, dil_wo):
    bsz, seq, d = x.shape
    m = bsz * seq
    depth = ffn_norm.shape[0]
    n_a = depth // 2
    h = x.reshape(m, d)
    tabs = _rope_tables(seq)
    hd = DIL_HEADS * DIL_HEAD_DIM
    k_sh = v_sh = None
    bias_all = None

    for window, dilation in DIL_GROUPS:
        assert seq % (dilation * Q_BLOCK) == 0 and window // dilation <= Q_BLOCK
    bq, bk = min(FLASH_BQ, seq), min(FLASH_BK, seq)

    dils = tuple(dilation for _, dilation in DIL_GROUPS)
    gathered = lambda t, dil: t.reshape(bsz, dil, seq // dil, hd)

    for l in range(depth):
        if l == n_a:
            gains = jnp.stack([kv_src_norm] * N_GROUPS + [ffn_norm[l, 0]])
            *xs, xn = _rmsnorm(h, gains, dils + (1,), bsz, seq)
            wkv = _cast_bf16(w_kv_shared)
            k_sh, v_sh = [], []
            for g, dil in enumerate(dils):
                a_g = xs[g].reshape(m, d)
                gk = jnp.tile(k_norm_shared[g], DIL_HEADS)[None, :]
                k_sh.append(gathered(_mm_headnorm(a_g, wkv, gk, DIL_HEAD_DIM, col0=g * hd), dil))
                v_sh.append(gathered(_mm_plain(a_g, wkv, BF16, n=hd, col0=(N_GROUPS + g) * hd), dil))
            bias_all = _band_bias(rel_bias, _band_buckets(), DIL_HEADS)
        else:
            (xn,) = _rmsnorm(h, ffn_norm[l, 0][None, :], (1,), bsz, seq)
        h = _ffn(h, xn, ffn_wg, ffn_wu, ffn_wd, (l, 0))

        if l < n_a:
            (xn,) = _rmsnorm(h, attn_norm[l][None, :], (1,), bsz, seq)
            a = l
            q_lora = mla_wdq.shape[2]
            kv_lora = mla_wdkv.shape[2] - QK_ROPE
            wcat = jnp.concatenate(
                [mla_wdq[a], mla_wdkv[a], jnp.zeros((d, LANES - QK_ROPE), F32)], axis=1).astype(BF16)
            cq, ckv, k_rope = _mla_down(xn, wcat, mla_q_lora_norm[a][None, :],
                                        mla_kv_lora_norm[a][None, :], q_lora, kv_lora)
            wuq_p = jnp.pad(mla_wuq[a].reshape(q_lora, MLA_HEADS, QK_HEAD),
                            ((0, 0), (0, 0), (0, QK_PAD - QK_HEAD))).reshape(q_lora, -1).astype(BF16)
            q_scale = QK_HEAD ** -0.5 * math.log2(math.e)
            gq_p = jnp.pad(mla_q_norm[a] * q_scale, (0, QK_PAD - QK_HEAD))[None, :]
            q = _mla_q(cq, wuq_p, gq_p, tabs, seq)
            g_nope = mla_k_norm[a][:QK_NOPE][None, :]
            g_rope = jnp.pad(mla_k_norm[a][QK_NOPE:], (0, LANES - QK_ROPE))[None, :]
            k, vt = _mla_kv(ckv, _cast_bf16(mla_wukv, (a,)), k_rope, g_nope, g_rope, tabs, bsz, seq, bk)
            o = _mla_flash(q.reshape(bsz, seq, -1), k.reshape(bsz, seq, -1), vt, bq, bk)
            h = _mm_residual(o.reshape(m, -1), _cast_bf16(mla_wo, (a,)), h, 1.0)
        else:
            bl = l - n_a
            xq = _rmsnorm(h, jnp.stack([attn_norm[l]] * N_GROUPS), dils, bsz, seq)
            wq = _cast_bf16(dil_wq, (bl,))
            outs, lses = [], []
            for g, dil in enumerate(dils):
                gq = jnp.tile(dil_q_norm[bl][g], DIL_HEADS)[None, :]
                q_g = gathered(_mm_headnorm(xq[g].reshape(m, d), wq, gq, DIL_HEAD_DIM, col0=g * hd), dil)
                o_g, l_g = _dilated_group(q_g, k_sh[g], v_sh[g], bias_all[g])
                outs.append(o_g)
                lses.append(l_g)
            o = _combine_groups(outs, lses, seq)
            h = _mm_residual(o, _cast_bf16(dil_wo, (bl,)), h, 1.0)

        (xn,) = _rmsnorm(h, ffn_norm[l, 1][None, :], (1,), bsz, seq)
        h = _ffn(h, xn, ffn_wg, ffn_wu, ffn_wd, (l, 1))

    return h.reshape(bsz, seq, d)
```

```python
import functools
import math

import jax
import jax.numpy as jnp
import numpy as np
from jax import lax
from jax.experimental import pallas as pl
from jax.experimental.pallas import tpu as pltpu

F32 = jnp.float32
BF16 = jnp.bfloat16

RMS_EPS = 1e-6
MLA_HEADS = 32
QK_NOPE = 128
QK_ROPE = 64
QK_HEAD = QK_NOPE + QK_ROPE
V_HEAD = 128
ROPE_THETA = 10000.0
DIL_GROUPS = ((128, 1), (512, 4), (2048, 16))
N_GROUPS = 3
DIL_HEADS = 32
DIL_HEAD_DIM = 128
NUM_BUCKETS = 32
MAX_DISTANCE = 2048
Q_BLOCK = 128

LANES = 128
QK_PAD = 2 * LANES
V7X_VMEM_BYTES = 64 * 1024 * 1024
VMEM_HEADROOM_BYTES = 8 * 1024 * 1024
MASK_VALUE = -1e30


def _nbytes(shape, dtype):
    return int(np.prod(shape)) * jnp.dtype(dtype).itemsize


def _params(semantics, block_bytes, scratch_bytes=0):
    need = 2 * block_bytes + scratch_bytes + VMEM_HEADROOM_BYTES
    limit = min(max(need, 32 * 1024 * 1024), V7X_VMEM_BYTES - 4 * 1024 * 1024)
    return pltpu.CompilerParams(dimension_semantics=semantics, vmem_limit_bytes=int(limit))


def _tile(dim, pref):
    if dim <= pref:
        return dim
    t = pref
    while dim % t:
        t //= 2
    return t


def _gather_perm(tm, dil):
    n = tm // dil
    i = np.arange(tm)
    p = np.zeros((tm, tm), np.float32)
    p[i, (i % n) * dil + i // n] = 1.0
    return jnp.asarray(p, BF16)


def _rmsnorm_kernel(x_ref, g_ref, *refs, dilations):
    n_perm = sum(dil > 1 for dil in dilations)
    p_refs, o_refs = refs[:n_perm], refs[n_perm:]
    x = x_ref[...]
    y = x * lax.rsqrt(jnp.mean(x * x, axis=-1, keepdims=True) + RMS_EPS)
    tm = x.shape[0]
    ip = 0
    for i, (o_ref, dil) in enumerate(zip(o_refs, dilations)):
        yg = (y * g_ref[i:i + 1, :]).astype(o_ref.dtype)
        if dil == 1:
            o_ref[...] = yg
        else:
            yp = _dot(p_refs[ip][...], yg).astype(o_ref.dtype)
            ip += 1
            n = tm // dil
            for r in range(dil):
                o_ref[r] = yp[r * n:(r + 1) * n, :]


def _rmsnorm(x, gains, dilations, bsz, seq, tm=256):
    m, d = x.shape
    n = gains.shape[0]
    tm = _tile(seq, tm)
    nsb = seq // tm
    out_specs, out_shape, perms = [], [], []
    for dil in dilations:
        if dil == 1:
            out_specs.append(pl.BlockSpec((tm, d), lambda i: (i, 0)))
            out_shape.append(jax.ShapeDtypeStruct((m, d), BF16))
        else:
            assert tm % (16 * dil) == 0
            perms.append(_gather_perm(tm, dil))
            out_specs.append(pl.BlockSpec((None, dil, tm // dil, d), lambda i: (i // nsb, 0, i % nsb, 0)))
            out_shape.append(jax.ShapeDtypeStruct((bsz, dil, seq // dil, d), BF16))
    blocks = (_nbytes((tm, d), F32) + n * _nbytes((tm, d), BF16) + _nbytes((n, d), F32)
              + len(perms) * _nbytes((tm, tm), BF16))
    return pl.pallas_call(
        functools.partial(_rmsnorm_kernel, dilations=tuple(dilations)),
        grid=(m // tm,),
        in_specs=[pl.BlockSpec((tm, d), lambda i: (i, 0)),
                  pl.BlockSpec((n, d), lambda i: (0, 0))]
                 + [pl.BlockSpec((tm, tm), lambda i: (0, 0)) for _ in perms],
        out_specs=out_specs,
        out_shape=out_shape,
        compiler_params=_params(("parallel",), blocks, scratch_bytes=3 * _nbytes((tm, d), F32)),
        name="rmsnorm",
    )(x, gains, *perms)


CAST_BLOCK_BYTES = 12 * 1024 * 1024


def _cast_kernel(x_ref, o_ref, *, rows_in, cols_in, tr):
    x = x_ref[...].astype(o_ref.dtype)
    if o_ref.shape[1] > cols_in:
        o_ref[:, cols_in:] = jnp.zeros((tr, o_ref.shape[1] - cols_in), o_ref.dtype)
    row = pl.program_id(0) * tr + lax.broadcasted_iota(jnp.int32, x.shape, 0)
    o_ref[:, :cols_in] = jnp.where(row < rows_in, x, jnp.zeros_like(x))


def _cast_bf16(w, lead=(), rows_out=None, cols_out=None):
    r, c = w.shape[-2:]
    rows_out = rows_out or r
    cols_out = cols_out or c
    tr = 8
    while tr * 2 * c * 4 <= CAST_BLOCK_BYTES and r % (tr * 2) == 0 and rows_out % (tr * 2) == 0:
        tr *= 2
    assert r % tr == 0 and rows_out % tr == 0 and c % LANES == 0
    n_in = r // tr
    squeeze = (None,) * len(lead)
    blocks = _nbytes((tr, c), F32) + _nbytes((tr, cols_out), BF16)
    return pl.pallas_call(
        functools.partial(_cast_kernel, rows_in=r, cols_in=c, tr=tr),
        grid=(rows_out // tr,),
        in_specs=[pl.BlockSpec(squeeze + (tr, c), lambda i: tuple(lead) + (jnp.minimum(i, n_in - 1), 0))],
        out_specs=pl.BlockSpec((tr, cols_out), lambda i: (i, 0)),
        out_shape=jax.ShapeDtypeStruct((rows_out, cols_out), BF16),
        compiler_params=_params(("parallel",), blocks),
        name="cast_bf16",
    )(w)


def _dot(a, b):
    return jnp.dot(a, b, preferred_element_type=F32)


def _swiglu_kernel(a_ref, wg_ref, wu_ref, o_ref):
    a = a_ref[...]
    g = _dot(a, wg_ref[...])
    u = _dot(a, wu_ref[...])
    o_ref[...] = (g * jax.nn.sigmoid(g) * u).astype(o_ref.dtype)


def _mm_swiglu(a, wg, wu, tm=1024, tn=512):
    m, k = a.shape
    n = wg.shape[1]
    tm, tn = _tile(m, tm), _tile(n, tn)
    blocks = _nbytes((tm, k), BF16) + 2 * _nbytes((k, tn), BF16) + _nbytes((tm, tn), BF16)
    return pl.pallas_call(
        _swiglu_kernel,
        grid=(m // tm, n // tn),
        in_specs=[pl.BlockSpec((tm, k), lambda i, j: (i, 0)),
                  pl.BlockSpec((k, tn), lambda i, j: (0, j)),
                  pl.BlockSpec((k, tn), lambda i, j: (0, j))],
        out_specs=pl.BlockSpec((tm, tn), lambda i, j: (i, j)),
        out_shape=jax.ShapeDtypeStruct((m, n), BF16),
        compiler_params=_params(("parallel", "parallel"), blocks,
                                scratch_bytes=2 * _nbytes((tm, tn), F32)),
        name="ffn_gate_up",
    )(a, wg, wu)


def _residual_kernel(a_ref, b_ref, r_ref, o_ref, acc_ref, *, scale, nk):
    k = pl.program_id(2)
    part = _dot(a_ref[...], b_ref[...])
    if nk == 1:
        o_ref[...] = r_ref[...] + scale * part
        return

    @pl.when(k == 0)
    def _():
        acc_ref[...] = part

    @pl.when(jnp.logical_and(k > 0, k < nk - 1))
    def _():
        acc_ref[...] += part

    @pl.when(k == nk - 1)
    def _():
        o_ref[...] = r_ref[...] + scale * (acc_ref[...] + part)


def _mm_residual(a, b, res, scale, tm=1024, tn=1024, tk_max=4096):
    m, kdim = a.shape
    n = b.shape[1]
    tm, tn = _tile(m, tm), _tile(n, tn)
    nk = 1
    while kdim // nk > tk_max or kdim % nk or (kdim // nk) % LANES:
        nk += 1
    tk = kdim // nk
    blocks = (_nbytes((tm, tk), BF16) + _nbytes((tk, tn), BF16) + 2 * _nbytes((tm, tn), F32))
    return pl.pallas_call(
        functools.partial(_residual_kernel, scale=scale, nk=nk),
        grid=(m // tm, n // tn, nk),
        in_specs=[pl.BlockSpec((tm, tk), lambda i, j, k: (i, k)),
                  pl.BlockSpec((tk, tn), lambda i, j, k: (k, j)),
                  pl.BlockSpec((tm, tn), lambda i, j, k: (i, j))],
        out_specs=pl.BlockSpec((tm, tn), lambda i, j, k: (i, j)),
        out_shape=jax.ShapeDtypeStruct((m, n), F32),
        scratch_shapes=[pltpu.VMEM((tm, tn), F32)],
        compiler_params=_params(("parallel", "parallel", "arbitrary"), blocks,
                                scratch_bytes=2 * _nbytes((tm, tn), F32)),
        name="matmul_residual",
    )(a, b, res)


def _plain_kernel(a_ref, b_ref, o_ref):
    o_ref[...] = _dot(a_ref[...], b_ref[...]).astype(o_ref.dtype)


def _mm_plain(a, b, out_dtype, n=None, col0=0, tm=1024, tn=1024):
    m, k = a.shape
    n = n or b.shape[1]
    tm, tn = _tile(m, tm), _tile(n, tn)
    assert col0 % tn == 0
    joff = col0 // tn
    blocks = _nbytes((tm, k), BF16) + _nbytes((k, tn), BF16) + _nbytes((tm, tn), out_dtype)
    return pl.pallas_call(
        _plain_kernel,
        grid=(m // tm, n // tn),
        in_specs=[pl.BlockSpec((tm, k), lambda i, j: (i, 0)),
                  pl.BlockSpec((k, tn), lambda i, j: (0, j + joff))],
        out_specs=pl.BlockSpec((tm, tn), lambda i, j: (i, j)),
        out_shape=jax.ShapeDtypeStruct((m, n), out_dtype),
        compiler_params=_params(("parallel", "parallel"), blocks,
                                scratch_bytes=_nbytes((tm, tn), F32)),
        name="matmul_plain",
    )(a, b)


def _headnorm_kernel(a_ref, b_ref, g_ref, o_ref, *, head_dim):
    x = _dot(a_ref[...], b_ref[...])
    tn = x.shape[1]
    for h in range(tn // head_dim):
        sl = slice(h * head_dim, (h + 1) * head_dim)
        xh = x[:, sl]
        y = xh * lax.rsqrt(jnp.mean(xh * xh, axis=-1, keepdims=True) + RMS_EPS)
        o_ref[:, sl] = (y * g_ref[:, sl]).astype(o_ref.dtype)


def _mm_headnorm(a, b, gain_row, head_dim, col0=0, tm=1024, tn=1024):
    m, k = a.shape
    n = gain_row.shape[1]
    tm, tn = _tile(m, tm), _tile(n, tn)
    assert col0 % tn == 0
    joff = col0 // tn
    blocks = (_nbytes((tm, k), BF16) + _nbytes((k, tn), BF16) + _nbytes((tm, tn), BF16)
              + _nbytes((8, tn), F32))
    return pl.pallas_call(
        functools.partial(_headnorm_kernel, head_dim=head_dim),
        grid=(m // tm, n // tn),
        in_specs=[pl.BlockSpec((tm, k), lambda i, j: (i, 0)),
                  pl.BlockSpec((k, tn), lambda i, j: (0, j + joff)),
                  pl.BlockSpec((1, tn), lambda i, j: (0, j))],
        out_specs=pl.BlockSpec((tm, tn), lambda i, j: (i, j)),
        out_shape=jax.ShapeDtypeStruct((m, n), BF16),
        compiler_params=_params(("parallel", "parallel"), blocks,
                                scratch_bytes=_nbytes((tm, tn), F32)),
        name="matmul_headnorm",
    )(a, b, gain_row)


def _mla_down_kernel(a_ref, w_ref, gq_ref, gkv_ref, cq_ref, ckv_ref, kr_ref, *, q_lora, kv_lora):
    x = _dot(a_ref[...], w_ref[...])
    xq = x[:, :q_lora]
    cq = xq * lax.rsqrt(jnp.mean(xq * xq, axis=-1, keepdims=True) + RMS_EPS) * gq_ref[...]
    cq_ref[...] = cq.astype(cq_ref.dtype)
    xkv = x[:, q_lora:q_lora + kv_lora]
    ckv = xkv * lax.rsqrt(jnp.mean(xkv * xkv, axis=-1, keepdims=True) + RMS_EPS) * gkv_ref[...]
    ckv_ref[...] = ckv.astype(ckv_ref.dtype)
    kr_ref[...] = x[:, q_lora + kv_lora:]


def _mla_down(a, wcat, gq, gkv, q_lora, kv_lora, tm=512):
    m, k = a.shape
    n = wcat.shape[1]
    tm = _tile(m, tm)
    blocks = (_nbytes((tm, k), BF16) + _nbytes((k, n), BF16) + _nbytes((tm, q_lora), BF16)
              + _nbytes((tm, kv_lora), BF16) + _nbytes((tm, LANES), F32))
    return pl.pallas_call(
        functools.partial(_mla_down_kernel, q_lora=q_lora, kv_lora=kv_lora),
        grid=(m // tm,),
        in_specs=[pl.BlockSpec((tm, k), lambda i: (i, 0)),
                  pl.BlockSpec((k, n), lambda i: (0, 0)),
                  pl.BlockSpec((1, q_lora), lambda i: (0, 0)),
                  pl.BlockSpec((1, kv_lora), lambda i: (0, 0))],
        out_specs=[pl.BlockSpec((tm, q_lora), lambda i: (i, 0)),
                   pl.BlockSpec((tm, kv_lora), lambda i: (i, 0)),
                   pl.BlockSpec((tm, LANES), lambda i: (i, 0))],
        out_shape=[jax.ShapeDtypeStruct((m, q_lora), BF16),
                   jax.ShapeDtypeStruct((m, kv_lora), BF16),
                   jax.ShapeDtypeStruct((m, LANES), F32)],
        compiler_params=_params(("parallel",), blocks, scratch_bytes=_nbytes((tm, n), F32)),
        name="mla_down",
    )(a, wcat, gq, gkv)


def _rope_lane_tile(x, cos, sin_lo, sin_hi):
    half = QK_ROPE // 2
    return (x * cos + pltpu.roll(x, LANES - half, axis=1) * sin_lo
            + pltpu.roll(x, half, axis=1) * sin_hi)


def _mla_q_kernel(a_ref, w_ref, g_ref, cos_ref, slo_ref, shi_ref, o_ref):
    x = _dot(a_ref[...], w_ref[...])
    cos, slo, shi = cos_ref[...], slo_ref[...], shi_ref[...]
    for h in range(x.shape[1] // QK_PAD):
        xh = x[:, h * QK_PAD:(h + 1) * QK_PAD]
        ms = jnp.sum(xh * xh, axis=-1, keepdims=True) * (1.0 / QK_HEAD)
        y = xh * lax.rsqrt(ms + RMS_EPS) * g_ref[...]
        o_ref[:, h * QK_PAD:h * QK_PAD + LANES] = y[:, :LANES].astype(o_ref.dtype)
        o_ref[:, h * QK_PAD + LANES:(h + 1) * QK_PAD] = _rope_lane_tile(
            y[:, LANES:], cos, slo, shi).astype(o_ref.dtype)


def _mla_q(cq, wuq_p, gain_p, tabs, seq, tm=1024, tn=1024):
    m, k = cq.shape
    n = wuq_p.shape[1]
    tm, tn = _tile(seq, tm), _tile(n, tn)
    nsb = seq // tm
    tab_spec = pl.BlockSpec((tm, LANES), lambda i, j: (i % nsb, 0))
    blocks = (_nbytes((tm, k), BF16) + _nbytes((k, tn), BF16) + _nbytes((tm, tn), BF16)
              + 3 * _nbytes((tm, LANES), F32))
    return pl.pallas_call(
        _mla_q_kernel,
        grid=(m // tm, n // tn),
        in_specs=[pl.BlockSpec((tm, k), lambda i, j: (i, 0)),
                  pl.BlockSpec((k, tn), lambda i, j: (0, j)),
                  pl.BlockSpec((1, QK_PAD), lambda i, j: (0, 0)),
                  tab_spec, tab_spec, tab_spec],
        out_specs=pl.BlockSpec((tm, tn), lambda i, j: (i, j)),
        out_shape=jax.ShapeDtypeStruct((m, n), BF16),
        compiler_params=_params(("parallel", "parallel"), blocks,
                                scratch_bytes=_nbytes((tm, tn), F32)),
        name="mla_q_proj",
    )(cq, wuq_p, gain_p, *tabs)


FLASH_BQ = 2048
FLASH_BK = 1024


def _mla_kv_kernel(a_ref, w_ref, kr_ref, gn_ref, gr_ref, cos_ref, slo_ref, shi_ref, k_ref, vt_ref, *, bk):
    x = _dot(a_ref[...], w_ref[...])
    kr = kr_ref[...]
    ss_rope = jnp.sum(kr * kr, axis=-1, keepdims=True)
    kr_roped = _rope_lane_tile(kr * gr_ref[...], cos_ref[...], slo_ref[...], shi_ref[...])
    width = QK_NOPE + V_HEAD
    for h in range(x.shape[1] // width):
        kn = x[:, h * width:h * width + QK_NOPE]
        ms = (jnp.sum(kn * kn, axis=-1, keepdims=True) + ss_rope) * (1.0 / QK_HEAD)
        rs = lax.rsqrt(ms + RMS_EPS)
        k_ref[:, h * QK_PAD:h * QK_PAD + LANES] = (kn * rs * gn_ref[...]).astype(k_ref.dtype)
        k_ref[:, h * QK_PAD + LANES:(h + 1) * QK_PAD] = (kr_roped * rs).astype(k_ref.dtype)
        for c in range(x.shape[0] // bk):
            v = x[c * bk:(c + 1) * bk, h * width + QK_NOPE:(h + 1) * width]
            vt_ref[h, c] = v.T.astype(vt_ref.dtype)


def _mla_kv(ckv, wukv, k_rope, g_nope, g_rope, tabs, bsz, seq, bk, tm=1024, tn=1024):
    m, k = ckv.shape
    n = wukv.shape[1]
    width = QK_NOPE + V_HEAD
    tm, tn = _tile(seq, tm), _tile(n, tn)
    assert tm % bk == 0
    hpt = tn // width
    nsb = seq // tm
    tab_spec = pl.BlockSpec((tm, LANES), lambda i, j: (i % nsb, 0))
    blocks = (_nbytes((tm, k), BF16) + _nbytes((k, tn), BF16) + _nbytes((tm, hpt * QK_PAD), BF16)
              + _nbytes((tm, hpt * V_HEAD), BF16) + 4 * _nbytes((tm, LANES), F32))
    return pl.pallas_call(
        functools.partial(_mla_kv_kernel, bk=bk),
        grid=(m // tm, n // tn),
        in_specs=[pl.BlockSpec((tm, k), lambda i, j: (i, 0)),
                  pl.BlockSpec((k, tn), lambda i, j: (0, j)),
                  pl.BlockSpec((tm, LANES), lambda i, j: (i, 0)),
                  pl.BlockSpec((1, LANES), lambda i, j: (0, 0)),
                  pl.BlockSpec((1, LANES), lambda i, j: (0, 0)),
                  tab_spec, tab_spec, tab_spec],
        out_specs=[pl.BlockSpec((tm, hpt * QK_PAD), lambda i, j: (i, j)),
                   pl.BlockSpec((None, hpt, tm // bk, V_HEAD, bk), lambda i, j: (i // nsb, j, i % nsb, 0, 0))],
        out_shape=[jax.ShapeDtypeStruct((m, (n // width) * QK_PAD), BF16),
                   jax.ShapeDtypeStruct((bsz, n // width, seq // bk, V_HEAD, bk), BF16)],
        compiler_params=_params(("parallel", "parallel"), blocks,
                                scratch_bytes=2 * _nbytes((tm, tn), F32)),
        name="mla_kv_proj",
    )(ckv, wukv, k_rope, g_nope, g_rope, *tabs)


def _dot_nt(a, b):
    return lax.dot_general(a, b, (((1,), (1,)), ((), ())), preferred_element_type=F32)


def _flash_kernel(q_ref, k_ref, vt_ref, o_ref, m_ref, l_ref, acc_ref, s_ref, *, bq, bk):
    qi = pl.program_id(2)
    half = bk // 2
    m_ref[...] = jnp.full(m_ref.shape, MASK_VALUE, F32)
    l_ref[...] = jnp.zeros(l_ref.shape, F32)
    acc_ref[...] = jnp.zeros(acc_ref.shape, F32)

    def scores(j, slot, c0=0):
        kb = k_ref[pl.ds(pl.multiple_of(j * bk, bk), bk), :]
        s_ref[slot, :, c0:] = _dot_nt(kb, q_ref[c0:, :])

    def softmax_pv(j, slot, r0=0, nr=bk, c0=0, diagonal=False):
        st = s_ref[slot, r0:r0 + nr, c0:]
        if diagonal:
            row = lax.broadcasted_iota(jnp.int32, (nr, nr), 0)
            col = lax.broadcasted_iota(jnp.int32, (nr, nr), 1)
            tri = jnp.where(row <= col, st[:, :nr], MASK_VALUE)
            st = tri if st.shape[1] == nr else jnp.concatenate([tri, st[:, nr:]], axis=1)
        m_old = m_ref[:, c0:]
        m_new = jnp.maximum(m_old, jnp.max(st, axis=0, keepdims=True))
        p = jnp.exp2(st - m_new)
        alpha = jnp.exp2(m_old - m_new)
        l_ref[:, c0:] = alpha * l_ref[:, c0:] + jnp.sum(p, axis=0, keepdims=True)
        acc_ref[:, c0:] = alpha * acc_ref[:, c0:] + _dot(vt_ref[j, :, r0:r0 + nr], p.astype(BF16))
        m_ref[:, c0:] = m_new

    def body(i, carry):
        scores(2 * i + 1, 1)
        softmax_pv(2 * i, 0)
        scores(2 * i + 2, 0)
        softmax_pv(2 * i + 1, 1)
        return carry

    scores(0, 0)
    lax.fori_loop(0, qi, body, 0)
    scores(2 * qi + 1, 1, c0=bk)
    for g in range(4):
        softmax_pv(2 * qi + g // 2, g // 2, r0=(g % 2) * half, nr=half, c0=g * half, diagonal=True)
    o_ref[...] = (acc_ref[...] / l_ref[...]).T.astype(o_ref.dtype)


def _mla_flash(q, k, vt, bq, bk):
    b, s, _ = q.shape
    h = q.shape[2] // QK_PAD
    assert s % bq == 0 and bq == 2 * bk
    blocks = (_nbytes((bq, QK_PAD), BF16) + _nbytes((s, QK_PAD), BF16) + _nbytes((s, V_HEAD), BF16)
              + _nbytes((bq, V_HEAD), BF16))
    scratch = 2 * _nbytes((8, bq), F32) + _nbytes((V_HEAD, bq), F32) + 6 * _nbytes((bk, bq), F32)
    return pl.pallas_call(
        functools.partial(_flash_kernel, bq=bq, bk=bk),
        grid=(b, h, s // bq),
        in_specs=[pl.BlockSpec((None, bq, QK_PAD), lambda bi, hi, qi: (bi, qi, hi)),
                  pl.BlockSpec((None, s, QK_PAD), lambda bi, hi, qi: (bi, 0, hi)),
                  pl.BlockSpec((None, None, s // bk, V_HEAD, bk), lambda bi, hi, qi: (bi, hi, 0, 0, 0))],
        out_specs=pl.BlockSpec((None, bq, V_HEAD), lambda bi, hi, qi: (bi, qi, hi)),
        out_shape=jax.ShapeDtypeStruct((b, s, h * V_HEAD), BF16),
        scratch_shapes=[pltpu.VMEM((1, bq), F32), pltpu.VMEM((1, bq), F32),
                        pltpu.VMEM((V_HEAD, bq), F32), pltpu.VMEM((2, bk, bq), F32)],
        compiler_params=_params(("parallel", "parallel", "arbitrary"), blocks, scratch_bytes=scratch),
        name="mla_flash_attention",
    )(q, k, vt)


def _t5_causal_bucket(dist):
    max_exact = NUM_BUCKETS // 2
    n = jnp.maximum(dist, 0)
    nf = jnp.maximum(n, 1).astype(F32)
    large = max_exact + (jnp.log(nf / max_exact) / math.log(MAX_DISTANCE / max_exact)
                         * (NUM_BUCKETS - max_exact)).astype(jnp.int32)
    large = jnp.minimum(large, NUM_BUCKETS - 1)
    return jnp.where(n < max_exact, n, large)


def _band_bias_kernel(tab_ref, bucket_ref, o_ref, *, heads):
    g = pl.program_id(0)
    h = pl.program_id(1)
    bucket = bucket_ref[...]
    acc = jnp.full(bucket.shape, MASK_VALUE, F32)
    for b in range(NUM_BUCKETS):
        acc = jnp.where(bucket == b, tab_ref[b, g * heads + h], acc)
    o_ref[...] = acc


def _band_bias(rel_bias, buckets, heads):
    g = buckets.shape[0]
    blk = buckets.shape[1:]
    blocks = _nbytes(blk, jnp.int32) + _nbytes(blk, F32)
    return pl.pallas_call(
        functools.partial(_band_bias_kernel, heads=heads),
        grid=(g, heads),
        in_specs=[pl.BlockSpec(memory_space=pltpu.SMEM),
                  pl.BlockSpec((None,) + blk, lambda gi, hi: (gi, 0, 0))],
        out_specs=pl.BlockSpec((None, None) + blk, lambda gi, hi: (gi, hi, 0, 0)),
        out_shape=jax.ShapeDtypeStruct((g, heads) + blk, F32),
        compiler_params=_params(("parallel", "parallel"), blocks),
        name="dilated_band_bias",
    )(rel_bias, buckets)


DIL_HEAD_GROUP = 4


def _dilated_kernel(q_ref, kp_ref, kc_ref, vp_ref, vc_ref, bias_ref, o_ref, lse_ref, s_ref, *, scale, heads):
    n = pl.program_id(2)
    has_prev = n > 0
    d = DIL_HEAD_DIM
    hg = s_ref.shape[1]
    lane = lax.broadcasted_iota(jnp.int32, (Q_BLOCK, LANES), 1)
    col = lax.broadcasted_iota(jnp.int32, (Q_BLOCK, 2 * Q_BLOCK), 1)
    keep = jnp.logical_or(has_prev, col >= Q_BLOCK)

    def scores(grp, slot):
        for i in range(hg):
            sl = slice((grp * hg + i) * d, (grp * hg + i + 1) * d)
            kcat = jnp.concatenate([kp_ref[:, sl], kc_ref[:, sl]], axis=0)
            s_ref[slot, i] = _dot_nt(q_ref[:, sl], kcat)

    def softmax_pv(grp, slot, lse_tile):
        for i in range(hg):
            h = grp * hg + i
            sl = slice(h * d, (h + 1) * d)
            s = jnp.where(keep, s_ref[slot, i] * scale + bias_ref[h], MASK_VALUE)
            m = jnp.max(s, axis=-1, keepdims=True)
            p = jnp.exp(s - m)
            den = jnp.sum(p, axis=-1, keepdims=True)
            vcat = jnp.concatenate([vp_ref[:, sl], vc_ref[:, sl]], axis=0)
            o_ref[:, sl] = _dot((p / den).astype(BF16), vcat).astype(o_ref.dtype)
            lse_tile = jnp.where(lane == h, m + jnp.log(den), lse_tile)
        return lse_tile

    lse_tile = jnp.zeros((Q_BLOCK, LANES), F32)
    n_grp = heads // hg
    scores(0, 0)
    for grp in range(n_grp):
        if grp + 1 < n_grp:
            scores(grp + 1, (grp + 1) % 2)
        lse_tile = softmax_pv(grp, grp % 2, lse_tile)
    lse_ref[...] = lse_tile


def _dilated_group(q, k, v, bias_g):
    b, dilation, length, hd = q.shape
    hg = math.gcd(DIL_HEAD_GROUP, DIL_HEADS)
    nb = length // Q_BLOCK
    cur = lambda bi, r, n: (bi, r, n, 0)
    prev = lambda bi, r, n: (bi, r, jnp.maximum(n - 1, 0), 0)
    blk = (None, None, Q_BLOCK, hd)
    blocks = 6 * _nbytes((Q_BLOCK, hd), BF16) + _nbytes(bias_g.shape, F32) + _nbytes((Q_BLOCK, LANES), F32)
    return pl.pallas_call(
        functools.partial(_dilated_kernel, scale=DIL_HEAD_DIM ** -0.5, heads=DIL_HEADS),
        grid=(b, dilation, nb),
        in_specs=[pl.BlockSpec(blk, cur),
                  pl.BlockSpec(blk, prev), pl.BlockSpec(blk, cur),
                  pl.BlockSpec(blk, prev), pl.BlockSpec(blk, cur),
                  pl.BlockSpec(bias_g.shape, lambda bi, r, n: (0, 0, 0))],
        out_specs=[pl.BlockSpec(blk, cur),
                   pl.BlockSpec((None, None, Q_BLOCK, LANES), cur)],
        out_shape=[jax.ShapeDtypeStruct((b, dilation, length, hd), BF16),
                   jax.ShapeDtypeStruct((b, dilation, length, LANES), F32)],
        scratch_shapes=[pltpu.VMEM((2, hg, Q_BLOCK, 2 * Q_BLOCK), F32)],
        compiler_params=_params(("parallel", "parallel", "arbitrary"), blocks,
                                scratch_bytes=2 * hg * _nbytes((Q_BLOCK, 2 * Q_BLOCK), F32)),
        name=f"dilated_attention_d{dilation}",
    )(q, k, k, v, v, bias_g)


def _combine_kernel(*refs, heads, dilations):
    ng = len(dilations)
    n_perm = sum(dil > 1 for dil in dilations)
    o_refs, l_refs = refs[:ng], refs[ng:2 * ng]
    p_refs = refs[2 * ng:2 * ng + n_perm]
    o_ref, lse_scr, o_scr = refs[2 * ng + n_perm:]
    tm = o_ref.shape[0]
    ip = 0
    for g, dil in enumerate(dilations):
        for r in range(dil):
            rows = pl.ds(r, tm // dil, stride=dil) if dil > 1 else slice(None)
            lse_scr[g, rows, :] = l_refs[g][r]
        og = o_refs[g][...].reshape(tm, o_ref.shape[1])
        if dil > 1:
            o_scr[g] = _dot(p_refs[ip][...], og)
            ip += 1
        else:
            o_scr[g] = og.astype(F32)
    lse = [lse_scr[g] for g in range(ng)]
    m = functools.reduce(jnp.maximum, lse)
    e = [jnp.exp(l - m) for l in lse]
    tot = functools.reduce(jnp.add, e)
    w = [x / tot for x in e]
    d = DIL_HEAD_DIM
    for h in range(heads):
        sl = slice(h * d, (h + 1) * d)
        o = functools.reduce(jnp.add, [w[g][:, h:h + 1] * o_scr[g, :, sl] for g in range(ng)])
        o_ref[:, sl] = o.astype(o_ref.dtype)


def _combine_groups(outs, lses, seq, tm=256):
    dilations = tuple(o.shape[1] for o in outs)
    bsz, hd = outs[0].shape[0], outs[0].shape[3]
    ng = len(outs)
    tm = _tile(seq, tm)
    nsb = seq // tm
    m = bsz * seq
    idx = lambda i: (i // nsb, 0, i % nsb, 0)
    o_specs = [pl.BlockSpec((None, dil, tm // dil, hd), idx) for dil in dilations]
    l_specs = [pl.BlockSpec((None, dil, tm // dil, LANES), idx) for dil in dilations]
    for dil in dilations:
        assert tm % (16 * dil) == 0
    perms = [_gather_perm(tm, dil).T for dil in dilations if dil > 1]
    p_specs = [pl.BlockSpec((tm, tm), lambda i: (0, 0)) for _ in perms]
    blocks = ((ng + 1) * _nbytes((tm, hd), BF16) + ng * _nbytes((tm, LANES), F32)
              + len(perms) * _nbytes((tm, tm), BF16))
    scratch = ng * (_nbytes((tm, LANES), F32) + _nbytes((tm, hd), F32))
    return pl.pallas_call(
        functools.partial(_combine_kernel, heads=DIL_HEADS, dilations=dilations),
        grid=(m // tm,),
        in_specs=o_specs + l_specs + p_specs,
        out_specs=pl.BlockSpec((tm, hd), lambda i: (i, 0)),
        out_shape=jax.ShapeDtypeStruct((m, hd), BF16),
        scratch_shapes=[pltpu.VMEM((ng, tm, LANES), F32), pltpu.VMEM((ng, tm, hd), F32)],
        compiler_params=_params(("parallel",), blocks, scratch_bytes=2 * scratch),
        name="dilated_combine",
    )(*outs, *lses, *perms)


GATE_UP_TILE = 512


def _rope_tables(seq):
    half = QK_ROPE // 2
    inv = ROPE_THETA ** (-jnp.arange(half, dtype=F32) / half)
    ang = jnp.arange(seq).astype(F32)[:, None] * inv[None, :]
    cos, sin = jnp.cos(ang), jnp.sin(ang)
    zeros = jnp.zeros_like(cos)
    cos_t = jnp.concatenate([cos, cos, zeros, zeros], axis=1)
    sin_lo = jnp.concatenate([-sin, zeros, zeros, zeros], axis=1)
    sin_hi = jnp.concatenate([zeros, sin, zeros, zeros], axis=1)
    return cos_t, sin_lo, sin_hi


def _band_buckets():
    r = jnp.arange(Q_BLOCK)[:, None]
    c = jnp.arange(2 * Q_BLOCK)[None, :]
    steps = Q_BLOCK + r - c
    out = []
    for window, dilation in DIL_GROUPS:
        band = (steps >= 0) & (steps <= window // dilation)
        out.append(jnp.where(band, _t5_causal_bucket(steps * dilation), -1))
    return jnp.stack(out).astype(jnp.int32)


def _ffn(h, xn, ffn_wg, ffn_wu, ffn_wd, lead):
    f = ffn_wg.shape[-1]
    fp = -(-f // GATE_UP_TILE) * GATE_UP_TILE
    wg = _cast_bf16(ffn_wg, lead, cols_out=fp)
    wu = _cast_bf16(ffn_wu, lead, cols_out=fp)
    wd = _cast_bf16(ffn_wd, lead, rows_out=fp)
    act = _mm_swiglu(xn, wg, wu, tn=GATE_UP_TILE)
    return _mm_residual(act, wd, h, 0.5)


def kernel(x, ffn_norm, ffn_wg, ffn_wu, ffn_wd, attn_norm, mla_wdq, mla_q_lora_norm, mla_wuq, mla_wdkv, mla_kv_lora_norm, mla_wukv, mla_q_norm, mla_k_norm, mla_wo, kv_src_norm, w_kv_shared, k_norm_shared, rel_bias, dil_wq, dil_q_norm, dil_wo):
    bsz, seq, d = x.shape
    m = bsz * seq
    depth = ffn_norm.shape[0]
    n_a = depth // 2
    h = x.reshape(m, d)
    tabs = _rope_tables(seq)
    hd = DIL_HEADS * DIL_HEAD_DIM
    k_sh = v_sh = None
    bias_all = None

    for window, dilation in DIL_GROUPS:
        assert seq % (dilation * Q_BLOCK) == 0 and window // dilation <= Q_BLOCK
    bq, bk = min(FLASH_BQ, seq), min(FLASH_BK, seq)

    dils = tuple(dilation for _, dilation in DIL_GROUPS)
    gathered = lambda t, dil: t.reshape(bsz, dil, seq // dil, hd)

    for l in range(depth):
        if l == n_a:
            gains = jnp.stack([kv_src_norm] * N_GROUPS + [ffn_norm[l, 0]])
            *xs, xn = _rmsnorm(h, gains, dils + (1,), bsz, seq)
            wkv = _cast_bf16(w_kv_shared)
            k_sh, v_sh = [], []
            for g, dil in enumerate(dils):
                a_g = xs[g].reshape(m, d)
                gk = jnp.tile(k_norm_shared[g], DIL_HEADS)[None, :]
                k_sh.append(gathered(_mm_headnorm(a_g, wkv, gk, DIL_HEAD_DIM, col0=g * hd), dil))
                v_sh.append(gathered(_mm_plain(a_g, wkv, BF16, n=hd, col0=(N_GROUPS + g) * hd), dil))
            bias_all = _band_bias(rel_bias, _band_buckets(), DIL_HEADS)
        else:
            (xn,) = _rmsnorm(h, ffn_norm[l, 0][None, :], (1,), bsz, seq)
        h = _ffn(h, xn, ffn_wg, ffn_wu, ffn_wd, (l, 0))

        if l < n_a:
            (xn,) = _rmsnorm(h, attn_norm[l][None, :], (1,), bsz, seq)
            a = l
            q_lora = mla_wdq.shape[2]
            kv_lora = mla_wdkv.shape[2] - QK_ROPE
            wcat = jnp.concatenate(
                [mla_wdq[a], mla_wdkv[a], jnp.zeros((d, LANES - QK_ROPE), F32)], axis=1).astype(BF16)
            cq, ckv, k_rope = _mla_down(xn, wcat, mla_q_lora_norm[a][None, :],
                                        mla_kv_lora_norm[a][None, :], q_lora, kv_lora)
            wuq_p = jnp.pad(mla_wuq[a].reshape(q_lora, MLA_HEADS, QK_HEAD),
                            ((0, 0), (0, 0), (0, QK_PAD - QK_HEAD))).reshape(q_lora, -1).astype(BF16)
            q_scale = QK_HEAD ** -0.5 * math.log2(math.e)
            gq_p = jnp.pad(mla_q_norm[a] * q_scale, (0, QK_PAD - QK_HEAD))[None, :]
            q = _mla_q(cq, wuq_p, gq_p, tabs, seq)
            g_nope = mla_k_norm[a][:QK_NOPE][None, :]
            g_rope = jnp.pad(mla_k_norm[a][QK_NOPE:], (0, LANES - QK_ROPE))[None, :]
            k, vt = _mla_kv(ckv, _cast_bf16(mla_wukv, (a,)), k_rope, g_nope, g_rope, tabs, bsz, seq, bk)
            o = _mla_flash(q.reshape(bsz, seq, -1), k.reshape(bsz, seq, -1), vt, bq, bk)
            h = _mm_residual(o.reshape(m, -1), _cast_bf16(mla_wo, (a,)), h, 1.0)
        else:
            bl = l - n_a
            xq = _rmsnorm(h, jnp.stack([attn_norm[l]] * N_GROUPS), dils, bsz, seq)
            wq = _cast_bf16(dil_wq, (bl,))
            outs, lses = [], []
            for g, dil in enumerate(dils):
                gq = jnp.tile(dil_q_norm[bl][g], DIL_HEADS)[None, :]
                q_g = gathered(_mm_headnorm(xq[g].reshape(m, d), wq, gq, DIL_HEAD_DIM, col0=g * hd), dil)
                o_g, l_g = _dilated_group(q_g, k_sh[g], v_sh[g], bias_all[g])
                outs.append(o_g)
                lses.append(l_g)
            o = _combine_groups(outs, lses, seq)
            h = _mm_residual(o, _cast_bf16(dil_wo, (bl,)), h, 1.0)

        (xn,) = _rmsnorm(h, ffn_norm[l, 1][None, :], (1,), bsz, seq)
        h = _ffn(h, xn, ffn_wg, ffn_wu, ffn_wd, (l, 1))

    return h.reshape(bsz, seq, d)
```

```python
import functools
import math

import jax
import jax.numpy as jnp
import numpy as np
from jax import lax
from jax.experimental import pallas as pl
from jax.experimental.pallas import tpu as pltpu

F32 = jnp.float32
BF16 = jnp.bfloat16

RMS_EPS = 1e-6
MLA_HEADS = 32
QK_NOPE = 128
QK_ROPE = 64
QK_HEAD = QK_NOPE + QK_ROPE
V_HEAD = 128
ROPE_THETA = 10000.0
DIL_GROUPS = ((128, 1), (512, 4), (2048, 16))
N_GROUPS = 3
DIL_HEADS = 32
DIL_HEAD_DIM = 128
NUM_BUCKETS = 32
MAX_DISTANCE = 2048
Q_BLOCK = 128

LANES = 128
QK_PAD = 2 * LANES
V7X_VMEM_BYTES = 64 * 1024 * 1024
VMEM_HEADROOM_BYTES = 8 * 1024 * 1024
MASK_VALUE = -1e30


def _nbytes(shape, dtype):
    return int(np.prod(shape)) * jnp.dtype(dtype).itemsize


def _params(semantics, block_bytes, scratch_bytes=0):
    need = 2 * block_bytes + scratch_bytes + VMEM_HEADROOM_BYTES
    limit = min(max(need, 32 * 1024 * 1024), V7X_VMEM_BYTES - 4 * 1024 * 1024)
    return pltpu.CompilerParams(dimension_semantics=semantics, vmem_limit_bytes=int(limit))


def _tile(dim, pref):
    if dim <= pref:
        return dim
    t = pref
    while dim % t:
        t //= 2
    return t


def _gather_perm(tm, dil):
    n = tm // dil
    i = np.arange(tm)
    p = np.zeros((tm, tm), np.float32)
    p[i, (i % n) * dil + i // n] = 1.0
    return jnp.asarray(p, BF16)


def _rmsnorm_kernel(x_ref, g_ref, *refs, dilations):
    n_perm = sum(dil > 1 for dil in dilations)
    p_refs, o_refs = refs[:n_perm], refs[n_perm:]
    x = x_ref[...]
    y = x * lax.rsqrt(jnp.mean(x * x, axis=-1, keepdims=True) + RMS_EPS)
    tm = x.shape[0]
    ip = 0
    for i, (o_ref, dil) in enumerate(zip(o_refs, dilations)):
        yg = (y * g_ref[i:i + 1, :]).astype(o_ref.dtype)
        if dil == 1:
            o_ref[...] = yg
        else:
            yp = _dot(p_refs[ip][...], yg).astype(o_ref.dtype)
            ip += 1
            n = tm // dil
            for r in range(dil):
                o_ref[r] = yp[r * n:(r + 1) * n, :]


def _rmsnorm(x, gains, dilations, bsz, seq, tm=256):
    m, d = x.shape
    n = gains.shape[0]
    tm = _tile(seq, tm)
    nsb = seq // tm
    out_specs, out_shape, perms = [], [], []
    for dil in dilations:
        if dil == 1:
            out_specs.append(pl.BlockSpec((tm, d), lambda i: (i, 0)))
            out_shape.append(jax.ShapeDtypeStruct((m, d), BF16))
        else:
            assert tm % (16 * dil) == 0
            perms.append(_gather_perm(tm, dil))
            out_specs.append(pl.BlockSpec((None, dil, tm // dil, d), lambda i: (i // nsb, 0, i % nsb, 0)))
            out_shape.append(jax.ShapeDtypeStruct((bsz, dil, seq // dil, d), BF16))
    blocks = (_nbytes((tm, d), F32) + n * _nbytes((tm, d), BF16) + _nbytes((n, d), F32)
              + len(perms) * _nbytes((tm, tm), BF16))
    return pl.pallas_call(
        functools.partial(_rmsnorm_kernel, dilations=tuple(dilations)),
        grid=(m // tm,),
        in_specs=[pl.BlockSpec((tm, d), lambda i: (i, 0)),
                  pl.BlockSpec((n, d), lambda i: (0, 0))]
                 + [pl.BlockSpec((tm, tm), lambda i: (0, 0)) for _ in perms],
        out_specs=out_specs,
        out_shape=out_shape,
        compiler_params=_params(("parallel",), blocks, scratch_bytes=3 * _nbytes((tm, d), F32)),
        name="rmsnorm",
    )(x, gains, *perms)


CAST_BLOCK_BYTES = 12 * 1024 * 1024


def _cast_kernel(x_ref, o_ref, *, rows_in, cols_in, tr):
    x = x_ref[...].astype(o_ref.dtype)
    if o_ref.shape[1] > cols_in:
        o_ref[:, cols_in:] = jnp.zeros((tr, o_ref.shape[1] - cols_in), o_ref.dtype)
    row = pl.program_id(0) * tr + lax.broadcasted_iota(jnp.int32, x.shape, 0)
    o_ref[:, :cols_in] = jnp.where(row < rows_in, x, jnp.zeros_like(x))


def _cast_bf16(w, lead=(), rows_out=None, cols_out=None):
    r, c = w.shape[-2:]
    rows_out = rows_out or r
    cols_out = cols_out or c
    tr = 8
    while tr * 2 * c * 4 <= CAST_BLOCK_BYTES and r % (tr * 2) == 0 and rows_out % (tr * 2) == 0:
        tr *= 2
    assert r % tr == 0 and rows_out % tr == 0 and c % LANES == 0
    n_in = r // tr
    squeeze = (None,) * len(lead)
    blocks = _nbytes((tr, c), F32) + _nbytes((tr, cols_out), BF16)
    return pl.pallas_call(
        functools.partial(_cast_kernel, rows_in=r, cols_in=c, tr=tr),
        grid=(rows_out // tr,),
        in_specs=[pl.BlockSpec(squeeze + (tr, c), lambda i: tuple(lead) + (jnp.minimum(i, n_in - 1), 0))],
        out_specs=pl.BlockSpec((tr, cols_out), lambda i: (i, 0)),
        out_shape=jax.ShapeDtypeStruct((rows_out, cols_out), BF16),
        compiler_params=_params(("parallel",), blocks),
        name="cast_bf16",
    )(w)


def _dot(a, b):
    return jnp.dot(a, b, preferred_element_type=F32)


def _row_rms_scale(ss_ref, width):
    return lax.rsqrt(jnp.sum(ss_ref[...], axis=-1, keepdims=True) * (1.0 / width) + RMS_EPS)


def _swiglu_kernel(a_ref, wg_ref, wu_ref, *refs):
    o_ref = refs[-1]
    a = a_ref[...]
    g = _dot(a, wg_ref[...])
    u = _dot(a, wu_ref[...])
    if len(refs) == 2:
        rs = _row_rms_scale(refs[0], a.shape[1])
        g = g * rs
        u = u * rs
    o_ref[...] = (g * jax.nn.sigmoid(g) * u).astype(o_ref.dtype)


def _mm_swiglu(a, wg, wu, ss=None, tm=1024, tn=512):
    m, k = a.shape
    n = wg.shape[1]
    tm, tn = _tile(m, tm), _tile(n, tn)
    blocks = _nbytes((tm, k), BF16) + 2 * _nbytes((k, tn), BF16) + _nbytes((tm, tn), BF16)
    ss_args = () if ss is None else (ss,)
    ss_specs = [] if ss is None else [pl.BlockSpec((tm, LANES), lambda i, j: (i, 0))]
    return pl.pallas_call(
        _swiglu_kernel,
        grid=(m // tm, n // tn),
        in_specs=[pl.BlockSpec((tm, k), lambda i, j: (i, 0)),
                  pl.BlockSpec((k, tn), lambda i, j: (0, j)),
                  pl.BlockSpec((k, tn), lambda i, j: (0, j))] + ss_specs,
        out_specs=pl.BlockSpec((tm, tn), lambda i, j: (i, j)),
        out_shape=jax.ShapeDtypeStruct((m, n), BF16),
        compiler_params=_params(("parallel", "parallel"), blocks,
                                scratch_bytes=2 * _nbytes((tm, tn), F32)),
        name="ffn_gate_up",
    )(a, wg, wu, *ss_args)


def _residual_kernel(*refs, scale, nk, emit_norm):
    acc_ref = refs[-1] if nk > 1 else None
    if emit_norm:
        a_ref, b_ref, r_ref, g_ref, o_ref, xt_ref, ss_ref = refs[:7]
    else:
        a_ref, b_ref, r_ref, o_ref = refs[:4]
    j = pl.program_id(1)
    k = pl.program_id(2)
    part = _dot(a_ref[...], b_ref[...])

    def finish(total):
        h = r_ref[...] + scale * total
        o_ref[...] = h
        if emit_norm:
            xt_ref[...] = (h * g_ref[...]).astype(xt_ref.dtype)
            partial = jnp.sum(h * h, axis=-1, keepdims=True)
            lane = lax.broadcasted_iota(jnp.int32, ss_ref.shape, 1)

            @pl.when(j == 0)
            def _():
                ss_ref[...] = jnp.where(lane == 0, partial, 0.0)

            @pl.when(j > 0)
            def _():
                ss_ref[...] = jnp.where(lane == j, partial, ss_ref[...])

    if nk == 1:
        finish(part)
        return

    @pl.when(k == 0)
    def _():
        acc_ref[...] = part

    @pl.when(jnp.logical_and(k > 0, k < nk - 1))
    def _():
        acc_ref[...] += part

    @pl.when(k == nk - 1)
    def _():
        finish(acc_ref[...] + part)


def _mm_residual(a, b, res, scale, norm_gain=None, tm=1024, tn=1024, tk_max=4096):
    m, kdim = a.shape
    n = b.shape[1]
    nk = 1
    while kdim // nk > tk_max or kdim % nk or (kdim // nk) % LANES:
        nk += 1
    tk = kdim // nk
    emit = norm_gain is not None
    if emit and nk == 1:
        tn //= 2
    tm, tn = _tile(m, tm), _tile(n, tn)
    assert n // tn <= LANES
    blocks = (_nbytes((tm, tk), BF16) + _nbytes((tk, tn), BF16) + 2 * _nbytes((tm, tn), F32)
              + emit * (_nbytes((tm, tn), BF16) + _nbytes((tm, LANES), F32)))
    h_spec = pl.BlockSpec((tm, tn), lambda i, j, k: (i, j))
    in_specs = [pl.BlockSpec((tm, tk), lambda i, j, k: (i, k)),
                pl.BlockSpec((tk, tn), lambda i, j, k: (k, j)),
                h_spec]
    out_specs, out_shape, args = h_spec, jax.ShapeDtypeStruct((m, n), F32), (a, b, res)
    if emit:
        in_specs.append(pl.BlockSpec((1, tn), lambda i, j, k: (0, j)))
        out_specs = [h_spec, h_spec, pl.BlockSpec((tm, LANES), lambda i, j, k: (i, 0))]
        out_shape = [out_shape, jax.ShapeDtypeStruct((m, n), BF16), jax.ShapeDtypeStruct((m, LANES), F32)]
        args = (a, b, res, norm_gain)
    return pl.pallas_call(
        functools.partial(_residual_kernel, scale=scale, nk=nk, emit_norm=emit),
        grid=(m // tm, n // tn, nk),
        in_specs=in_specs,
        out_specs=out_specs,
        out_shape=out_shape,
        scratch_shapes=[pltpu.VMEM((tm, tn), F32)] if nk > 1 else [],
        compiler_params=_params(("parallel", "arbitrary" if emit else "parallel", "arbitrary"), blocks,
                                scratch_bytes=2 * _nbytes((tm, tn), F32)),
        name="matmul_residual",
    )(*args)


def _plain_kernel(a_ref, b_ref, o_ref):
    o_ref[...] = _dot(a_ref[...], b_ref[...]).astype(o_ref.dtype)


def _mm_plain(a, b, out_dtype, n=None, col0=0, tm=1024, tn=1024):
    m, k = a.shape
    n = n or b.shape[1]
    tm, tn = _tile(m, tm), _tile(n, tn)
    assert col0 % tn == 0
    joff = col0 // tn
    blocks = _nbytes((tm, k), BF16) + _nbytes((k, tn), BF16) + _nbytes((tm, tn), out_dtype)
    return pl.pallas_call(
        _plain_kernel,
        grid=(m // tm, n // tn),
        in_specs=[pl.BlockSpec((tm, k), lambda i, j: (i, 0)),
                  pl.BlockSpec((k, tn), lambda i, j: (0, j + joff))],
        out_specs=pl.BlockSpec((tm, tn), lambda i, j: (i, j)),
        out_shape=jax.ShapeDtypeStruct((m, n), out_dtype),
        compiler_params=_params(("parallel", "parallel"), blocks,
                                scratch_bytes=_nbytes((tm, tn), F32)),
        name="matmul_plain",
    )(a, b)


def _headnorm_kernel(a_ref, b_ref, g_ref, o_ref, *, head_dim):
    x = _dot(a_ref[...], b_ref[...])
    tn = x.shape[1]
    for h in range(tn // head_dim):
        sl = slice(h * head_dim, (h + 1) * head_dim)
        xh = x[:, sl]
        y = xh * lax.rsqrt(jnp.mean(xh * xh, axis=-1, keepdims=True) + RMS_EPS)
        o_ref[:, sl] = (y * g_ref[:, sl]).astype(o_ref.dtype)


def _mm_headnorm(a, b, gain_row, head_dim, col0=0, tm=1024, tn=1024):
    m, k = a.shape
    n = gain_row.shape[1]
    tm, tn = _tile(m, tm), _tile(n, tn)
    assert col0 % tn == 0
    joff = col0 // tn
    blocks = (_nbytes((tm, k), BF16) + _nbytes((k, tn), BF16) + _nbytes((tm, tn), BF16)
              + _nbytes((8, tn), F32))
    return pl.pallas_call(
        functools.partial(_headnorm_kernel, head_dim=head_dim),
        grid=(m // tm, n // tn),
        in_specs=[pl.BlockSpec((tm, k), lambda i, j: (i, 0)),
                  pl.BlockSpec((k, tn), lambda i, j: (0, j + joff)),
                  pl.BlockSpec((1, tn), lambda i, j: (0, j))],
        out_specs=pl.BlockSpec((tm, tn), lambda i, j: (i, j)),
        out_shape=jax.ShapeDtypeStruct((m, n), BF16),
        compiler_params=_params(("parallel", "parallel"), blocks,
                                scratch_bytes=_nbytes((tm, tn), F32)),
        name="matmul_headnorm",
    )(a, b, gain_row)


def _mla_down_kernel(a_ref, w_ref, ss_ref, gq_ref, gkv_ref, cq_ref, ckv_ref, kr_ref, *, q_lora, kv_lora):
    x = _dot(a_ref[...], w_ref[...]) * _row_rms_scale(ss_ref, a_ref.shape[1])
    xq = x[:, :q_lora]
    cq = xq * lax.rsqrt(jnp.mean(xq * xq, axis=-1, keepdims=True) + RMS_EPS) * gq_ref[...]
    cq_ref[...] = cq.astype(cq_ref.dtype)
    xkv = x[:, q_lora:q_lora + kv_lora]
    ckv = xkv * lax.rsqrt(jnp.mean(xkv * xkv, axis=-1, keepdims=True) + RMS_EPS) * gkv_ref[...]
    ckv_ref[...] = ckv.astype(ckv_ref.dtype)
    kr_ref[...] = x[:, q_lora + kv_lora:]


def _mla_down(a, ss, wcat, gq, gkv, q_lora, kv_lora, tm=512):
    m, k = a.shape
    n = wcat.shape[1]
    tm = _tile(m, tm)
    blocks = (_nbytes((tm, k), BF16) + _nbytes((k, n), BF16) + _nbytes((tm, q_lora), BF16)
              + _nbytes((tm, kv_lora), BF16) + 2 * _nbytes((tm, LANES), F32))
    return pl.pallas_call(
        functools.partial(_mla_down_kernel, q_lora=q_lora, kv_lora=kv_lora),
        grid=(m // tm,),
        in_specs=[pl.BlockSpec((tm, k), lambda i: (i, 0)),
                  pl.BlockSpec((k, n), lambda i: (0, 0)),
                  pl.BlockSpec((tm, LANES), lambda i: (i, 0)),
                  pl.BlockSpec((1, q_lora), lambda i: (0, 0)),
                  pl.BlockSpec((1, kv_lora), lambda i: (0, 0))],
        out_specs=[pl.BlockSpec((tm, q_lora), lambda i: (i, 0)),
                   pl.BlockSpec((tm, kv_lora), lambda i: (i, 0)),
                   pl.BlockSpec((tm, LANES), lambda i: (i, 0))],
        out_shape=[jax.ShapeDtypeStruct((m, q_lora), BF16),
                   jax.ShapeDtypeStruct((m, kv_lora), BF16),
                   jax.ShapeDtypeStruct((m, LANES), F32)],
        compiler_params=_params(("parallel",), blocks, scratch_bytes=_nbytes((tm, n), F32)),
        name="mla_down",
    )(a, wcat, ss, gq, gkv)


def _rope_lane_tile(x, cos, sin_lo, sin_hi):
    half = QK_ROPE // 2
    return (x * cos + pltpu.roll(x, LANES - half, axis=1) * sin_lo
            + pltpu.roll(x, half, axis=1) * sin_hi)


def _mla_q_kernel(a_ref, w_ref, g_ref, cos_ref, slo_ref, shi_ref, o_ref):
    x = _dot(a_ref[...], w_ref[...])
    cos, slo, shi = cos_ref[...], slo_ref[...], shi_ref[...]
    for h in range(x.shape[1] // QK_PAD):
        xh = x[:, h * QK_PAD:(h + 1) * QK_PAD]
        ms = jnp.sum(xh * xh, axis=-1, keepdims=True) * (1.0 / QK_HEAD)
        y = xh * lax.rsqrt(ms + RMS_EPS) * g_ref[...]
        o_ref[:, h * QK_PAD:h * QK_PAD + LANES] = y[:, :LANES].astype(o_ref.dtype)
        o_ref[:, h * QK_PAD + LANES:(h + 1) * QK_PAD] = _rope_lane_tile(
            y[:, LANES:], cos, slo, shi).astype(o_ref.dtype)


def _mla_q(cq, wuq_p, gain_p, tabs, seq, tm=1024, tn=1024):
    m, k = cq.shape
    n = wuq_p.shape[1]
    tm, tn = _tile(seq, tm), _tile(n, tn)
    nsb = seq // tm
    tab_spec = pl.BlockSpec((tm, LANES), lambda i, j: (i % nsb, 0))
    blocks = (_nbytes((tm, k), BF16) + _nbytes((k, tn), BF16) + _nbytes((tm, tn), BF16)
              + 3 * _nbytes((tm, LANES), F32))
    return pl.pallas_call(
        _mla_q_kernel,
        grid=(m // tm, n // tn),
        in_specs=[pl.BlockSpec((tm, k), lambda i, j: (i, 0)),
                  pl.BlockSpec((k, tn), lambda i, j: (0, j)),
                  pl.BlockSpec((1, QK_PAD), lambda i, j: (0, 0)),
                  tab_spec, tab_spec, tab_spec],
        out_specs=pl.BlockSpec((tm, tn), lambda i, j: (i, j)),
        out_shape=jax.ShapeDtypeStruct((m, n), BF16),
        compiler_params=_params(("parallel", "parallel"), blocks,
                                scratch_bytes=_nbytes((tm, tn), F32)),
        name="mla_q_proj",
    )(cq, wuq_p, gain_p, *tabs)


FLASH_BQ = 2048
FLASH_BK = 1024


def _mla_kv_kernel(a_ref, w_ref, kr_ref, gn_ref, gr_ref, cos_ref, slo_ref, shi_ref, k_ref, vt_ref, *, bk):
    x = _dot(a_ref[...], w_ref[...])
    kr = kr_ref[...]
    ss_rope = jnp.sum(kr * kr, axis=-1, keepdims=True)
    kr_roped = _rope_lane_tile(kr * gr_ref[...], cos_ref[...], slo_ref[...], shi_ref[...])
    width = QK_NOPE + V_HEAD
    for h in range(x.shape[1] // width):
        kn = x[:, h * width:h * width + QK_NOPE]
        ms = (jnp.sum(kn * kn, axis=-1, keepdims=True) + ss_rope) * (1.0 / QK_HEAD)
        rs = lax.rsqrt(ms + RMS_EPS)
        k_ref[:, h * QK_PAD:h * QK_PAD + LANES] = (kn * rs * gn_ref[...]).astype(k_ref.dtype)
        k_ref[:, h * QK_PAD + LANES:(h + 1) * QK_PAD] = (kr_roped * rs).astype(k_ref.dtype)
        for c in range(x.shape[0] // bk):
            v = x[c * bk:(c + 1) * bk, h * width + QK_NOPE:(h + 1) * width]
            vt_ref[h, c] = v.T.astype(vt_ref.dtype)


def _mla_kv(ckv, wukv, k_rope, g_nope, g_rope, tabs, bsz, seq, bk, tm=1024, tn=1024):
    m, k = ckv.shape
    n = wukv.shape[1]
    width = QK_NOPE + V_HEAD
    tm, tn = _tile(seq, tm), _tile(n, tn)
    assert tm % bk == 0
    hpt = tn // width
    nsb = seq // tm
    tab_spec = pl.BlockSpec((tm, LANES), lambda i, j: (i % nsb, 0))
    blocks = (_nbytes((tm, k), BF16) + _nbytes((k, tn), BF16) + _nbytes((tm, hpt * QK_PAD), BF16)
              + _nbytes((tm, hpt * V_HEAD), BF16) + 4 * _nbytes((tm, LANES), F32))
    return pl.pallas_call(
        functools.partial(_mla_kv_kernel, bk=bk),
        grid=(m // tm, n // tn),
        in_specs=[pl.BlockSpec((tm, k), lambda i, j: (i, 0)),
                  pl.BlockSpec((k, tn), lambda i, j: (0, j)),
                  pl.BlockSpec((tm, LANES), lambda i, j: (i, 0)),
                  pl.BlockSpec((1, LANES), lambda i, j: (0, 0)),
                  pl.BlockSpec((1, LANES), lambda i, j: (0, 0)),
                  tab_spec, tab_spec, tab_spec],
        out_specs=[pl.BlockSpec((tm, hpt * QK_PAD), lambda i, j: (i, j)),
                   pl.BlockSpec((None, hpt, tm // bk, V_HEAD, bk), lambda i, j: (i // nsb, j, i % nsb, 0, 0))],
        out_shape=[jax.ShapeDtypeStruct((m, (n // width) * QK_PAD), BF16),
                   jax.ShapeDtypeStruct((bsz, n // width, seq // bk, V_HEAD, bk), BF16)],
        compiler_params=_params(("parallel", "parallel"), blocks,
                                scratch_bytes=2 * _nbytes((tm, tn), F32)),
        name="mla_kv_proj",
    )(ckv, wukv, k_rope, g_nope, g_rope, *tabs)


def _dot_nt(a, b):
    return lax.dot_general(a, b, (((1,), (1,)), ((), ())), preferred_element_type=F32)


def _flash_kernel(q_ref, k_ref, vt_ref, o_ref, m_ref, l_ref, acc_ref, s_ref, *, bq, bk):
    qi = pl.program_id(2)
    half = bk // 2
    m_ref[...] = jnp.full(m_ref.shape, MASK_VALUE, F32)
    l_ref[...] = jnp.zeros(l_ref.shape, F32)
    acc_ref[...] = jnp.zeros(acc_ref.shape, F32)

    def scores(j, slot, c0=0):
        kb = k_ref[pl.ds(pl.multiple_of(j * bk, bk), bk), :]
        s_ref[slot, :, c0:] = _dot_nt(kb, q_ref[c0:, :])

    def softmax_pv(j, slot, r0=0, nr=bk, c0=0, diagonal=False):
        st = s_ref[slot, r0:r0 + nr, c0:]
        if diagonal:
            row = lax.broadcasted_iota(jnp.int32, (nr, nr), 0)
            col = lax.broadcasted_iota(jnp.int32, (nr, nr), 1)
            tri = jnp.where(row <= col, st[:, :nr], MASK_VALUE)
            st = tri if st.shape[1] == nr else jnp.concatenate([tri, st[:, nr:]], axis=1)
        m_old = m_ref[:, c0:]
        m_new = jnp.maximum(m_old, jnp.max(st, axis=0, keepdims=True))
        p = jnp.exp2(st - m_new)
        alpha = jnp.exp2(m_old - m_new)
        l_ref[:, c0:] = alpha * l_ref[:, c0:] + jnp.sum(p, axis=0, keepdims=True)
        acc_ref[:, c0:] = alpha * acc_ref[:, c0:] + _dot(vt_ref[j, :, r0:r0 + nr], p.astype(BF16))
        m_ref[:, c0:] = m_new

    def body(i, carry):
        scores(2 * i + 1, 1)
        softmax_pv(2 * i, 0)
        scores(2 * i + 2, 0)
        softmax_pv(2 * i + 1, 1)
        return carry

    scores(0, 0)
    lax.fori_loop(0, qi, body, 0)
    scores(2 * qi + 1, 1, c0=bk)
    for g in range(4):
        softmax_pv(2 * qi + g // 2, g // 2, r0=(g % 2) * half, nr=half, c0=g * half, diagonal=True)
    o_ref[...] = (acc_ref[...] / l_ref[...]).T.astype(o_ref.dtype)


def _mla_flash(q, k, vt, bq, bk):
    b, s, _ = q.shape
    h = q.shape[2] // QK_PAD
    assert s % bq == 0 and bq == 2 * bk
    blocks = (_nbytes((bq, QK_PAD), BF16) + _nbytes((s, QK_PAD), BF16) + _nbytes((s, V_HEAD), BF16)
              + _nbytes((bq, V_HEAD), BF16))
    scratch = 2 * _nbytes((8, bq), F32) + _nbytes((V_HEAD, bq), F32) + 6 * _nbytes((bk, bq), F32)
    return pl.pallas_call(
        functools.partial(_flash_kernel, bq=bq, bk=bk),
        grid=(b, h, s // bq),
        in_specs=[pl.BlockSpec((None, bq, QK_PAD), lambda bi, hi, qi: (bi, qi, hi)),
                  pl.BlockSpec((None, s, QK_PAD), lambda bi, hi, qi: (bi, 0, hi)),
                  pl.BlockSpec((None, None, s // bk, V_HEAD, bk), lambda bi, hi, qi: (bi, hi, 0, 0, 0))],
        out_specs=pl.BlockSpec((None, bq, V_HEAD), lambda bi, hi, qi: (bi, qi, hi)),
        out_shape=jax.ShapeDtypeStruct((b, s, h * V_HEAD), BF16),
        scratch_shapes=[pltpu.VMEM((1, bq), F32), pltpu.VMEM((1, bq), F32),
                        pltpu.VMEM((V_HEAD, bq), F32), pltpu.VMEM((2, bk, bq), F32)],
        compiler_params=_params(("parallel", "parallel", "arbitrary"), blocks, scratch_bytes=scratch),
        name="mla_flash_attention",
    )(q, k, vt)


def _t5_causal_bucket(dist):
    max_exact = NUM_BUCKETS // 2
    n = jnp.maximum(dist, 0)
    nf = jnp.maximum(n, 1).astype(F32)
    large = max_exact + (jnp.log(nf / max_exact) / math.log(MAX_DISTANCE / max_exact)
                         * (NUM_BUCKETS - max_exact)).astype(jnp.int32)
    large = jnp.minimum(large, NUM_BUCKETS - 1)
    return jnp.where(n < max_exact, n, large)


def _band_bias_kernel(tab_ref, bucket_ref, o_ref, *, heads):
    g = pl.program_id(0)
    h = pl.program_id(1)
    bucket = bucket_ref[...]
    acc = jnp.full(bucket.shape, MASK_VALUE, F32)
    for b in range(NUM_BUCKETS):
        acc = jnp.where(bucket == b, tab_ref[b, g * heads + h], acc)
    o_ref[...] = acc


def _band_bias(rel_bias, buckets, heads):
    g = buckets.shape[0]
    blk = buckets.shape[1:]
    blocks = _nbytes(blk, jnp.int32) + _nbytes(blk, F32)
    return pl.pallas_call(
        functools.partial(_band_bias_kernel, heads=heads),
        grid=(g, heads),
        in_specs=[pl.BlockSpec(memory_space=pltpu.SMEM),
                  pl.BlockSpec((None,) + blk, lambda gi, hi: (gi, 0, 0))],
        out_specs=pl.BlockSpec((None, None) + blk, lambda gi, hi: (gi, hi, 0, 0)),
        out_shape=jax.ShapeDtypeStruct((g, heads) + blk, F32),
        compiler_params=_params(("parallel", "parallel"), blocks),
        name="dilated_band_bias",
    )(rel_bias, buckets)


DIL_HEAD_GROUP = 4


def _dilated_kernel(q_ref, kp_ref, kc_ref, vp_ref, vc_ref, bias_ref, o_ref, lse_ref, s_ref, *, scale, heads):
    n = pl.program_id(2)
    has_prev = n > 0
    d = DIL_HEAD_DIM
    hg = s_ref.shape[1]
    lane = lax.broadcasted_iota(jnp.int32, (Q_BLOCK, LANES), 1)
    col = lax.broadcasted_iota(jnp.int32, (Q_BLOCK, 2 * Q_BLOCK), 1)
    keep = jnp.logical_or(has_prev, col >= Q_BLOCK)

    def scores(grp, slot):
        for i in range(hg):
            sl = slice((grp * hg + i) * d, (grp * hg + i + 1) * d)
            kcat = jnp.concatenate([kp_ref[:, sl], kc_ref[:, sl]], axis=0)
            s_ref[slot, i] = _dot_nt(q_ref[:, sl], kcat)

    def softmax_pv(grp, slot, lse_tile):
        for i in range(hg):
            h = grp * hg + i
            sl = slice(h * d, (h + 1) * d)
            s = jnp.where(keep, s_ref[slot, i] * scale + bias_ref[h], MASK_VALUE)
            m = jnp.max(s, axis=-1, keepdims=True)
            p = jnp.exp(s - m)
            den = jnp.sum(p, axis=-1, keepdims=True)
            vcat = jnp.concatenate([vp_ref[:, sl], vc_ref[:, sl]], axis=0)
            o_ref[:, sl] = _dot((p / den).astype(BF16), vcat).astype(o_ref.dtype)
            lse_tile = jnp.where(lane == h, m + jnp.log(den), lse_tile)
        return lse_tile

    lse_tile = jnp.zeros((Q_BLOCK, LANES), F32)
    n_grp = heads // hg
    scores(0, 0)
    for grp in range(n_grp):
        if grp + 1 < n_grp:
            scores(grp + 1, (grp + 1) % 2)
        lse_tile = softmax_pv(grp, grp % 2, lse_tile)
    lse_ref[...] = lse_tile


def _dilated_group(q, k, v, bias_g):
    b, dilation, length, hd = q.shape
    hg = math.gcd(DIL_HEAD_GROUP, DIL_HEADS)
    nb = length // Q_BLOCK
    cur = lambda bi, r, n: (bi, r, n, 0)
    prev = lambda bi, r, n: (bi, r, jnp.maximum(n - 1, 0), 0)
    blk = (None, None, Q_BLOCK, hd)
    blocks = 6 * _nbytes((Q_BLOCK, hd), BF16) + _nbytes(bias_g.shape, F32) + _nbytes((Q_BLOCK, LANES), F32)
    return pl.pallas_call(
        functools.partial(_dilated_kernel, scale=DIL_HEAD_DIM ** -0.5, heads=DIL_HEADS),
        grid=(b, dilation, nb),
        in_specs=[pl.BlockSpec(blk, cur),
                  pl.BlockSpec(blk, prev), pl.BlockSpec(blk, cur),
                  pl.BlockSpec(blk, prev), pl.BlockSpec(blk, cur),
                  pl.BlockSpec(bias_g.shape, lambda bi, r, n: (0, 0, 0))],
        out_specs=[pl.BlockSpec(blk, cur),
                   pl.BlockSpec((None, None, Q_BLOCK, LANES), cur)],
        out_shape=[jax.ShapeDtypeStruct((b, dilation, length, hd), BF16),
                   jax.ShapeDtypeStruct((b, dilation, length, LANES), F32)],
        scratch_shapes=[pltpu.VMEM((2, hg, Q_BLOCK, 2 * Q_BLOCK), F32)],
        compiler_params=_params(("parallel", "parallel", "arbitrary"), blocks,
                                scratch_bytes=2 * hg * _nbytes((Q_BLOCK, 2 * Q_BLOCK), F32)),
        name=f"dilated_attention_d{dilation}",
    )(q, k, k, v, v, bias_g)


def _combine_kernel(*refs, heads, dilations):
    ng = len(dilations)
    n_perm = sum(dil > 1 for dil in dilations)
    o_refs, l_refs = refs[:ng], refs[ng:2 * ng]
    p_refs = refs[2 * ng:2 * ng + n_perm]
    o_ref, lse_scr, o_scr = refs[2 * ng + n_perm:]
    tm = o_ref.shape[0]
    ip = 0
    for g, dil in enumerate(dilations):
        for r in range(dil):
            rows = pl.ds(r, tm // dil, stride=dil) if dil > 1 else slice(None)
            lse_scr[g, rows, :] = l_refs[g][r]
        og = o_refs[g][...].reshape(tm, o_ref.shape[1])
        if dil > 1:
            o_scr[g] = _dot(p_refs[ip][...], og)
            ip += 1
        else:
            o_scr[g] = og.astype(F32)
    lse = [lse_scr[g] for g in range(ng)]
    m = functools.reduce(jnp.maximum, lse)
    e = [jnp.exp(l - m) for l in lse]
    tot = functools.reduce(jnp.add, e)
    w = [x / tot for x in e]
    d = DIL_HEAD_DIM
    for h in range(heads):
        sl = slice(h * d, (h + 1) * d)
        o = functools.reduce(jnp.add, [w[g][:, h:h + 1] * o_scr[g, :, sl] for g in range(ng)])
        o_ref[:, sl] = o.astype(o_ref.dtype)


def _combine_groups(outs, lses, seq, tm=256):
    dilations = tuple(o.shape[1] for o in outs)
    bsz, hd = outs[0].shape[0], outs[0].shape[3]
    ng = len(outs)
    tm = _tile(seq, tm)
    nsb = seq // tm
    m = bsz * seq
    idx = lambda i: (i // nsb, 0, i % nsb, 0)
    o_specs = [pl.BlockSpec((None, dil, tm // dil, hd), idx) for dil in dilations]
    l_specs = [pl.BlockSpec((None, dil, tm // dil, LANES), idx) for dil in dilations]
    for dil in dilations:
        assert tm % (16 * dil) == 0
    perms = [_gather_perm(tm, dil).T for dil in dilations if dil > 1]
    p_specs = [pl.BlockSpec((tm, tm), lambda i: (0, 0)) for _ in perms]
    blocks = ((ng + 1) * _nbytes((tm, hd), BF16) + ng * _nbytes((tm, LANES), F32)
              + len(perms) * _nbytes((tm, tm), BF16))
    scratch = ng * (_nbytes((tm, LANES), F32) + _nbytes((tm, hd), F32))
    return pl.pallas_call(
        functools.partial(_combine_kernel, heads=DIL_HEADS, dilations=dilations),
        grid=(m // tm,),
        in_specs=o_specs + l_specs + p_specs,
        out_specs=pl.BlockSpec((tm, hd), lambda i: (i, 0)),
        out_shape=jax.ShapeDtypeStruct((m, hd), BF16),
        scratch_shapes=[pltpu.VMEM((ng, tm, LANES), F32), pltpu.VMEM((ng, tm, hd), F32)],
        compiler_params=_params(("parallel",), blocks, scratch_bytes=2 * scratch),
        name="dilated_combine",
    )(*outs, *lses, *perms)


GATE_UP_TILE = 512


def _rope_tables(seq):
    half = QK_ROPE // 2
    inv = ROPE_THETA ** (-jnp.arange(half, dtype=F32) / half)
    ang = jnp.arange(seq).astype(F32)[:, None] * inv[None, :]
    cos, sin = jnp.cos(ang), jnp.sin(ang)
    zeros = jnp.zeros_like(cos)
    cos_t = jnp.concatenate([cos, cos, zeros, zeros], axis=1)
    sin_lo = jnp.concatenate([-sin, zeros, zeros, zeros], axis=1)
    sin_hi = jnp.concatenate([zeros, sin, zeros, zeros], axis=1)
    return cos_t, sin_lo, sin_hi


def _band_buckets():
    r = jnp.arange(Q_BLOCK)[:, None]
    c = jnp.arange(2 * Q_BLOCK)[None, :]
    steps = Q_BLOCK + r - c
    out = []
    for window, dilation in DIL_GROUPS:
        band = (steps >= 0) & (steps <= window // dilation)
        out.append(jnp.where(band, _t5_causal_bucket(steps * dilation), -1))
    return jnp.stack(out).astype(jnp.int32)


def _ffn(h, xn, ss, ffn_wg, ffn_wu, ffn_wd, lead, next_gain=None):
    f = ffn_wg.shape[-1]
    fp = -(-f // GATE_UP_TILE) * GATE_UP_TILE
    wg = _cast_bf16(ffn_wg, lead, cols_out=fp)
    wu = _cast_bf16(ffn_wu, lead, cols_out=fp)
    wd = _cast_bf16(ffn_wd, lead, rows_out=fp)
    act = _mm_swiglu(xn, wg, wu, ss, tn=GATE_UP_TILE)
    return _mm_residual(act, wd, h, 0.5, norm_gain=next_gain)


def kernel(x, ffn_norm, ffn_wg, ffn_wu, ffn_wd, attn_norm, mla_wdq, mla_q_lora_norm, mla_wuq, mla_wdkv, mla_kv_lora_norm, mla_wukv, mla_q_norm, mla_k_norm, mla_wo, kv_src_norm, w_kv_shared, k_norm_shared, rel_bias, dil_wq, dil_q_norm, dil_wo):
    bsz, seq, d = x.shape
    m = bsz * seq
    depth = ffn_norm.shape[0]
    n_a = depth // 2
    h = x.reshape(m, d)
    tabs = _rope_tables(seq)
    hd = DIL_HEADS * DIL_HEAD_DIM
    k_sh = v_sh = None
    bias_all = None

    for window, dilation in DIL_GROUPS:
        assert seq % (dilation * Q_BLOCK) == 0 and window // dilation <= Q_BLOCK
    bq, bk = min(FLASH_BQ, seq), min(FLASH_BK, seq)

    dils = tuple(dilation for _, dilation in DIL_GROUPS)
    gathered = lambda t, dil: t.reshape(bsz, dil, seq // dil, hd)

    pre = None
    for l in range(depth):
        if l == n_a:
            gains = jnp.stack([kv_src_norm] * N_GROUPS + [ffn_norm[l, 0]])
            *xs, xn = _rmsnorm(h, gains, dils + (1,), bsz, seq)
            wkv = _cast_bf16(w_kv_shared)
            k_sh, v_sh = [], []
            for g, dil in enumerate(dils):
                a_g = xs[g].reshape(m, d)
                gk = jnp.tile(k_norm_shared[g], DIL_HEADS)[None, :]
                k_sh.append(gathered(_mm_headnorm(a_g, wkv, gk, DIL_HEAD_DIM, col0=g * hd), dil))
                v_sh.append(gathered(_mm_plain(a_g, wkv, BF16, n=hd, col0=(N_GROUPS + g) * hd), dil))
            bias_all = _band_bias(rel_bias, _band_buckets(), DIL_HEADS)
            ss = None
        elif pre is not None:
            xn, ss = pre
        else:
            (xn,) = _rmsnorm(h, ffn_norm[l, 0][None, :], (1,), bsz, seq)
            ss = None

        if l < n_a:
            h, xn, ss = _ffn(h, xn, ss, ffn_wg, ffn_wu, ffn_wd, (l, 0), next_gain=attn_norm[l][None, :])
            a = l
            q_lora = mla_wdq.shape[2]
            kv_lora = mla_wdkv.shape[2] - QK_ROPE
            wcat = jnp.concatenate(
                [mla_wdq[a], mla_wdkv[a], jnp.zeros((d, LANES - QK_ROPE), F32)], axis=1).astype(BF16)
            cq, ckv, k_rope = _mla_down(xn, ss, wcat, mla_q_lora_norm[a][None, :],
                                        mla_kv_lora_norm[a][None, :], q_lora, kv_lora)
            wuq_p = jnp.pad(mla_wuq[a].reshape(q_lora, MLA_HEADS, QK_HEAD),
                            ((0, 0), (0, 0), (0, QK_PAD - QK_HEAD))).reshape(q_lora, -1).astype(BF16)
            q_scale = QK_HEAD ** -0.5 * math.log2(math.e)
            gq_p = jnp.pad(mla_q_norm[a] * q_scale, (0, QK_PAD - QK_HEAD))[None, :]
            q = _mla_q(cq, wuq_p, gq_p, tabs, seq)
            g_nope = mla_k_norm[a][:QK_NOPE][None, :]
            g_rope = jnp.pad(mla_k_norm[a][QK_NOPE:], (0, LANES - QK_ROPE))[None, :]
            k, vt = _mla_kv(ckv, _cast_bf16(mla_wukv, (a,)), k_rope, g_nope, g_rope, tabs, bsz, seq, bk)
            o = _mla_flash(q.reshape(bsz, seq, -1), k.reshape(bsz, seq, -1), vt, bq, bk)
            h, xn, ss = _mm_residual(o.reshape(m, -1), _cast_bf16(mla_wo, (a,)), h, 1.0,
                                     norm_gain=ffn_norm[l, 1][None, :])
        else:
            h = _ffn(h, xn, ss, ffn_wg, ffn_wu, ffn_wd, (l, 0))
            bl = l - n_a
            xq = _rmsnorm(h, jnp.stack([attn_norm[l]] * N_GROUPS), dils, bsz, seq)
            wq = _cast_bf16(dil_wq, (bl,))
            outs, lses = [], []
            for g, dil in enumerate(dils):
                gq = jnp.tile(dil_q_norm[bl][g], DIL_HEADS)[None, :]
                q_g = gathered(_mm_headnorm(xq[g].reshape(m, d), wq, gq, DIL_HEAD_DIM, col0=g * hd), dil)
                o_g, l_g = _dilated_group(q_g, k_sh[g], v_sh[g], bias_all[g])
                outs.append(o_g)
                lses.append(l_g)
            o = _combine_groups(outs, lses, seq)
            h, xn, ss = _mm_residual(o, _cast_bf16(dil_wo, (bl,)), h, 1.0, norm_gain=ffn_norm[l, 1][None, :])

        chain = l + 1 < depth and l + 1 != n_a
        out = _ffn(h, xn, ss, ffn_wg, ffn_wu, ffn_wd, (l, 1),
                   next_gain=ffn_norm[l + 1, 0][None, :] if chain else None)
        h, pre = (out[0], out[1:]) if chain else (out, None)

    return h.reshape(bsz, seq, d)
```

```python
import functools
import math

import jax
import jax.numpy as jnp
import numpy as np
from jax import lax
from jax.experimental import pallas as pl
from jax.experimental.pallas import tpu as pltpu

F32 = jnp.float32
BF16 = jnp.bfloat16

RMS_EPS = 1e-6
MLA_HEADS = 32
QK_NOPE = 128
QK_ROPE = 64
QK_HEAD = QK_NOPE + QK_ROPE
V_HEAD = 128
ROPE_THETA = 10000.0
DIL_GROUPS = ((128, 1), (512, 4), (2048, 16))
N_GROUPS = 3
DIL_HEADS = 32
DIL_HEAD_DIM = 128
NUM_BUCKETS = 32
MAX_DISTANCE = 2048
Q_BLOCK = 128

LANES = 128
QK_PAD = 2 * LANES
V7X_VMEM_BYTES = 64 * 1024 * 1024
VMEM_HEADROOM_BYTES = 8 * 1024 * 1024
MASK_VALUE = -1e30


def _nbytes(shape, dtype):
    return int(np.prod(shape)) * jnp.dtype(dtype).itemsize


def _params(semantics, block_bytes, scratch_bytes=0):
    need = 2 * block_bytes + scratch_bytes + VMEM_HEADROOM_BYTES
    limit = min(max(need, 32 * 1024 * 1024), V7X_VMEM_BYTES - 4 * 1024 * 1024)
    return pltpu.CompilerParams(dimension_semantics=semantics, vmem_limit_bytes=int(limit))


def _tile(dim, pref):
    if dim <= pref:
        return dim
    t = pref
    while dim % t:
        t //= 2
    return t


def _gather_perm(tm, dil):
    n = tm // dil
    i = np.arange(tm)
    p = np.zeros((tm, tm), np.float32)
    p[i, (i % n) * dil + i // n] = 1.0
    return jnp.asarray(p, BF16)


def _rmsnorm_kernel(x_ref, g_ref, *refs, dilations):
    n_perm = sum(dil > 1 for dil in dilations)
    p_refs, o_refs = refs[:n_perm], refs[n_perm:]
    x = x_ref[...]
    y = x * lax.rsqrt(jnp.mean(x * x, axis=-1, keepdims=True) + RMS_EPS)
    tm = x.shape[0]
    ip = 0
    for i, (o_ref, dil) in enumerate(zip(o_refs, dilations)):
        yg = (y * g_ref[i:i + 1, :]).astype(o_ref.dtype)
        if dil == 1:
            o_ref[...] = yg
        else:
            yp = _dot(p_refs[ip][...], yg).astype(o_ref.dtype)
            ip += 1
            n = tm // dil
            for r in range(dil):
                o_ref[r] = yp[r * n:(r + 1) * n, :]


def _rmsnorm(x, gains, dilations, bsz, seq, tm=256):
    m, d = x.shape
    n = gains.shape[0]
    tm = _tile(seq, tm)
    nsb = seq // tm
    out_specs, out_shape, perms = [], [], []
    for dil in dilations:
        if dil == 1:
            out_specs.append(pl.BlockSpec((tm, d), lambda i: (i, 0)))
            out_shape.append(jax.ShapeDtypeStruct((m, d), BF16))
        else:
            assert tm % (16 * dil) == 0
            perms.append(_gather_perm(tm, dil))
            out_specs.append(pl.BlockSpec((None, dil, tm // dil, d), lambda i: (i // nsb, 0, i % nsb, 0)))
            out_shape.append(jax.ShapeDtypeStruct((bsz, dil, seq // dil, d), BF16))
    blocks = (_nbytes((tm, d), F32) + n * _nbytes((tm, d), BF16) + _nbytes((n, d), F32)
              + len(perms) * _nbytes((tm, tm), BF16))
    return pl.pallas_call(
        functools.partial(_rmsnorm_kernel, dilations=tuple(dilations)),
        grid=(m // tm,),
        in_specs=[pl.BlockSpec((tm, d), lambda i: (i, 0)),
                  pl.BlockSpec((n, d), lambda i: (0, 0))]
                 + [pl.BlockSpec((tm, tm), lambda i: (0, 0)) for _ in perms],
        out_specs=out_specs,
        out_shape=out_shape,
        compiler_params=_params(("parallel",), blocks, scratch_bytes=3 * _nbytes((tm, d), F32)),
        name="rmsnorm",
    )(x, gains, *perms)


CAST_BLOCK_BYTES = 12 * 1024 * 1024


def _cast_kernel(x_ref, o_ref, *, rows_in, cols_in, tr):
    x = x_ref[...].astype(o_ref.dtype)
    if o_ref.shape[1] > cols_in:
        o_ref[:, cols_in:] = jnp.zeros((tr, o_ref.shape[1] - cols_in), o_ref.dtype)
    row = pl.program_id(0) * tr + lax.broadcasted_iota(jnp.int32, x.shape, 0)
    o_ref[:, :cols_in] = jnp.where(row < rows_in, x, jnp.zeros_like(x))


def _cast_bf16(w, lead=(), rows_out=None, cols_out=None):
    r, c = w.shape[-2:]
    rows_out = rows_out or r
    cols_out = cols_out or c
    tr = 8
    while tr * 2 * c * 4 <= CAST_BLOCK_BYTES and r % (tr * 2) == 0 and rows_out % (tr * 2) == 0:
        tr *= 2
    assert r % tr == 0 and rows_out % tr == 0 and c % LANES == 0
    n_in = r // tr
    squeeze = (None,) * len(lead)
    blocks = _nbytes((tr, c), F32) + _nbytes((tr, cols_out), BF16)
    return pl.pallas_call(
        functools.partial(_cast_kernel, rows_in=r, cols_in=c, tr=tr),
        grid=(rows_out // tr,),
        in_specs=[pl.BlockSpec(squeeze + (tr, c), lambda i: tuple(lead) + (jnp.minimum(i, n_in - 1), 0))],
        out_specs=pl.BlockSpec((tr, cols_out), lambda i: (i, 0)),
        out_shape=jax.ShapeDtypeStruct((rows_out, cols_out), BF16),
        compiler_params=_params(("parallel",), blocks),
        name="cast_bf16",
    )(w)


def _dot(a, b):
    return jnp.dot(a, b, preferred_element_type=F32)


def _swiglu_kernel(a_ref, wg_ref, wu_ref, o_ref):
    a = a_ref[...]
    g = _dot(a, wg_ref[...])
    u = _dot(a, wu_ref[...])
    o_ref[...] = (g * jax.nn.sigmoid(g) * u).astype(o_ref.dtype)


def _mm_swiglu(a, wg, wu, tm, tn):
    m, k = a.shape
    n = wg.shape[1]
    tm = _tile(m, tm)
    assert n % tn == 0
    blocks = _nbytes((tm, k), BF16) + 2 * _nbytes((k, tn), BF16) + _nbytes((tm, tn), BF16)
    return pl.pallas_call(
        _swiglu_kernel,
        grid=(m // tm, n // tn),
        in_specs=[pl.BlockSpec((tm, k), lambda i, j: (i, 0)),
                  pl.BlockSpec((k, tn), lambda i, j: (0, j)),
                  pl.BlockSpec((k, tn), lambda i, j: (0, j))],
        out_specs=pl.BlockSpec((tm, tn), lambda i, j: (i, j)),
        out_shape=jax.ShapeDtypeStruct((m, n), BF16),
        compiler_params=_params(("parallel", "parallel"), blocks,
                                scratch_bytes=3 * _nbytes((tm, tn), F32)),
        name="ffn_gate_up",
    )(a, wg, wu)


def _residual_kernel(a_ref, b_ref, r_ref, o_ref, *scratch, scale, nk):
    k = pl.program_id(2)
    part = _dot(a_ref[...], b_ref[...])
    if nk == 1:
        o_ref[...] = r_ref[...] + scale * part
        return
    (acc_ref,) = scratch

    @pl.when(k == 0)
    def _():
        acc_ref[...] = part

    @pl.when(jnp.logical_and(k > 0, k < nk - 1))
    def _():
        acc_ref[...] += part

    @pl.when(k == nk - 1)
    def _():
        o_ref[...] = r_ref[...] + scale * (acc_ref[...] + part)


def _mm_residual(a, b, res, scale, tm=1024, tn=1024, tk_max=4096):
    m, kdim = a.shape
    n = b.shape[1]
    tm, tn = _tile(m, tm), _tile(n, tn)
    nk = 1
    while kdim // nk > tk_max or kdim % nk or (kdim // nk) % LANES:
        nk += 1
    tk = kdim // nk
    blocks = (_nbytes((tm, tk), BF16) + _nbytes((tk, tn), BF16) + 2 * _nbytes((tm, tn), F32))
    return pl.pallas_call(
        functools.partial(_residual_kernel, scale=scale, nk=nk),
        grid=(m // tm, n // tn, nk),
        in_specs=[pl.BlockSpec((tm, tk), lambda i, j, k: (i, k)),
                  pl.BlockSpec((tk, tn), lambda i, j, k: (k, j)),
                  pl.BlockSpec((tm, tn), lambda i, j, k: (i, j))],
        out_specs=pl.BlockSpec((tm, tn), lambda i, j, k: (i, j)),
        out_shape=jax.ShapeDtypeStruct((m, n), F32),
        scratch_shapes=[pltpu.VMEM((tm, tn), F32)] if nk > 1 else [],
        compiler_params=_params(("parallel", "parallel", "arbitrary"), blocks,
                                scratch_bytes=2 * _nbytes((tm, tn), F32)),
        name="matmul_residual",
    )(a, b, res)


def _plain_kernel(a_ref, b_ref, o_ref):
    o_ref[...] = _dot(a_ref[...], b_ref[...]).astype(o_ref.dtype)


def _mm_plain(a, b, out_dtype, n=None, col0=0, tm=1024, tn=1024):
    m, k = a.shape
    n = n or b.shape[1]
    tm, tn = _tile(m, tm), _tile(n, tn)
    assert col0 % tn == 0
    joff = col0 // tn
    blocks = _nbytes((tm, k), BF16) + _nbytes((k, tn), BF16) + _nbytes((tm, tn), out_dtype)
    return pl.pallas_call(
        _plain_kernel,
        grid=(m // tm, n // tn),
        in_specs=[pl.BlockSpec((tm, k), lambda i, j: (i, 0)),
                  pl.BlockSpec((k, tn), lambda i, j: (0, j + joff))],
        out_specs=pl.BlockSpec((tm, tn), lambda i, j: (i, j)),
        out_shape=jax.ShapeDtypeStruct((m, n), out_dtype),
        compiler_params=_params(("parallel", "parallel"), blocks,
                                scratch_bytes=_nbytes((tm, tn), F32)),
        name="matmul_plain",
    )(a, b)


def _headnorm_kernel(a_ref, b_ref, g_ref, o_ref, *, head_dim):
    x = _dot(a_ref[...], b_ref[...])
    tn = x.shape[1]
    for h in range(tn // head_dim):
        sl = slice(h * head_dim, (h + 1) * head_dim)
        xh = x[:, sl]
        y = xh * lax.rsqrt(jnp.mean(xh * xh, axis=-1, keepdims=True) + RMS_EPS)
        o_ref[:, sl] = (y * g_ref[:, sl]).astype(o_ref.dtype)


def _mm_headnorm(a, b, gain_row, head_dim, col0=0, tm=1024, tn=1024):
    m, k = a.shape
    n = gain_row.shape[1]
    tm, tn = _tile(m, tm), _tile(n, tn)
    assert col0 % tn == 0
    joff = col0 // tn
    blocks = (_nbytes((tm, k), BF16) + _nbytes((k, tn), BF16) + _nbytes((tm, tn), BF16)
              + _nbytes((8, tn), F32))
    return pl.pallas_call(
        functools.partial(_headnorm_kernel, head_dim=head_dim),
        grid=(m // tm, n // tn),
        in_specs=[pl.BlockSpec((tm, k), lambda i, j: (i, 0)),
                  pl.BlockSpec((k, tn), lambda i, j: (0, j + joff)),
                  pl.BlockSpec((1, tn), lambda i, j: (0, j))],
        out_specs=pl.BlockSpec((tm, tn), lambda i, j: (i, j)),
        out_shape=jax.ShapeDtypeStruct((m, n), BF16),
        compiler_params=_params(("parallel", "parallel"), blocks,
                                scratch_bytes=_nbytes((tm, tn), F32)),
        name="matmul_headnorm",
    )(a, b, gain_row)


def _mla_down_kernel(a_ref, w_ref, gq_ref, gkv_ref, cq_ref, ckv_ref, kr_ref, *, q_lora, kv_lora):
    x = _dot(a_ref[...], w_ref[...])
    xq = x[:, :q_lora]
    cq = xq * lax.rsqrt(jnp.mean(xq * xq, axis=-1, keepdims=True) + RMS_EPS) * gq_ref[...]
    cq_ref[...] = cq.astype(cq_ref.dtype)
    xkv = x[:, q_lora:q_lora + kv_lora]
    ckv = xkv * lax.rsqrt(jnp.mean(xkv * xkv, axis=-1, keepdims=True) + RMS_EPS) * gkv_ref[...]
    ckv_ref[...] = ckv.astype(ckv_ref.dtype)
    kr_ref[...] = x[:, q_lora + kv_lora:]


def _mla_down(a, wcat, gq, gkv, q_lora, kv_lora, tm=512):
    m, k = a.shape
    n = wcat.shape[1]
    tm = _tile(m, tm)
    blocks = (_nbytes((tm, k), BF16) + _nbytes((k, n), BF16) + _nbytes((tm, q_lora), BF16)
              + _nbytes((tm, kv_lora), BF16) + _nbytes((tm, LANES), F32))
    return pl.pallas_call(
        functools.partial(_mla_down_kernel, q_lora=q_lora, kv_lora=kv_lora),
        grid=(m // tm,),
        in_specs=[pl.BlockSpec((tm, k), lambda i: (i, 0)),
                  pl.BlockSpec((k, n), lambda i: (0, 0)),
                  pl.BlockSpec((1, q_lora), lambda i: (0, 0)),
                  pl.BlockSpec((1, kv_lora), lambda i: (0, 0))],
        out_specs=[pl.BlockSpec((tm, q_lora), lambda i: (i, 0)),
                   pl.BlockSpec((tm, kv_lora), lambda i: (i, 0)),
                   pl.BlockSpec((tm, LANES), lambda i: (i, 0))],
        out_shape=[jax.ShapeDtypeStruct((m, q_lora), BF16),
                   jax.ShapeDtypeStruct((m, kv_lora), BF16),
                   jax.ShapeDtypeStruct((m, LANES), F32)],
        compiler_params=_params(("parallel",), blocks, scratch_bytes=_nbytes((tm, n), F32)),
        name="mla_down",
    )(a, wcat, gq, gkv)


def _rope_lane_tile(x, cos, sin_lo, sin_hi):
    half = QK_ROPE // 2
    return (x * cos + pltpu.roll(x, LANES - half, axis=1) * sin_lo
            + pltpu.roll(x, half, axis=1) * sin_hi)


def _mla_q_kernel(a_ref, w_ref, g_ref, cos_ref, slo_ref, shi_ref, o_ref):
    x = _dot(a_ref[...], w_ref[...])
    cos, slo, shi = cos_ref[...], slo_ref[...], shi_ref[...]
    for h in range(x.shape[1] // QK_PAD):
        xh = x[:, h * QK_PAD:(h + 1) * QK_PAD]
        ms = jnp.sum(xh * xh, axis=-1, keepdims=True) * (1.0 / QK_HEAD)
        y = xh * lax.rsqrt(ms + RMS_EPS) * g_ref[...]
        o_ref[:, h * QK_PAD:h * QK_PAD + LANES] = y[:, :LANES].astype(o_ref.dtype)
        o_ref[:, h * QK_PAD + LANES:(h + 1) * QK_PAD] = _rope_lane_tile(
            y[:, LANES:], cos, slo, shi).astype(o_ref.dtype)


def _mla_q(cq, wuq_p, gain_p, tabs, seq, tm=1024, tn=1024):
    m, k = cq.shape
    n = wuq_p.shape[1]
    tm, tn = _tile(seq, tm), _tile(n, tn)
    nsb = seq // tm
    tab_spec = pl.BlockSpec((tm, LANES), lambda i, j: (i % nsb, 0))
    blocks = (_nbytes((tm, k), BF16) + _nbytes((k, tn), BF16) + _nbytes((tm, tn), BF16)
              + 3 * _nbytes((tm, LANES), F32))
    return pl.pallas_call(
        _mla_q_kernel,
        grid=(m // tm, n // tn),
        in_specs=[pl.BlockSpec((tm, k), lambda i, j: (i, 0)),
                  pl.BlockSpec((k, tn), lambda i, j: (0, j)),
                  pl.BlockSpec((1, QK_PAD), lambda i, j: (0, 0)),
                  tab_spec, tab_spec, tab_spec],
        out_specs=pl.BlockSpec((tm, tn), lambda i, j: (i, j)),
        out_shape=jax.ShapeDtypeStruct((m, n), BF16),
        compiler_params=_params(("parallel", "parallel"), blocks,
                                scratch_bytes=_nbytes((tm, tn), F32)),
        name="mla_q_proj",
    )(cq, wuq_p, gain_p, *tabs)


FLASH_BQ = 2048
FLASH_BK = 1024


def _mla_kv_kernel(a_ref, w_ref, kr_ref, gn_ref, gr_ref, cos_ref, slo_ref, shi_ref, k_ref, vt_ref, *, bk):
    x = _dot(a_ref[...], w_ref[...])
    kr = kr_ref[...]
    ss_rope = jnp.sum(kr * kr, axis=-1, keepdims=True)
    kr_roped = _rope_lane_tile(kr * gr_ref[...], cos_ref[...], slo_ref[...], shi_ref[...])
    width = QK_NOPE + V_HEAD
    for h in range(x.shape[1] // width):
        kn = x[:, h * width:h * width + QK_NOPE]
        ms = (jnp.sum(kn * kn, axis=-1, keepdims=True) + ss_rope) * (1.0 / QK_HEAD)
        rs = lax.rsqrt(ms + RMS_EPS)
        k_ref[:, h * QK_PAD:h * QK_PAD + LANES] = (kn * rs * gn_ref[...]).astype(k_ref.dtype)
        k_ref[:, h * QK_PAD + LANES:(h + 1) * QK_PAD] = (kr_roped * rs).astype(k_ref.dtype)
        for c in range(x.shape[0] // bk):
            v = x[c * bk:(c + 1) * bk, h * width + QK_NOPE:(h + 1) * width]
            vt_ref[h, c] = v.T.astype(vt_ref.dtype)


def _mla_kv(ckv, wukv, k_rope, g_nope, g_rope, tabs, bsz, seq, bk, tm=1024, tn=1024):
    m, k = ckv.shape
    n = wukv.shape[1]
    width = QK_NOPE + V_HEAD
    tm, tn = _tile(seq, tm), _tile(n, tn)
    assert tm % bk == 0
    hpt = tn // width
    nsb = seq // tm
    tab_spec = pl.BlockSpec((tm, LANES), lambda i, j: (i % nsb, 0))
    blocks = (_nbytes((tm, k), BF16) + _nbytes((k, tn), BF16) + _nbytes((tm, hpt * QK_PAD), BF16)
              + _nbytes((tm, hpt * V_HEAD), BF16) + 4 * _nbytes((tm, LANES), F32))
    return pl.pallas_call(
        functools.partial(_mla_kv_kernel, bk=bk),
        grid=(m // tm, n // tn),
        in_specs=[pl.BlockSpec((tm, k), lambda i, j: (i, 0)),
                  pl.BlockSpec((k, tn), lambda i, j: (0, j)),
                  pl.BlockSpec((tm, LANES), lambda i, j: (i, 0)),
                  pl.BlockSpec((1, LANES), lambda i, j: (0, 0)),
                  pl.BlockSpec((1, LANES), lambda i, j: (0, 0)),
                  tab_spec, tab_spec, tab_spec],
        out_specs=[pl.BlockSpec((tm, hpt * QK_PAD), lambda i, j: (i, j)),
                   pl.BlockSpec((None, hpt, tm // bk, V_HEAD, bk), lambda i, j: (i // nsb, j, i % nsb, 0, 0))],
        out_shape=[jax.ShapeDtypeStruct((m, (n // width) * QK_PAD), BF16),
                   jax.ShapeDtypeStruct((bsz, n // width, seq // bk, V_HEAD, bk), BF16)],
        compiler_params=_params(("parallel", "parallel"), blocks,
                                scratch_bytes=2 * _nbytes((tm, tn), F32)),
        name="mla_kv_proj",
    )(ckv, wukv, k_rope, g_nope, g_rope, *tabs)


def _dot_nt(a, b):
    return lax.dot_general(a, b, (((1,), (1,)), ((), ())), preferred_element_type=F32)


def _flash_kernel(q_ref, k_ref, vt_ref, o_ref, m_ref, l_ref, acc_ref, s_ref, *, bq, bk):
    qi = pl.program_id(2)
    half = bk // 2
    m_ref[...] = jnp.full(m_ref.shape, MASK_VALUE, F32)
    l_ref[...] = jnp.zeros(l_ref.shape, F32)
    acc_ref[...] = jnp.zeros(acc_ref.shape, F32)

    def scores(j, slot, c0=0):
        kb = k_ref[pl.ds(pl.multiple_of(j * bk, bk), bk), :]
        s_ref[slot, :, c0:] = _dot_nt(kb, q_ref[c0:, :])

    def softmax_pv(j, slot, r0=0, nr=bk, c0=0, diagonal=False):
        st = s_ref[slot, r0:r0 + nr, c0:]
        if diagonal:
            row = lax.broadcasted_iota(jnp.int32, (nr, nr), 0)
            col = lax.broadcasted_iota(jnp.int32, (nr, nr), 1)
            tri = jnp.where(row <= col, st[:, :nr], MASK_VALUE)
            st = tri if st.shape[1] == nr else jnp.concatenate([tri, st[:, nr:]], axis=1)
        m_old = m_ref[:, c0:]
        m_new = jnp.maximum(m_old, jnp.max(st, axis=0, keepdims=True))
        p = jnp.exp2(st - m_new)
        alpha = jnp.exp2(m_old - m_new)
        l_ref[:, c0:] = alpha * l_ref[:, c0:] + jnp.sum(p, axis=0, keepdims=True)
        acc_ref[:, c0:] = alpha * acc_ref[:, c0:] + _dot(vt_ref[j, :, r0:r0 + nr], p.astype(BF16))
        m_ref[:, c0:] = m_new

    def body(i, carry):
        scores(2 * i + 1, 1)
        softmax_pv(2 * i, 0)
        scores(2 * i + 2, 0)
        softmax_pv(2 * i + 1, 1)
        return carry

    scores(0, 0)
    lax.fori_loop(0, qi, body, 0)
    scores(2 * qi + 1, 1, c0=bk)
    for g in range(4):
        softmax_pv(2 * qi + g // 2, g // 2, r0=(g % 2) * half, nr=half, c0=g * half, diagonal=True)
    o_ref[...] = (acc_ref[...] / l_ref[...]).T.astype(o_ref.dtype)


def _mla_flash(q, k, vt, bq, bk):
    b, s, _ = q.shape
    h = q.shape[2] // QK_PAD
    assert s % bq == 0 and bq == 2 * bk
    blocks = (_nbytes((bq, QK_PAD), BF16) + _nbytes((s, QK_PAD), BF16) + _nbytes((s, V_HEAD), BF16)
              + _nbytes((bq, V_HEAD), BF16))
    scratch = 2 * _nbytes((8, bq), F32) + _nbytes((V_HEAD, bq), F32) + 6 * _nbytes((bk, bq), F32)
    return pl.pallas_call(
        functools.partial(_flash_kernel, bq=bq, bk=bk),
        grid=(b, h, s // bq),
        in_specs=[pl.BlockSpec((None, bq, QK_PAD), lambda bi, hi, qi: (bi, qi, hi)),
                  pl.BlockSpec((None, s, QK_PAD), lambda bi, hi, qi: (bi, 0, hi)),
                  pl.BlockSpec((None, None, s // bk, V_HEAD, bk), lambda bi, hi, qi: (bi, hi, 0, 0, 0))],
        out_specs=pl.BlockSpec((None, bq, V_HEAD), lambda bi, hi, qi: (bi, qi, hi)),
        out_shape=jax.ShapeDtypeStruct((b, s, h * V_HEAD), BF16),
        scratch_shapes=[pltpu.VMEM((1, bq), F32), pltpu.VMEM((1, bq), F32),
                        pltpu.VMEM((V_HEAD, bq), F32), pltpu.VMEM((2, bk, bq), F32)],
        compiler_params=_params(("parallel", "parallel", "arbitrary"), blocks, scratch_bytes=scratch),
        name="mla_flash_attention",
    )(q, k, vt)


def _t5_causal_bucket(dist):
    max_exact = NUM_BUCKETS // 2
    n = jnp.maximum(dist, 0)
    nf = jnp.maximum(n, 1).astype(F32)
    large = max_exact + (jnp.log(nf / max_exact) / math.log(MAX_DISTANCE / max_exact)
                         * (NUM_BUCKETS - max_exact)).astype(jnp.int32)
    large = jnp.minimum(large, NUM_BUCKETS - 1)
    return jnp.where(n < max_exact, n, large)


def _band_bias_kernel(tab_ref, bucket_ref, o_ref, *, heads):
    g = pl.program_id(0)
    h = pl.program_id(1)
    bucket = bucket_ref[...]
    acc = jnp.full(bucket.shape, MASK_VALUE, F32)
    for b in range(NUM_BUCKETS):
        acc = jnp.where(bucket == b, tab_ref[b, g * heads + h], acc)
    o_ref[...] = acc


def _band_bias(rel_bias, buckets, heads):
    g = buckets.shape[0]
    blk = buckets.shape[1:]
    blocks = _nbytes(blk, jnp.int32) + _nbytes(blk, F32)
    return pl.pallas_call(
        functools.partial(_band_bias_kernel, heads=heads),
        grid=(g, heads),
        in_specs=[pl.BlockSpec(memory_space=pltpu.SMEM),
                  pl.BlockSpec((None,) + blk, lambda gi, hi: (gi, 0, 0))],
        out_specs=pl.BlockSpec((None, None) + blk, lambda gi, hi: (gi, hi, 0, 0)),
        out_shape=jax.ShapeDtypeStruct((g, heads) + blk, F32),
        compiler_params=_params(("parallel", "parallel"), blocks),
        name="dilated_band_bias",
    )(rel_bias, buckets)


DIL_HEAD_GROUP = 4


def _dilated_kernel(q_ref, kp_ref, kc_ref, vp_ref, vc_ref, bias_ref, o_ref, lse_ref, s_ref, *, scale, heads):
    n = pl.program_id(2)
    has_prev = n > 0
    d = DIL_HEAD_DIM
    hg = s_ref.shape[1]
    lane = lax.broadcasted_iota(jnp.int32, (Q_BLOCK, LANES), 1)
    col = lax.broadcasted_iota(jnp.int32, (Q_BLOCK, 2 * Q_BLOCK), 1)
    keep = jnp.logical_or(has_prev, col >= Q_BLOCK)

    def scores(grp, slot):
        for i in range(hg):
            sl = slice((grp * hg + i) * d, (grp * hg + i + 1) * d)
            kcat = jnp.concatenate([kp_ref[:, sl], kc_ref[:, sl]], axis=0)
            s_ref[slot, i] = _dot_nt(q_ref[:, sl], kcat)

    def softmax_pv(grp, slot, lse_tile):
        for i in range(hg):
            h = grp * hg + i
            sl = slice(h * d, (h + 1) * d)
            s = jnp.where(keep, s_ref[slot, i] * scale + bias_ref[h], MASK_VALUE)
            m = jnp.max(s, axis=-1, keepdims=True)
            p = jnp.exp(s - m)
            den = jnp.sum(p, axis=-1, keepdims=True)
            vcat = jnp.concatenate([vp_ref[:, sl], vc_ref[:, sl]], axis=0)
            o_ref[:, sl] = _dot((p / den).astype(BF16), vcat).astype(o_ref.dtype)
            lse_tile = jnp.where(lane == h, m + jnp.log(den), lse_tile)
        return lse_tile

    lse_tile = jnp.zeros((Q_BLOCK, LANES), F32)
    n_grp = heads // hg
    scores(0, 0)
    for grp in range(n_grp):
        if grp + 1 < n_grp:
            scores(grp + 1, (grp + 1) % 2)
        lse_tile = softmax_pv(grp, grp % 2, lse_tile)
    lse_ref[...] = lse_tile


def _dilated_group(q, k, v, bias_g):
    b, dilation, length, hd = q.shape
    hg = math.gcd(DIL_HEAD_GROUP, DIL_HEADS)
    nb = length // Q_BLOCK
    cur = lambda bi, r, n: (bi, r, n, 0)
    prev = lambda bi, r, n: (bi, r, jnp.maximum(n - 1, 0), 0)
    blk = (None, None, Q_BLOCK, hd)
    blocks = 6 * _nbytes((Q_BLOCK, hd), BF16) + _nbytes(bias_g.shape, F32) + _nbytes((Q_BLOCK, LANES), F32)
    return pl.pallas_call(
        functools.partial(_dilated_kernel, scale=DIL_HEAD_DIM ** -0.5, heads=DIL_HEADS),
        grid=(b, dilation, nb),
        in_specs=[pl.BlockSpec(blk, cur),
                  pl.BlockSpec(blk, prev), pl.BlockSpec(blk, cur),
                  pl.BlockSpec(blk, prev), pl.BlockSpec(blk, cur),
                  pl.BlockSpec(bias_g.shape, lambda bi, r, n: (0, 0, 0))],
        out_specs=[pl.BlockSpec(blk, cur),
                   pl.BlockSpec((None, None, Q_BLOCK, LANES), cur)],
        out_shape=[jax.ShapeDtypeStruct((b, dilation, length, hd), BF16),
                   jax.ShapeDtypeStruct((b, dilation, length, LANES), F32)],
        scratch_shapes=[pltpu.VMEM((2, hg, Q_BLOCK, 2 * Q_BLOCK), F32)],
        compiler_params=_params(("parallel", "parallel", "arbitrary"), blocks,
                                scratch_bytes=2 * hg * _nbytes((Q_BLOCK, 2 * Q_BLOCK), F32)),
        name=f"dilated_attention_d{dilation}",
    )(q, k, k, v, v, bias_g)


def _combine_kernel(*refs, heads, dilations):
    ng = len(dilations)
    n_perm = sum(dil > 1 for dil in dilations)
    o_refs, l_refs = refs[:ng], refs[ng:2 * ng]
    p_refs = refs[2 * ng:2 * ng + n_perm]
    o_ref, lse_scr, o_scr = refs[2 * ng + n_perm:]
    tm = o_ref.shape[0]
    ip = 0
    for g, dil in enumerate(dilations):
        for r in range(dil):
            rows = pl.ds(r, tm // dil, stride=dil) if dil > 1 else slice(None)
            lse_scr[g, rows, :] = l_refs[g][r]
        og = o_refs[g][...].reshape(tm, o_ref.shape[1])
        if dil > 1:
            o_scr[g] = _dot(p_refs[ip][...], og)
            ip += 1
        else:
            o_scr[g] = og.astype(F32)
    lse = [lse_scr[g] for g in range(ng)]
    m = functools.reduce(jnp.maximum, lse)
    e = [jnp.exp(l - m) for l in lse]
    tot = functools.reduce(jnp.add, e)
    w = [x / tot for x in e]
    d = DIL_HEAD_DIM
    for h in range(heads):
        sl = slice(h * d, (h + 1) * d)
        o = functools.reduce(jnp.add, [w[g][:, h:h + 1] * o_scr[g, :, sl] for g in range(ng)])
        o_ref[:, sl] = o.astype(o_ref.dtype)


def _combine_groups(outs, lses, seq, tm=256):
    dilations = tuple(o.shape[1] for o in outs)
    bsz, hd = outs[0].shape[0], outs[0].shape[3]
    ng = len(outs)
    tm = _tile(seq, tm)
    nsb = seq // tm
    m = bsz * seq
    idx = lambda i: (i // nsb, 0, i % nsb, 0)
    o_specs = [pl.BlockSpec((None, dil, tm // dil, hd), idx) for dil in dilations]
    l_specs = [pl.BlockSpec((None, dil, tm // dil, LANES), idx) for dil in dilations]
    for dil in dilations:
        assert tm % (16 * dil) == 0
    perms = [_gather_perm(tm, dil).T for dil in dilations if dil > 1]
    p_specs = [pl.BlockSpec((tm, tm), lambda i: (0, 0)) for _ in perms]
    blocks = ((ng + 1) * _nbytes((tm, hd), BF16) + ng * _nbytes((tm, LANES), F32)
              + len(perms) * _nbytes((tm, tm), BF16))
    scratch = ng * (_nbytes((tm, LANES), F32) + _nbytes((tm, hd), F32))
    return pl.pallas_call(
        functools.partial(_combine_kernel, heads=DIL_HEADS, dilations=dilations),
        grid=(m // tm,),
        in_specs=o_specs + l_specs + p_specs,
        out_specs=pl.BlockSpec((tm, hd), lambda i: (i, 0)),
        out_shape=jax.ShapeDtypeStruct((m, hd), BF16),
        scratch_shapes=[pltpu.VMEM((ng, tm, LANES), F32), pltpu.VMEM((ng, tm, hd), F32)],
        compiler_params=_params(("parallel",), blocks, scratch_bytes=2 * scratch),
        name="dilated_combine",
    )(*outs, *lses, *perms)


GATE_UP_TILE = 256
GATE_UP_ROWS = 2048
DOWN_ROWS = 512
DOWN_K_MAX = 6144


def _rope_tables(seq):
    half = QK_ROPE // 2
    inv = ROPE_THETA ** (-jnp.arange(half, dtype=F32) / half)
    ang = jnp.arange(seq).astype(F32)[:, None] * inv[None, :]
    cos, sin = jnp.cos(ang), jnp.sin(ang)
    zeros = jnp.zeros_like(cos)
    cos_t = jnp.concatenate([cos, cos, zeros, zeros], axis=1)
    sin_lo = jnp.concatenate([-sin, zeros, zeros, zeros], axis=1)
    sin_hi = jnp.concatenate([zeros, sin, zeros, zeros], axis=1)
    return cos_t, sin_lo, sin_hi


def _band_buckets():
    r = jnp.arange(Q_BLOCK)[:, None]
    c = jnp.arange(2 * Q_BLOCK)[None, :]
    steps = Q_BLOCK + r - c
    out = []
    for window, dilation in DIL_GROUPS:
        band = (steps >= 0) & (steps <= window // dilation)
        out.append(jnp.where(band, _t5_causal_bucket(steps * dilation), -1))
    return jnp.stack(out).astype(jnp.int32)


def _ffn(h, xn, ffn_wg, ffn_wu, ffn_wd, lead):
    f = ffn_wg.shape[-1]
    wg = _cast_bf16(ffn_wg, lead)
    wu = _cast_bf16(ffn_wu, lead)
    wd = _cast_bf16(ffn_wd, lead)
    gate_tile = GATE_UP_TILE if f % GATE_UP_TILE == 0 else LANES
    act = _mm_swiglu(xn, wg, wu, tm=GATE_UP_ROWS, tn=gate_tile)
    return _mm_residual(act, wd, h, 0.5, tm=DOWN_ROWS, tk_max=DOWN_K_MAX)


def kernel(x, ffn_norm, ffn_wg, ffn_wu, ffn_wd, attn_norm, mla_wdq, mla_q_lora_norm, mla_wuq, mla_wdkv, mla_kv_lora_norm, mla_wukv, mla_q_norm, mla_k_norm, mla_wo, kv_src_norm, w_kv_shared, k_norm_shared, rel_bias, dil_wq, dil_q_norm, dil_wo):
    bsz, seq, d = x.shape
    m = bsz * seq
    depth = ffn_norm.shape[0]
    n_a = depth // 2
    h = x.reshape(m, d)
    tabs = _rope_tables(seq)
    hd = DIL_HEADS * DIL_HEAD_DIM
    k_sh = v_sh = None
    bias_all = None

    for window, dilation in DIL_GROUPS:
        assert seq % (dilation * Q_BLOCK) == 0 and window // dilation <= Q_BLOCK
    bq, bk = min(FLASH_BQ, seq), min(FLASH_BK, seq)

    dils = tuple(dilation for _, dilation in DIL_GROUPS)
    gathered = lambda t, dil: t.reshape(bsz, dil, seq // dil, hd)

    for l in range(depth):
        if l == n_a:
            gains = jnp.stack([kv_src_norm] * N_GROUPS + [ffn_norm[l, 0]])
            *xs, xn = _rmsnorm(h, gains, dils + (1,), bsz, seq)
            wkv = _cast_bf16(w_kv_shared)
            k_sh, v_sh = [], []
            for g, dil in enumerate(dils):
                a_g = xs[g].reshape(m, d)
                gk = jnp.tile(k_norm_shared[g], DIL_HEADS)[None, :]
                k_sh.append(gathered(_mm_headnorm(a_g, wkv, gk, DIL_HEAD_DIM, col0=g * hd), dil))
                v_sh.append(gathered(_mm_plain(a_g, wkv, BF16, n=hd, col0=(N_GROUPS + g) * hd), dil))
            bias_all = _band_bias(rel_bias, _band_buckets(), DIL_HEADS)
        else:
            (xn,) = _rmsnorm(h, ffn_norm[l, 0][None, :], (1,), bsz, seq)
        h = _ffn(h, xn, ffn_wg, ffn_wu, ffn_wd, (l, 0))

        if l < n_a:
            (xn,) = _rmsnorm(h, attn_norm[l][None, :], (1,), bsz, seq)
            a = l
            q_lora = mla_wdq.shape[2]
            kv_lora = mla_wdkv.shape[2] - QK_ROPE
            wcat = jnp.concatenate(
                [mla_wdq[a], mla_wdkv[a], jnp.zeros((d, LANES - QK_ROPE), F32)], axis=1).astype(BF16)
            cq, ckv, k_rope = _mla_down(xn, wcat, mla_q_lora_norm[a][None, :],
                                        mla_kv_lora_norm[a][None, :], q_lora, kv_lora)
            wuq_p = jnp.pad(mla_wuq[a].reshape(q_lora, MLA_HEADS, QK_HEAD),
                            ((0, 0), (0, 0), (0, QK_PAD - QK_HEAD))).reshape(q_lora, -1).astype(BF16)
            q_scale = QK_HEAD ** -0.5 * math.log2(math.e)
            gq_p = jnp.pad(mla_q_norm[a] * q_scale, (0, QK_PAD - QK_HEAD))[None, :]
            q = _mla_q(cq, wuq_p, gq_p, tabs, seq)
            g_nope = mla_k_norm[a][:QK_NOPE][None, :]
            g_rope = jnp.pad(mla_k_norm[a][QK_NOPE:], (0, LANES - QK_ROPE))[None, :]
            k, vt = _mla_kv(ckv, _cast_bf16(mla_wukv, (a,)), k_rope, g_nope, g_rope, tabs, bsz, seq, bk)
            o = _mla_flash(q.reshape(bsz, seq, -1), k.reshape(bsz, seq, -1), vt, bq, bk)
            h = _mm_residual(o.reshape(m, -1), _cast_bf16(mla_wo, (a,)), h, 1.0)
        else:
            bl = l - n_a
            xq = _rmsnorm(h, jnp.stack([attn_norm[l]] * N_GROUPS), dils, bsz, seq)
            wq = _cast_bf16(dil_wq, (bl,))
            outs, lses = [], []
            for g, dil in enumerate(dils):
                gq = jnp.tile(dil_q_norm[bl][g], DIL_HEADS)[None, :]
                q_g = gathered(_mm_headnorm(xq[g].reshape(m, d), wq, gq, DIL_HEAD_DIM, col0=g * hd), dil)
                o_g, l_g = _dilated_group(q_g, k_sh[g], v_sh[g], bias_all[g])
                outs.append(o_g)
                lses.append(l_g)
            o = _combine_groups(outs, lses, seq)
            h = _mm_residual(o, _cast_bf16(dil_wo, (bl,)), h, 1.0)

        (xn,) = _rmsnorm(h, ffn_norm[l, 1][None, :], (1,), bsz, seq)
        h = _ffn(h, xn, ffn_wg, ffn_wu, ffn_wd, (l, 1))

    return h.reshape(bsz, seq, d)
```

```python
import functools
import math

import jax
import jax.numpy as jnp
import numpy as np
from jax import lax
from jax.experimental import pallas as pl
from jax.experimental.pallas import tpu as pltpu

F32 = jnp.float32
BF16 = jnp.bfloat16

RMS_EPS = 1e-6
MLA_HEADS = 32
QK_NOPE = 128
QK_ROPE = 64
QK_HEAD = QK_NOPE + QK_ROPE
V_HEAD = 128
ROPE_THETA = 10000.0
DIL_GROUPS = ((128, 1), (512, 4), (2048, 16))
N_GROUPS = 3
DIL_HEADS = 32
DIL_HEAD_DIM = 128
NUM_BUCKETS = 32
MAX_DISTANCE = 2048
Q_BLOCK = 128

LANES = 128
QK_PAD = 2 * LANES
V7X_VMEM_BYTES = 64 * 1024 * 1024
VMEM_HEADROOM_BYTES = 8 * 1024 * 1024
MASK_VALUE = -1e30


def _nbytes(shape, dtype):
    return int(np.prod(shape)) * jnp.dtype(dtype).itemsize


def _params(semantics, block_bytes, scratch_bytes=0):
    need = 2 * block_bytes + scratch_bytes + VMEM_HEADROOM_BYTES
    limit = min(max(need, 32 * 1024 * 1024), V7X_VMEM_BYTES - 4 * 1024 * 1024)
    return pltpu.CompilerParams(dimension_semantics=semantics, vmem_limit_bytes=int(limit))


def _tile(dim, pref):
    if dim <= pref:
        return dim
    t = pref
    while dim % t:
        t //= 2
    return t


def _gather_perm(tm, dil):
    n = tm // dil
    i = np.arange(tm)
    p = np.zeros((tm, tm), np.float32)
    p[i, (i % n) * dil + i // n] = 1.0
    return jnp.asarray(p, BF16)


def _rmsnorm_kernel(x_ref, g_ref, *refs, dilations):
    n_perm = sum(dil > 1 for dil in dilations)
    p_refs, o_refs = refs[:n_perm], refs[n_perm:]
    x = x_ref[...]
    y = x * lax.rsqrt(jnp.mean(x * x, axis=-1, keepdims=True) + RMS_EPS)
    tm = x.shape[0]
    ip = 0
    for i, (o_ref, dil) in enumerate(zip(o_refs, dilations)):
        yg = (y * g_ref[i:i + 1, :]).astype(o_ref.dtype)
        if dil == 1:
            o_ref[...] = yg
        else:
            yp = _dot(p_refs[ip][...], yg).astype(o_ref.dtype)
            ip += 1
            n = tm // dil
            for r in range(dil):
                o_ref[r] = yp[r * n:(r + 1) * n, :]


def _rmsnorm(x, gains, dilations, bsz, seq, tm=256):
    m, d = x.shape
    n = gains.shape[0]
    tm = _tile(seq, tm)
    nsb = seq // tm
    out_specs, out_shape, perms = [], [], []
    for dil in dilations:
        if dil == 1:
            out_specs.append(pl.BlockSpec((tm, d), lambda i: (i, 0)))
            out_shape.append(jax.ShapeDtypeStruct((m, d), BF16))
        else:
            assert tm % (16 * dil) == 0
            perms.append(_gather_perm(tm, dil))
            out_specs.append(pl.BlockSpec((None, dil, tm // dil, d), lambda i: (i // nsb, 0, i % nsb, 0)))
            out_shape.append(jax.ShapeDtypeStruct((bsz, dil, seq // dil, d), BF16))
    blocks = (_nbytes((tm, d), F32) + n * _nbytes((tm, d), BF16) + _nbytes((n, d), F32)
              + len(perms) * _nbytes((tm, tm), BF16))
    return pl.pallas_call(
        functools.partial(_rmsnorm_kernel, dilations=tuple(dilations)),
        grid=(m // tm,),
        in_specs=[pl.BlockSpec((tm, d), lambda i: (i, 0)),
                  pl.BlockSpec((n, d), lambda i: (0, 0))]
                 + [pl.BlockSpec((tm, tm), lambda i: (0, 0)) for _ in perms],
        out_specs=out_specs,
        out_shape=out_shape,
        compiler_params=_params(("parallel",), blocks, scratch_bytes=3 * _nbytes((tm, d), F32)),
        name="rmsnorm",
    )(x, gains, *perms)


CAST_BLOCK_BYTES = 12 * 1024 * 1024


def _cast_kernel(x_ref, o_ref, *, rows_in, cols_in, tr):
    x = x_ref[...].astype(o_ref.dtype)
    if o_ref.shape[1] > cols_in:
        o_ref[:, cols_in:] = jnp.zeros((tr, o_ref.shape[1] - cols_in), o_ref.dtype)
    row = pl.program_id(0) * tr + lax.broadcasted_iota(jnp.int32, x.shape, 0)
    o_ref[:, :cols_in] = jnp.where(row < rows_in, x, jnp.zeros_like(x))


def _cast_bf16(w, lead=(), rows_out=None, cols_out=None):
    r, c = w.shape[-2:]
    rows_out = rows_out or r
    cols_out = cols_out or c
    tr = 8
    while tr * 2 * c * 4 <= CAST_BLOCK_BYTES and r % (tr * 2) == 0 and rows_out % (tr * 2) == 0:
        tr *= 2
    assert r % tr == 0 and rows_out % tr == 0 and c % LANES == 0
    n_in = r // tr
    squeeze = (None,) * len(lead)
    blocks = _nbytes((tr, c), F32) + _nbytes((tr, cols_out), BF16)
    return pl.pallas_call(
        functools.partial(_cast_kernel, rows_in=r, cols_in=c, tr=tr),
        grid=(rows_out // tr,),
        in_specs=[pl.BlockSpec(squeeze + (tr, c), lambda i: tuple(lead) + (jnp.minimum(i, n_in - 1), 0))],
        out_specs=pl.BlockSpec((tr, cols_out), lambda i: (i, 0)),
        out_shape=jax.ShapeDtypeStruct((rows_out, cols_out), BF16),
        compiler_params=_params(("parallel",), blocks),
        name="cast_bf16",
    )(w)


def _dot(a, b):
    return jnp.dot(a, b, preferred_element_type=F32)


def _swiglu_kernel(a_ref, wg_ref, wu_ref, o_ref):
    a = a_ref[...]
    g = _dot(a, wg_ref[...])
    u = _dot(a, wu_ref[...])
    o_ref[...] = (g * jax.nn.sigmoid(g) * u).astype(o_ref.dtype)


def _mm_swiglu(a, wg, wu, tm, tn):
    m, k = a.shape
    n = wg.shape[1]
    tm = _tile(m, tm)
    assert n % tn == 0
    blocks = _nbytes((tm, k), BF16) + 2 * _nbytes((k, tn), BF16) + _nbytes((tm, tn), BF16)
    return pl.pallas_call(
        _swiglu_kernel,
        grid=(m // tm, n // tn),
        in_specs=[pl.BlockSpec((tm, k), lambda i, j: (i, 0)),
                  pl.BlockSpec((k, tn), lambda i, j: (0, j)),
                  pl.BlockSpec((k, tn), lambda i, j: (0, j))],
        out_specs=pl.BlockSpec((tm, tn), lambda i, j: (i, j)),
        out_shape=jax.ShapeDtypeStruct((m, n), BF16),
        compiler_params=_params(("parallel", "parallel"), blocks,
                                scratch_bytes=3 * _nbytes((tm, tn), F32)),
        name="ffn_gate_up",
    )(a, wg, wu)


def _residual_kernel(a_ref, b_ref, r_ref, o_ref, *, scale):
    o_ref[...] = r_ref[...] + scale * _dot(a_ref[...], b_ref[...])


def _mm_residual(a, b, res, scale, tm=1024, tn=1024, tk_max=4096):
    m, kdim = a.shape
    n = b.shape[1]
    tm, tn = _tile(m, tm), _tile(n, tn)
    nk = 1
    while kdim // nk > tk_max or kdim % nk or (kdim // nk) % LANES:
        nk += 1
    tk = kdim // nk
    blocks = (_nbytes((tm, tk), BF16) + _nbytes((tk, tn), BF16) + 2 * _nbytes((tm, tn), F32))
    out = res
    for kc in range(nk):
        out = pl.pallas_call(
            functools.partial(_residual_kernel, scale=scale),
            grid=(m // tm, n // tn),
            in_specs=[pl.BlockSpec((tm, tk), lambda i, j, kc=kc: (i, kc)),
                      pl.BlockSpec((tk, tn), lambda i, j, kc=kc: (kc, j)),
                      pl.BlockSpec((tm, tn), lambda i, j: (i, j))],
            out_specs=pl.BlockSpec((tm, tn), lambda i, j: (i, j)),
            out_shape=jax.ShapeDtypeStruct((m, n), F32),
            compiler_params=_params(("parallel", "parallel"), blocks,
                                    scratch_bytes=_nbytes((tm, tn), F32)),
            name="matmul_residual",
        )(a, b, out)
    return out


def _plain_kernel(a_ref, b_ref, o_ref):
    o_ref[...] = _dot(a_ref[...], b_ref[...]).astype(o_ref.dtype)


def _mm_plain(a, b, out_dtype, n=None, col0=0, tm=1024, tn=1024):
    m, k = a.shape
    n = n or b.shape[1]
    tm, tn = _tile(m, tm), _tile(n, tn)
    assert col0 % tn == 0
    joff = col0 // tn
    blocks = _nbytes((tm, k), BF16) + _nbytes((k, tn), BF16) + _nbytes((tm, tn), out_dtype)
    return pl.pallas_call(
        _plain_kernel,
        grid=(m // tm, n // tn),
        in_specs=[pl.BlockSpec((tm, k), lambda i, j: (i, 0)),
                  pl.BlockSpec((k, tn), lambda i, j: (0, j + joff))],
        out_specs=pl.BlockSpec((tm, tn), lambda i, j: (i, j)),
        out_shape=jax.ShapeDtypeStruct((m, n), out_dtype),
        compiler_params=_params(("parallel", "parallel"), blocks,
                                scratch_bytes=_nbytes((tm, tn), F32)),
        name="matmul_plain",
    )(a, b)


def _headnorm_kernel(a_ref, b_ref, g_ref, o_ref, *, head_dim):
    x = _dot(a_ref[...], b_ref[...])
    tn = x.shape[1]
    for h in range(tn // head_dim):
        sl = slice(h * head_dim, (h + 1) * head_dim)
        xh = x[:, sl]
        y = xh * lax.rsqrt(jnp.mean(xh * xh, axis=-1, keepdims=True) + RMS_EPS)
        o_ref[:, sl] = (y * g_ref[:, sl]).astype(o_ref.dtype)


def _mm_headnorm(a, b, gain_row, head_dim, col0=0, tm=1024, tn=1024):
    m, k = a.shape
    n = gain_row.shape[1]
    tm, tn = _tile(m, tm), _tile(n, tn)
    assert col0 % tn == 0
    joff = col0 // tn
    blocks = (_nbytes((tm, k), BF16) + _nbytes((k, tn), BF16) + _nbytes((tm, tn), BF16)
              + _nbytes((8, tn), F32))
    return pl.pallas_call(
        functools.partial(_headnorm_kernel, head_dim=head_dim),
        grid=(m // tm, n // tn),
        in_specs=[pl.BlockSpec((tm, k), lambda i, j: (i, 0)),
                  pl.BlockSpec((k, tn), lambda i, j: (0, j + joff)),
                  pl.BlockSpec((1, tn), lambda i, j: (0, j))],
        out_specs=pl.BlockSpec((tm, tn), lambda i, j: (i, j)),
        out_shape=jax.ShapeDtypeStruct((m, n), BF16),
        compiler_params=_params(("parallel", "parallel"), blocks,
                                scratch_bytes=_nbytes((tm, tn), F32)),
        name="matmul_headnorm",
    )(a, b, gain_row)


def _mla_down_kernel(a_ref, w_ref, gq_ref, gkv_ref, cq_ref, ckv_ref, kr_ref, *, q_lora, kv_lora):
    x = _dot(a_ref[...], w_ref[...])
    xq = x[:, :q_lora]
    cq = xq * lax.rsqrt(jnp.mean(xq * xq, axis=-1, keepdims=True) + RMS_EPS) * gq_ref[...]
    cq_ref[...] = cq.astype(cq_ref.dtype)
    xkv = x[:, q_lora:q_lora + kv_lora]
    ckv = xkv * lax.rsqrt(jnp.mean(xkv * xkv, axis=-1, keepdims=True) + RMS_EPS) * gkv_ref[...]
    ckv_ref[...] = ckv.astype(ckv_ref.dtype)
    kr_ref[...] = x[:, q_lora + kv_lora:]


def _mla_down(a, wcat, gq, gkv, q_lora, kv_lora, tm=512):
    m, k = a.shape
    n = wcat.shape[1]
    tm = _tile(m, tm)
    blocks = (_nbytes((tm, k), BF16) + _nbytes((k, n), BF16) + _nbytes((tm, q_lora), BF16)
              + _nbytes((tm, kv_lora), BF16) + _nbytes((tm, LANES), F32))
    return pl.pallas_call(
        functools.partial(_mla_down_kernel, q_lora=q_lora, kv_lora=kv_lora),
        grid=(m // tm,),
        in_specs=[pl.BlockSpec((tm, k), lambda i: (i, 0)),
                  pl.BlockSpec((k, n), lambda i: (0, 0)),
                  pl.BlockSpec((1, q_lora), lambda i: (0, 0)),
                  pl.BlockSpec((1, kv_lora), lambda i: (0, 0))],
        out_specs=[pl.BlockSpec((tm, q_lora), lambda i: (i, 0)),
                   pl.BlockSpec((tm, kv_lora), lambda i: (i, 0)),
                   pl.BlockSpec((tm, LANES), lambda i: (i, 0))],
        out_shape=[jax.ShapeDtypeStruct((m, q_lora), BF16),
                   jax.ShapeDtypeStruct((m, kv_lora), BF16),
                   jax.ShapeDtypeStruct((m, LANES), F32)],
        compiler_params=_params(("parallel",), blocks, scratch_bytes=_nbytes((tm, n), F32)),
        name="mla_down",
    )(a, wcat, gq, gkv)


def _rope_lane_tile(x, cos, sin_lo, sin_hi):
    half = QK_ROPE // 2
    return (x * cos + pltpu.roll(x, LANES - half, axis=1) * sin_lo
            + pltpu.roll(x, half, axis=1) * sin_hi)


def _mla_q_kernel(a_ref, w_ref, g_ref, cos_ref, slo_ref, shi_ref, o_ref):
    x = _dot(a_ref[...], w_ref[...])
    cos, slo, shi = cos_ref[...], slo_ref[...], shi_ref[...]
    for h in range(x.shape[1] // QK_PAD):
        xh = x[:, h * QK_PAD:(h + 1) * QK_PAD]
        ms = jnp.sum(xh * xh, axis=-1, keepdims=True) * (1.0 / QK_HEAD)
        y = xh * lax.rsqrt(ms + RMS_EPS) * g_ref[...]
        o_ref[:, h * QK_PAD:h * QK_PAD + LANES] = y[:, :LANES].astype(o_ref.dtype)
        o_ref[:, h * QK_PAD + LANES:(h + 1) * QK_PAD] = _rope_lane_tile(
            y[:, LANES:], cos, slo, shi).astype(o_ref.dtype)


def _mla_q(cq, wuq_p, gain_p, tabs, seq, tm=1024, tn=1024):
    m, k = cq.shape
    n = wuq_p.shape[1]
    tm, tn = _tile(seq, tm), _tile(n, tn)
    nsb = seq // tm
    tab_spec = pl.BlockSpec((tm, LANES), lambda i, j: (i % nsb, 0))
    blocks = (_nbytes((tm, k), BF16) + _nbytes((k, tn), BF16) + _nbytes((tm, tn), BF16)
              + 3 * _nbytes((tm, LANES), F32))
    return pl.pallas_call(
        _mla_q_kernel,
        grid=(m // tm, n // tn),
        in_specs=[pl.BlockSpec((tm, k), lambda i, j: (i, 0)),
                  pl.BlockSpec((k, tn), lambda i, j: (0, j)),
                  pl.BlockSpec((1, QK_PAD), lambda i, j: (0, 0)),
                  tab_spec, tab_spec, tab_spec],
        out_specs=pl.BlockSpec((tm, tn), lambda i, j: (i, j)),
        out_shape=jax.ShapeDtypeStruct((m, n), BF16),
        compiler_params=_params(("parallel", "parallel"), blocks,
                                scratch_bytes=_nbytes((tm, tn), F32)),
        name="mla_q_proj",
    )(cq, wuq_p, gain_p, *tabs)


FLASH_BQ = 2048
FLASH_BK = 1024


def _mla_kv_kernel(a_ref, w_ref, kr_ref, gn_ref, gr_ref, cos_ref, slo_ref, shi_ref, k_ref, vt_ref, *, bk):
    x = _dot(a_ref[...], w_ref[...])
    kr = kr_ref[...]
    ss_rope = jnp.sum(kr * kr, axis=-1, keepdims=True)
    kr_roped = _rope_lane_tile(kr * gr_ref[...], cos_ref[...], slo_ref[...], shi_ref[...])
    width = QK_NOPE + V_HEAD
    for h in range(x.shape[1] // width):
        kn = x[:, h * width:h * width + QK_NOPE]
        ms = (jnp.sum(kn * kn, axis=-1, keepdims=True) + ss_rope) * (1.0 / QK_HEAD)
        rs = lax.rsqrt(ms + RMS_EPS)
        k_ref[:, h * QK_PAD:h * QK_PAD + LANES] = (kn * rs * gn_ref[...]).astype(k_ref.dtype)
        k_ref[:, h * QK_PAD + LANES:(h + 1) * QK_PAD] = (kr_roped * rs).astype(k_ref.dtype)
        for c in range(x.shape[0] // bk):
            v = x[c * bk:(c + 1) * bk, h * width + QK_NOPE:(h + 1) * width]
            vt_ref[h, c] = v.T.astype(vt_ref.dtype)


def _mla_kv(ckv, wukv, k_rope, g_nope, g_rope, tabs, bsz, seq, bk, tm=1024, tn=1024):
    m, k = ckv.shape
    n = wukv.shape[1]
    width = QK_NOPE + V_HEAD
    tm, tn = _tile(seq, tm), _tile(n, tn)
    assert tm % bk == 0
    hpt = tn // width
    nsb = seq // tm
    tab_spec = pl.BlockSpec((tm, LANES), lambda i, j: (i % nsb, 0))
    blocks = (_nbytes((tm, k), BF16) + _nbytes((k, tn), BF16) + _nbytes((tm, hpt * QK_PAD), BF16)
              + _nbytes((tm, hpt * V_HEAD), BF16) + 4 * _nbytes((tm, LANES), F32))
    return pl.pallas_call(
        functools.partial(_mla_kv_kernel, bk=bk),
        grid=(m // tm, n // tn),
        in_specs=[pl.BlockSpec((tm, k), lambda i, j: (i, 0)),
                  pl.BlockSpec((k, tn), lambda i, j: (0, j)),
                  pl.BlockSpec((tm, LANES), lambda i, j: (i, 0)),
                  pl.BlockSpec((1, LANES), lambda i, j: (0, 0)),
                  pl.BlockSpec((1, LANES), lambda i, j: (0, 0)),
                  tab_spec, tab_spec, tab_spec],
        out_specs=[pl.BlockSpec((tm, hpt * QK_PAD), lambda i, j: (i, j)),
                   pl.BlockSpec((None, hpt, tm // bk, V_HEAD, bk), lambda i, j: (i // nsb, j, i % nsb, 0, 0))],
        out_shape=[jax.ShapeDtypeStruct((m, (n // width) * QK_PAD), BF16),
                   jax.ShapeDtypeStruct((bsz, n // width, seq // bk, V_HEAD, bk), BF16)],
        compiler_params=_params(("parallel", "parallel"), blocks,
                                scratch_bytes=2 * _nbytes((tm, tn), F32)),
        name="mla_kv_proj",
    )(ckv, wukv, k_rope, g_nope, g_rope, *tabs)


def _dot_nt(a, b):
    return lax.dot_general(a, b, (((1,), (1,)), ((), ())), preferred_element_type=F32)


def _flash_kernel(q_ref, k_ref, vt_ref, o_ref, m_ref, l_ref, acc_ref, s_ref, *, bq, bk):
    qi = pl.program_id(2)
    half = bk // 2
    m_ref[...] = jnp.full(m_ref.shape, MASK_VALUE, F32)
    l_ref[...] = jnp.zeros(l_ref.shape, F32)
    acc_ref[...] = jnp.zeros(acc_ref.shape, F32)

    def scores(j, slot, c0=0):
        kb = k_ref[pl.ds(pl.multiple_of(j * bk, bk), bk), :]
        s_ref[slot, :, c0:] = _dot_nt(kb, q_ref[c0:, :])

    def softmax_pv(j, slot, r0=0, nr=bk, c0=0, diagonal=False):
        st = s_ref[slot, r0:r0 + nr, c0:]
        if diagonal:
            row = lax.broadcasted_iota(jnp.int32, (nr, nr), 0)
            col = lax.broadcasted_iota(jnp.int32, (nr, nr), 1)
            tri = jnp.where(row <= col, st[:, :nr], MASK_VALUE)
            st = tri if st.shape[1] == nr else jnp.concatenate([tri, st[:, nr:]], axis=1)
        m_old = m_ref[:, c0:]
        m_new = jnp.maximum(m_old, jnp.max(st, axis=0, keepdims=True))
        p = jnp.exp2(st - m_new)
        alpha = jnp.exp2(m_old - m_new)
        l_ref[:, c0:] = alpha * l_ref[:, c0:] + jnp.sum(p, axis=0, keepdims=True)
        acc_ref[:, c0:] = alpha * acc_ref[:, c0:] + _dot(vt_ref[j, :, r0:r0 + nr], p.astype(BF16))
        m_ref[:, c0:] = m_new

    def body(i, carry):
        scores(2 * i + 1, 1)
        softmax_pv(2 * i, 0)
        scores(2 * i + 2, 0)
        softmax_pv(2 * i + 1, 1)
        return carry

    scores(0, 0)
    lax.fori_loop(0, qi, body, 0)
    scores(2 * qi + 1, 1, c0=bk)
    for g in range(4):
        softmax_pv(2 * qi + g // 2, g // 2, r0=(g % 2) * half, nr=half, c0=g * half, diagonal=True)
    o_ref[...] = (acc_ref[...] / l_ref[...]).T.astype(o_ref.dtype)


def _mla_flash(q, k, vt, bq, bk):
    b, s, _ = q.shape
    h = q.shape[2] // QK_PAD
    assert s % bq == 0 and bq == 2 * bk
    blocks = (_nbytes((bq, QK_PAD), BF16) + _nbytes((s, QK_PAD), BF16) + _nbytes((s, V_HEAD), BF16)
              + _nbytes((bq, V_HEAD), BF16))
    scratch = 2 * _nbytes((8, bq), F32) + _nbytes((V_HEAD, bq), F32) + 6 * _nbytes((bk, bq), F32)
    return pl.pallas_call(
        functools.partial(_flash_kernel, bq=bq, bk=bk),
        grid=(b, h, s // bq),
        in_specs=[pl.BlockSpec((None, bq, QK_PAD), lambda bi, hi, qi: (bi, qi, hi)),
                  pl.BlockSpec((None, s, QK_PAD), lambda bi, hi, qi: (bi, 0, hi)),
                  pl.BlockSpec((None, None, s // bk, V_HEAD, bk), lambda bi, hi, qi: (bi, hi, 0, 0, 0))],
        out_specs=pl.BlockSpec((None, bq, V_HEAD), lambda bi, hi, qi: (bi, qi, hi)),
        out_shape=jax.ShapeDtypeStruct((b, s, h * V_HEAD), BF16),
        scratch_shapes=[pltpu.VMEM((1, bq), F32), pltpu.VMEM((1, bq), F32),
                        pltpu.VMEM((V_HEAD, bq), F32), pltpu.VMEM((2, bk, bq), F32)],
        compiler_params=_params(("parallel", "parallel", "arbitrary"), blocks, scratch_bytes=scratch),
        name="mla_flash_attention",
    )(q, k, vt)


def _t5_causal_bucket(dist):
    max_exact = NUM_BUCKETS // 2
    n = jnp.maximum(dist, 0)
    nf = jnp.maximum(n, 1).astype(F32)
    large = max_exact + (jnp.log(nf / max_exact) / math.log(MAX_DISTANCE / max_exact)
                         * (NUM_BUCKETS - max_exact)).astype(jnp.int32)
    large = jnp.minimum(large, NUM_BUCKETS - 1)
    return jnp.where(n < max_exact, n, large)


def _band_bias_kernel(tab_ref, bucket_ref, o_ref, *, heads):
    g = pl.program_id(0)
    h = pl.program_id(1)
    bucket = bucket_ref[...]
    acc = jnp.full(bucket.shape, MASK_VALUE, F32)
    for b in range(NUM_BUCKETS):
        acc = jnp.where(bucket == b, tab_ref[b, g * heads + h], acc)
    o_ref[...] = acc


def _band_bias(rel_bias, buckets, heads):
    g = buckets.shape[0]
    blk = buckets.shape[1:]
    blocks = _nbytes(blk, jnp.int32) + _nbytes(blk, F32)
    return pl.pallas_call(
        functools.partial(_band_bias_kernel, heads=heads),
        grid=(g, heads),
        in_specs=[pl.BlockSpec(memory_space=pltpu.SMEM),
                  pl.BlockSpec((None,) + blk, lambda gi, hi: (gi, 0, 0))],
        out_specs=pl.BlockSpec((None, None) + blk, lambda gi, hi: (gi, hi, 0, 0)),
        out_shape=jax.ShapeDtypeStruct((g, heads) + blk, F32),
        compiler_params=_params(("parallel", "parallel"), blocks),
        name="dilated_band_bias",
    )(rel_bias, buckets)


DIL_HEAD_GROUP = 4


def _dilated_kernel(q_ref, kp_ref, kc_ref, vp_ref, vc_ref, bias_ref, o_ref, lse_ref, s_ref, *, scale, heads):
    n = pl.program_id(2)
    has_prev = n > 0
    d = DIL_HEAD_DIM
    hg = s_ref.shape[1]
    lane = lax.broadcasted_iota(jnp.int32, (Q_BLOCK, LANES), 1)
    col = lax.broadcasted_iota(jnp.int32, (Q_BLOCK, 2 * Q_BLOCK), 1)
    keep = jnp.logical_or(has_prev, col >= Q_BLOCK)

    def scores(grp, slot):
        for i in range(hg):
            sl = slice((grp * hg + i) * d, (grp * hg + i + 1) * d)
            kcat = jnp.concatenate([kp_ref[:, sl], kc_ref[:, sl]], axis=0)
            s_ref[slot, i] = _dot_nt(q_ref[:, sl], kcat)

    def softmax_pv(grp, slot, lse_tile):
        for i in range(hg):
            h = grp * hg + i
            sl = slice(h * d, (h + 1) * d)
            s = jnp.where(keep, s_ref[slot, i] * scale + bias_ref[h], MASK_VALUE)
            m = jnp.max(s, axis=-1, keepdims=True)
            p = jnp.exp(s - m)
            den = jnp.sum(p, axis=-1, keepdims=True)
            vcat = jnp.concatenate([vp_ref[:, sl], vc_ref[:, sl]], axis=0)
            o_ref[:, sl] = _dot((p / den).astype(BF16), vcat).astype(o_ref.dtype)
            lse_tile = jnp.where(lane == h, m + jnp.log(den), lse_tile)
        return lse_tile

    lse_tile = jnp.zeros((Q_BLOCK, LANES), F32)
    n_grp = heads // hg
    scores(0, 0)
    for grp in range(n_grp):
        if grp + 1 < n_grp:
            scores(grp + 1, (grp + 1) % 2)
        lse_tile = softmax_pv(grp, grp % 2, lse_tile)
    lse_ref[...] = lse_tile


def _dilated_group(q, k, v, bias_g):
    b, dilation, length, hd = q.shape
    hg = math.gcd(DIL_HEAD_GROUP, DIL_HEADS)
    nb = length // Q_BLOCK
    cur = lambda bi, r, n: (bi, r, n, 0)
    prev = lambda bi, r, n: (bi, r, jnp.maximum(n - 1, 0), 0)
    blk = (None, None, Q_BLOCK, hd)
    blocks = 6 * _nbytes((Q_BLOCK, hd), BF16) + _nbytes(bias_g.shape, F32) + _nbytes((Q_BLOCK, LANES), F32)
    return pl.pallas_call(
        functools.partial(_dilated_kernel, scale=DIL_HEAD_DIM ** -0.5, heads=DIL_HEADS),
        grid=(b, dilation, nb),
        in_specs=[pl.BlockSpec(blk, cur),
                  pl.BlockSpec(blk, prev), pl.BlockSpec(blk, cur),
                  pl.BlockSpec(blk, prev), pl.BlockSpec(blk, cur),
                  pl.BlockSpec(bias_g.shape, lambda bi, r, n: (0, 0, 0))],
        out_specs=[pl.BlockSpec(blk, cur),
                   pl.BlockSpec((None, None, Q_BLOCK, LANES), cur)],
        out_shape=[jax.ShapeDtypeStruct((b, dilation, length, hd), BF16),
                   jax.ShapeDtypeStruct((b, dilation, length, LANES), F32)],
        scratch_shapes=[pltpu.VMEM((2, hg, Q_BLOCK, 2 * Q_BLOCK), F32)],
        compiler_params=_params(("parallel", "parallel", "arbitrary"), blocks,
                                scratch_bytes=2 * hg * _nbytes((Q_BLOCK, 2 * Q_BLOCK), F32)),
        name=f"dilated_attention_d{dilation}",
    )(q, k, k, v, v, bias_g)


def _combine_kernel(*refs, heads, dilations):
    ng = len(dilations)
    n_perm = sum(dil > 1 for dil in dilations)
    o_refs, l_refs = refs[:ng], refs[ng:2 * ng]
    p_refs = refs[2 * ng:2 * ng + n_perm]
    o_ref, lse_scr, o_scr = refs[2 * ng + n_perm:]
    tm = o_ref.shape[0]
    ip = 0
    for g, dil in enumerate(dilations):
        for r in range(dil):
            rows = pl.ds(r, tm // dil, stride=dil) if dil > 1 else slice(None)
            lse_scr[g, rows, :] = l_refs[g][r]
        og = o_refs[g][...].reshape(tm, o_ref.shape[1])
        if dil > 1:
            o_scr[g] = _dot(p_refs[ip][...], og)
            ip += 1
        else:
            o_scr[g] = og.astype(F32)
    lse = [lse_scr[g] for g in range(ng)]
    m = functools.reduce(jnp.maximum, lse)
    e = [jnp.exp(l - m) for l in lse]
    tot = functools.reduce(jnp.add, e)
    w = [x / tot for x in e]
    d = DIL_HEAD_DIM
    for h in range(heads):
        sl = slice(h * d, (h + 1) * d)
        o = functools.reduce(jnp.add, [w[g][:, h:h + 1] * o_scr[g, :, sl] for g in range(ng)])
        o_ref[:, sl] = o.astype(o_ref.dtype)


def _combine_groups(outs, lses, seq, tm=256):
    dilations = tuple(o.shape[1] for o in outs)
    bsz, hd = outs[0].shape[0], outs[0].shape[3]
    ng = len(outs)
    tm = _tile(seq, tm)
    nsb = seq // tm
    m = bsz * seq
    idx = lambda i: (i // nsb, 0, i % nsb, 0)
    o_specs = [pl.BlockSpec((None, dil, tm // dil, hd), idx) for dil in dilations]
    l_specs = [pl.BlockSpec((None, dil, tm // dil, LANES), idx) for dil in dilations]
    for dil in dilations:
        assert tm % (16 * dil) == 0
    perms = [_gather_perm(tm, dil).T for dil in dilations if dil > 1]
    p_specs = [pl.BlockSpec((tm, tm), lambda i: (0, 0)) for _ in perms]
    blocks = ((ng + 1) * _nbytes((tm, hd), BF16) + ng * _nbytes((tm, LANES), F32)
              + len(perms) * _nbytes((tm, tm), BF16))
    scratch = ng * (_nbytes((tm, LANES), F32) + _nbytes((tm, hd), F32))
    return pl.pallas_call(
        functools.partial(_combine_kernel, heads=DIL_HEADS, dilations=dilations),
        grid=(m // tm,),
        in_specs=o_specs + l_specs + p_specs,
        out_specs=pl.BlockSpec((tm, hd), lambda i: (i, 0)),
        out_shape=jax.ShapeDtypeStruct((m, hd), BF16),
        scratch_shapes=[pltpu.VMEM((ng, tm, LANES), F32), pltpu.VMEM((ng, tm, hd), F32)],
        compiler_params=_params(("parallel",), blocks, scratch_bytes=2 * scratch),
        name="dilated_combine",
    )(*outs, *lses, *perms)


GATE_UP_TILE = 512
GATE_UP_ROWS = 1024
DOWN_TILE = 512
DOWN_K_MAX = 6144


def _rope_tables(seq):
    half = QK_ROPE // 2
    inv = ROPE_THETA ** (-jnp.arange(half, dtype=F32) / half)
    ang = jnp.arange(seq).astype(F32)[:, None] * inv[None, :]
    cos, sin = jnp.cos(ang), jnp.sin(ang)
    zeros = jnp.zeros_like(cos)
    cos_t = jnp.concatenate([cos, cos, zeros, zeros], axis=1)
    sin_lo = jnp.concatenate([-sin, zeros, zeros, zeros], axis=1)
    sin_hi = jnp.concatenate([zeros, sin, zeros, zeros], axis=1)
    return cos_t, sin_lo, sin_hi


def _band_buckets():
    r = jnp.arange(Q_BLOCK)[:, None]
    c = jnp.arange(2 * Q_BLOCK)[None, :]
    steps = Q_BLOCK + r - c
    out = []
    for window, dilation in DIL_GROUPS:
        band = (steps >= 0) & (steps <= window // dilation)
        out.append(jnp.where(band, _t5_causal_bucket(steps * dilation), -1))
    return jnp.stack(out).astype(jnp.int32)


def _ffn(h, xn, ffn_wg, ffn_wu, ffn_wd, lead):
    f = ffn_wg.shape[-1]
    fp = -(-f // GATE_UP_TILE) * GATE_UP_TILE
    wg = _cast_bf16(ffn_wg, lead, cols_out=fp)
    wu = _cast_bf16(ffn_wu, lead, cols_out=fp)
    wd = _cast_bf16(ffn_wd, lead, rows_out=fp)
    act = _mm_swiglu(xn, wg, wu, tm=GATE_UP_ROWS, tn=GATE_UP_TILE)
    return _mm_residual(act, wd, h, 0.5, tn=DOWN_TILE, tk_max=DOWN_K_MAX)


def kernel(x, ffn_norm, ffn_wg, ffn_wu, ffn_wd, attn_norm, mla_wdq, mla_q_lora_norm, mla_wuq, mla_wdkv, mla_kv_lora_norm, mla_wukv, mla_q_norm, mla_k_norm, mla_wo, kv_src_norm, w_kv_shared, k_norm_shared, rel_bias, dil_wq, dil_q_norm, dil_wo):
    bsz, seq, d = x.shape
    m = bsz * seq
    depth = ffn_norm.shape[0]
    n_a = depth // 2
    h = x.reshape(m, d)
    tabs = _rope_tables(seq)
    hd = DIL_HEADS * DIL_HEAD_DIM
    k_sh = v_sh = None
    bias_all = None

    for window, dilation in DIL_GROUPS:
        assert seq % (dilation * Q_BLOCK) == 0 and window // dilation <= Q_BLOCK
    bq, bk = min(FLASH_BQ, seq), min(FLASH_BK, seq)

    dils = tuple(dilation for _, dilation in DIL_GROUPS)
    gathered = lambda t, dil: t.reshape(bsz, dil, seq // dil, hd)

    for l in range(depth):
        if l == n_a:
            gains = jnp.stack([kv_src_norm] * N_GROUPS + [ffn_norm[l, 0]])
            *xs, xn = _rmsnorm(h, gains, dils + (1,), bsz, seq)
            wkv = _cast_bf16(w_kv_shared)
            k_sh, v_sh = [], []
            for g, dil in enumerate(dils):
                a_g = xs[g].reshape(m, d)
                gk = jnp.tile(k_norm_shared[g], DIL_HEADS)[None, :]
                k_sh.append(gathered(_mm_headnorm(a_g, wkv, gk, DIL_HEAD_DIM, col0=g * hd), dil))
                v_sh.append(gathered(_mm_plain(a_g, wkv, BF16, n=hd, col0=(N_GROUPS + g) * hd), dil))
            bias_all = _band_bias(rel_bias, _band_buckets(), DIL_HEADS)
        else:
            (xn,) = _rmsnorm(h, ffn_norm[l, 0][None, :], (1,), bsz, seq)
        h = _ffn(h, xn, ffn_wg, ffn_wu, ffn_wd, (l, 0))

        if l < n_a:
            (xn,) = _rmsnorm(h, attn_norm[l][None, :], (1,), bsz, seq)
            a = l
            q_lora = mla_wdq.shape[2]
            kv_lora = mla_wdkv.shape[2] - QK_ROPE
            wcat = jnp.concatenate(
                [mla_wdq[a], mla_wdkv[a], jnp.zeros((d, LANES - QK_ROPE), F32)], axis=1).astype(BF16)
            cq, ckv, k_rope = _mla_down(xn, wcat, mla_q_lora_norm[a][None, :],
                                        mla_kv_lora_norm[a][None, :], q_lora, kv_lora)
            wuq_p = jnp.pad(mla_wuq[a].reshape(q_lora, MLA_HEADS, QK_HEAD),
                            ((0, 0), (0, 0), (0, QK_PAD - QK_HEAD))).reshape(q_lora, -1).astype(BF16)
            q_scale = QK_HEAD ** -0.5 * math.log2(math.e)
            gq_p = jnp.pad(mla_q_norm[a] * q_scale, (0, QK_PAD - QK_HEAD))[None, :]
            q = _mla_q(cq, wuq_p, gq_p, tabs, seq)
            g_nope = mla_k_norm[a][:QK_NOPE][None, :]
            g_rope = jnp.pad(mla_k_norm[a][QK_NOPE:], (0, LANES - QK_ROPE))[None, :]
            k, vt = _mla_kv(ckv, _cast_bf16(mla_wukv, (a,)), k_rope, g_nope, g_rope, tabs, bsz, seq, bk)
            o = _mla_flash(q.reshape(bsz, seq, -1), k.reshape(bsz, seq, -1), vt, bq, bk)
            h = _mm_residual(o.reshape(m, -1), _cast_bf16(mla_wo, (a,)), h, 1.0)
        else:
            bl = l - n_a
            xq = _rmsnorm(h, jnp.stack([attn_norm[l]] * N_GROUPS), dils, bsz, seq)
            wq = _cast_bf16(dil_wq, (bl,))
            outs, lses = [], []
            for g, dil in enumerate(dils):
                gq = jnp.tile(dil_q_norm[bl][g], DIL_HEADS)[None, :]
                q_g = gathered(_mm_headnorm(xq[g].reshape(m, d), wq, gq, DIL_HEAD_DIM, col0=g * hd), dil)
                o_g, l_g = _dilated_group(q_g, k_sh[g], v_sh[g], bias_all[g])
                outs.append(o_g)
                lses.append(l_g)
            o = _combine_groups(outs, lses, seq)
            h = _mm_residual(o, _cast_bf16(dil_wo, (bl,)), h, 1.0)

        (xn,) = _rmsnorm(h, ffn_norm[l, 1][None, :], (1,), bsz, seq)
        h = _ffn(h, xn, ffn_wg, ffn_wu, ffn_wd, (l, 1))

    return h.reshape(bsz, seq, d)
```

```python
import functools
import math

import jax
import jax.numpy as jnp
import numpy as np
from jax import lax
from jax.experimental import pallas as pl
from jax.experimental.pallas import tpu as pltpu

F32 = jnp.float32
BF16 = jnp.bfloat16

RMS_EPS = 1e-6
MLA_HEADS = 32
QK_NOPE = 128
QK_ROPE = 64
QK_HEAD = QK_NOPE + QK_ROPE
V_HEAD = 128
ROPE_THETA = 10000.0
DIL_GROUPS = ((128, 1), (512, 4), (2048, 16))
N_GROUPS = 3
DIL_HEADS = 32
DIL_HEAD_DIM = 128
NUM_BUCKETS = 32
MAX_DISTANCE = 2048
Q_BLOCK = 128

LANES = 128
QK_PAD = 2 * LANES
V7X_VMEM_BYTES = 64 * 1024 * 1024
VMEM_HEADROOM_BYTES = 8 * 1024 * 1024
MASK_VALUE = -1e30


def _nbytes(shape, dtype):
    return int(np.prod(shape)) * jnp.dtype(dtype).itemsize


def _params(semantics, block_bytes, scratch_bytes=0):
    need = 2 * block_bytes + scratch_bytes + VMEM_HEADROOM_BYTES
    limit = min(max(need, 32 * 1024 * 1024), V7X_VMEM_BYTES - 4 * 1024 * 1024)
    return pltpu.CompilerParams(dimension_semantics=semantics, vmem_limit_bytes=int(limit))


def _tile(dim, pref):
    if dim <= pref:
        return dim
    t = pref
    while dim % t:
        t //= 2
    return t


def _gather_perm(tm, dil):
    n = tm // dil
    i = np.arange(tm)
    p = np.zeros((tm, tm), np.float32)
    p[i, (i % n) * dil + i // n] = 1.0
    return jnp.asarray(p, BF16)


def _rmsnorm_kernel(x_ref, g_ref, *refs, dilations):
    n_perm = sum(dil > 1 for dil in dilations)
    p_refs, o_refs = refs[:n_perm], refs[n_perm:]
    x = x_ref[...]
    y = x * lax.rsqrt(jnp.mean(x * x, axis=-1, keepdims=True) + RMS_EPS)
    tm = x.shape[0]
    ip = 0
    for i, (o_ref, dil) in enumerate(zip(o_refs, dilations)):
        yg = (y * g_ref[i:i + 1, :]).astype(o_ref.dtype)
        if dil == 1:
            o_ref[...] = yg
        else:
            yp = _dot(p_refs[ip][...], yg).astype(o_ref.dtype)
            ip += 1
            n = tm // dil
            for r in range(dil):
                o_ref[r] = yp[r * n:(r + 1) * n, :]


def _rmsnorm(x, gains, dilations, bsz, seq, tm=256):
    m, d = x.shape
    n = gains.shape[0]
    tm = _tile(seq, tm)
    nsb = seq // tm
    out_specs, out_shape, perms = [], [], []
    for dil in dilations:
        if dil == 1:
            out_specs.append(pl.BlockSpec((tm, d), lambda i: (i, 0)))
            out_shape.append(jax.ShapeDtypeStruct((m, d), BF16))
        else:
            assert tm % (16 * dil) == 0
            perms.append(_gather_perm(tm, dil))
            out_specs.append(pl.BlockSpec((None, dil, tm // dil, d), lambda i: (i // nsb, 0, i % nsb, 0)))
            out_shape.append(jax.ShapeDtypeStruct((bsz, dil, seq // dil, d), BF16))
    blocks = (_nbytes((tm, d), F32) + n * _nbytes((tm, d), BF16) + _nbytes((n, d), F32)
              + len(perms) * _nbytes((tm, tm), BF16))
    return pl.pallas_call(
        functools.partial(_rmsnorm_kernel, dilations=tuple(dilations)),
        grid=(m // tm,),
        in_specs=[pl.BlockSpec((tm, d), lambda i: (i, 0)),
                  pl.BlockSpec((n, d), lambda i: (0, 0))]
                 + [pl.BlockSpec((tm, tm), lambda i: (0, 0)) for _ in perms],
        out_specs=out_specs,
        out_shape=out_shape,
        compiler_params=_params(("parallel",), blocks, scratch_bytes=3 * _nbytes((tm, d), F32)),
        name="rmsnorm",
    )(x, gains, *perms)


CAST_BLOCK_BYTES = 12 * 1024 * 1024


def _cast_block(x_ref, o_ref, block, rows_in, pad_rows):
    tr, cols_in = x_ref.shape
    x = x_ref[...].astype(o_ref.dtype)
    if o_ref.shape[1] > cols_in:
        o_ref[:, cols_in:] = jnp.zeros((tr, o_ref.shape[1] - cols_in), o_ref.dtype)
    if pad_rows:
        row = block * tr + lax.broadcasted_iota(jnp.int32, x.shape, 0)
        x = jnp.where(row < rows_in, x, jnp.zeros_like(x))
    o_ref[:, :cols_in] = x


def _cast_kernel(x_ref, o_ref, *, rows_in, pad_rows):
    _cast_block(x_ref, o_ref, pl.program_id(0), rows_in, pad_rows)


def _cast_bf16(w, lead=(), rows_out=None, cols_out=None):
    r, c = w.shape[-2:]
    rows_out = rows_out or r
    cols_out = cols_out or c
    tr = 8
    while tr * 2 * c * 4 <= CAST_BLOCK_BYTES and r % (tr * 2) == 0 and rows_out % (tr * 2) == 0:
        tr *= 2
    assert r % tr == 0 and rows_out % tr == 0 and c % LANES == 0
    n_in = r // tr
    squeeze = (None,) * len(lead)
    blocks = _nbytes((tr, c), F32) + _nbytes((tr, cols_out), BF16)
    return pl.pallas_call(
        functools.partial(_cast_kernel, rows_in=r, pad_rows=rows_out > r),
        grid=(rows_out // tr,),
        in_specs=[pl.BlockSpec(squeeze + (tr, c), lambda i: tuple(lead) + (jnp.minimum(i, n_in - 1), 0))],
        out_specs=pl.BlockSpec((tr, cols_out), lambda i: (i, 0)),
        out_shape=jax.ShapeDtypeStruct((rows_out, cols_out), BF16),
        compiler_params=_params(("parallel",), blocks),
        name="cast_bf16",
    )(w)


class _RidingCast:
    def __init__(self, w, lead, rows_out, cols_out, n_steps):
        self.w, self.lead = w, tuple(lead)
        self.rows_in, self.cols_in = w.shape[-2:]
        self.rows_out, self.cols_out = rows_out or self.rows_in, cols_out or self.cols_in
        tr = 16
        while (self.rows_out % tr or self.rows_in % tr or self.rows_out // tr > n_steps):
            tr += 16
        self.tr = tr
        self.n_out, self.n_in = self.rows_out // tr, self.rows_in // tr

    def specs(self, step_of):
        squeeze = (None,) * len(self.lead)
        blk = lambda *g: jnp.minimum(step_of(*g), self.n_out - 1)
        return (pl.BlockSpec(squeeze + (self.tr, self.cols_in),
                             lambda *g: self.lead + (jnp.minimum(blk(*g), self.n_in - 1), 0)),
                pl.BlockSpec((self.tr, self.cols_out), lambda *g: (blk(*g), 0)),
                jax.ShapeDtypeStruct((self.rows_out, self.cols_out), BF16))

    def block_bytes(self):
        return _nbytes((self.tr, self.cols_in), F32) + _nbytes((self.tr, self.cols_out), BF16)

    def run(self, x_ref, o_ref, step):
        _cast_block(x_ref, o_ref, jnp.minimum(step, self.n_out - 1), self.rows_in,
                    self.rows_out > self.rows_in)


def _dot(a, b):
    return jnp.dot(a, b, preferred_element_type=F32)


def _swiglu_kernel(a_ref, wg_ref, wu_ref, *refs, rides):
    nr = len(rides)
    o_ref = refs[nr]
    step = pl.program_id(0) * pl.num_programs(1) + pl.program_id(1)
    for ride, src_ref, dst_ref in zip(rides, refs[:nr], refs[nr + 1:]):
        ride.run(src_ref, dst_ref, step)
    a = a_ref[...]
    g = _dot(a, wg_ref[...])
    u = _dot(a, wu_ref[...])
    o_ref[...] = (g * jax.nn.sigmoid(g) * u).astype(o_ref.dtype)


def _mm_swiglu(a, wg, wu, tm, tn, ride_specs=()):
    m, k = a.shape
    n = wg.shape[1]
    tm = _tile(m, tm)
    assert n % tn == 0
    nj = n // tn
    rides = [_RidingCast(*spec, n_steps=(m // tm) * nj) for spec in ride_specs]
    ride_io = [ride.specs(lambda i, j: i * nj + j) for ride in rides]
    blocks = (_nbytes((tm, k), BF16) + 2 * _nbytes((k, tn), BF16) + _nbytes((tm, tn), BF16)
              + sum(ride.block_bytes() for ride in rides))
    out = pl.pallas_call(
        functools.partial(_swiglu_kernel, rides=rides),
        grid=(m // tm, nj),
        in_specs=[pl.BlockSpec((tm, k), lambda i, j: (i, 0)),
                  pl.BlockSpec((k, tn), lambda i, j: (0, j)),
                  pl.BlockSpec((k, tn), lambda i, j: (0, j))] + [io[0] for io in ride_io],
        out_specs=[pl.BlockSpec((tm, tn), lambda i, j: (i, j))] + [io[1] for io in ride_io],
        out_shape=[jax.ShapeDtypeStruct((m, n), BF16)] + [io[2] for io in ride_io],
        compiler_params=_params(("arbitrary", "arbitrary") if rides else ("parallel", "parallel"), blocks,
                                scratch_bytes=3 * _nbytes((tm, tn), F32)),
        name="ffn_gate_up",
    )(a, wg, wu, *[ride.w for ride in rides])
    return out[0], out[1:]


def _residual_kernel(a_ref, b_ref, r_ref, o_ref, *, scale):
    o_ref[...] = r_ref[...] + scale * _dot(a_ref[...], b_ref[...])


def _mm_residual(a, b, res, scale, tm=1024, tn=1024, tk_max=4096):
    m, kdim = a.shape
    n = b.shape[1]
    tm, tn = _tile(m, tm), _tile(n, tn)
    nk = 1
    while kdim // nk > tk_max or kdim % nk or (kdim // nk) % LANES:
        nk += 1
    tk = kdim // nk
    blocks = (_nbytes((tm, tk), BF16) + _nbytes((tk, tn), BF16) + 2 * _nbytes((tm, tn), F32))
    out = res
    for kc in range(nk):
        out = pl.pallas_call(
            functools.partial(_residual_kernel, scale=scale),
            grid=(m // tm, n // tn),
            in_specs=[pl.BlockSpec((tm, tk), lambda i, j, kc=kc: (i, kc)),
                      pl.BlockSpec((tk, tn), lambda i, j, kc=kc: (kc, j)),
                      pl.BlockSpec((tm, tn), lambda i, j: (i, j))],
            out_specs=pl.BlockSpec((tm, tn), lambda i, j: (i, j)),
            out_shape=jax.ShapeDtypeStruct((m, n), F32),
            compiler_params=_params(("parallel", "parallel"), blocks,
                                    scratch_bytes=_nbytes((tm, tn), F32)),
            name="matmul_residual",
        )(a, b, out)
    return out


def _plain_kernel(a_ref, b_ref, o_ref):
    o_ref[...] = _dot(a_ref[...], b_ref[...]).astype(o_ref.dtype)


def _mm_plain(a, b, out_dtype, n=None, col0=0, tm=1024, tn=1024):
    m, k = a.shape
    n = n or b.shape[1]
    tm, tn = _tile(m, tm), _tile(n, tn)
    assert col0 % tn == 0
    joff = col0 // tn
    blocks = _nbytes((tm, k), BF16) + _nbytes((k, tn), BF16) + _nbytes((tm, tn), out_dtype)
    return pl.pallas_call(
        _plain_kernel,
        grid=(m // tm, n // tn),
        in_specs=[pl.BlockSpec((tm, k), lambda i, j: (i, 0)),
                  pl.BlockSpec((k, tn), lambda i, j: (0, j + joff))],
        out_specs=pl.BlockSpec((tm, tn), lambda i, j: (i, j)),
        out_shape=jax.ShapeDtypeStruct((m, n), out_dtype),
        compiler_params=_params(("parallel", "parallel"), blocks,
                                scratch_bytes=_nbytes((tm, tn), F32)),
        name="matmul_plain",
    )(a, b)


def _headnorm_kernel(a_ref, b_ref, g_ref, o_ref, *, head_dim):
    x = _dot(a_ref[...], b_ref[...])
    tn = x.shape[1]
    for h in range(tn // head_dim):
        sl = slice(h * head_dim, (h + 1) * head_dim)
        xh = x[:, sl]
        y = xh * lax.rsqrt(jnp.mean(xh * xh, axis=-1, keepdims=True) + RMS_EPS)
        o_ref[:, sl] = (y * g_ref[:, sl]).astype(o_ref.dtype)


def _mm_headnorm(a, b, gain_row, head_dim, col0=0, tm=1024, tn=1024):
    m, k = a.shape
    n = gain_row.shape[1]
    tm, tn = _tile(m, tm), _tile(n, tn)
    assert col0 % tn == 0
    joff = col0 // tn
    blocks = (_nbytes((tm, k), BF16) + _nbytes((k, tn), BF16) + _nbytes((tm, tn), BF16)
              + _nbytes((8, tn), F32))
    return pl.pallas_call(
        functools.partial(_headnorm_kernel, head_dim=head_dim),
        grid=(m // tm, n // tn),
        in_specs=[pl.BlockSpec((tm, k), lambda i, j: (i, 0)),
                  pl.BlockSpec((k, tn), lambda i, j: (0, j + joff)),
                  pl.BlockSpec((1, tn), lambda i, j: (0, j))],
        out_specs=pl.BlockSpec((tm, tn), lambda i, j: (i, j)),
        out_shape=jax.ShapeDtypeStruct((m, n), BF16),
        compiler_params=_params(("parallel", "parallel"), blocks,
                                scratch_bytes=_nbytes((tm, tn), F32)),
        name="matmul_headnorm",
    )(a, b, gain_row)


def _mla_down_kernel(a_ref, w_ref, gq_ref, gkv_ref, cq_ref, ckv_ref, kr_ref, *, q_lora, kv_lora):
    x = _dot(a_ref[...], w_ref[...])
    xq = x[:, :q_lora]
    cq = xq * lax.rsqrt(jnp.mean(xq * xq, axis=-1, keepdims=True) + RMS_EPS) * gq_ref[...]
    cq_ref[...] = cq.astype(cq_ref.dtype)
    xkv = x[:, q_lora:q_lora + kv_lora]
    ckv = xkv * lax.rsqrt(jnp.mean(xkv * xkv, axis=-1, keepdims=True) + RMS_EPS) * gkv_ref[...]
    ckv_ref[...] = ckv.astype(ckv_ref.dtype)
    kr_ref[...] = x[:, q_lora + kv_lora:]


def _mla_down(a, wcat, gq, gkv, q_lora, kv_lora, tm=512):
    m, k = a.shape
    n = wcat.shape[1]
    tm = _tile(m, tm)
    blocks = (_nbytes((tm, k), BF16) + _nbytes((k, n), BF16) + _nbytes((tm, q_lora), BF16)
              + _nbytes((tm, kv_lora), BF16) + _nbytes((tm, LANES), F32))
    return pl.pallas_call(
        functools.partial(_mla_down_kernel, q_lora=q_lora, kv_lora=kv_lora),
        grid=(m // tm,),
        in_specs=[pl.BlockSpec((tm, k), lambda i: (i, 0)),
                  pl.BlockSpec((k, n), lambda i: (0, 0)),
                  pl.BlockSpec((1, q_lora), lambda i: (0, 0)),
                  pl.BlockSpec((1, kv_lora), lambda i: (0, 0))],
        out_specs=[pl.BlockSpec((tm, q_lora), lambda i: (i, 0)),
                   pl.BlockSpec((tm, kv_lora), lambda i: (i, 0)),
                   pl.BlockSpec((tm, LANES), lambda i: (i, 0))],
        out_shape=[jax.ShapeDtypeStruct((m, q_lora), BF16),
                   jax.ShapeDtypeStruct((m, kv_lora), BF16),
                   jax.ShapeDtypeStruct((m, LANES), F32)],
        compiler_params=_params(("parallel",), blocks, scratch_bytes=_nbytes((tm, n), F32)),
        name="mla_down",
    )(a, wcat, gq, gkv)


def _rope_lane_tile(x, cos, sin_lo, sin_hi):
    half = QK_ROPE // 2
    return (x * cos + pltpu.roll(x, LANES - half, axis=1) * sin_lo
            + pltpu.roll(x, half, axis=1) * sin_hi)


def _mla_q_kernel(a_ref, w_ref, g_ref, cos_ref, slo_ref, shi_ref, o_ref):
    x = _dot(a_ref[...], w_ref[...])
    cos, slo, shi = cos_ref[...], slo_ref[...], shi_ref[...]
    for h in range(x.shape[1] // QK_PAD):
        xh = x[:, h * QK_PAD:(h + 1) * QK_PAD]
        ms = jnp.sum(xh * xh, axis=-1, keepdims=True) * (1.0 / QK_HEAD)
        y = xh * lax.rsqrt(ms + RMS_EPS) * g_ref[...]
        o_ref[:, h * QK_PAD:h * QK_PAD + LANES] = y[:, :LANES].astype(o_ref.dtype)
        o_ref[:, h * QK_PAD + LANES:(h + 1) * QK_PAD] = _rope_lane_tile(
            y[:, LANES:], cos, slo, shi).astype(o_ref.dtype)


def _mla_q(cq, wuq_p, gain_p, tabs, seq, tm=1024, tn=1024):
    m, k = cq.shape
    n = wuq_p.shape[1]
    tm, tn = _tile(seq, tm), _tile(n, tn)
    nsb = seq // tm
    tab_spec = pl.BlockSpec((tm, LANES), lambda i, j: (i % nsb, 0))
    blocks = (_nbytes((tm, k), BF16) + _nbytes((k, tn), BF16) + _nbytes((tm, tn), BF16)
              + 3 * _nbytes((tm, LANES), F32))
    return pl.pallas_call(
        _mla_q_kernel,
        grid=(m // tm, n // tn),
        in_specs=[pl.BlockSpec((tm, k), lambda i, j: (i, 0)),
                  pl.BlockSpec((k, tn), lambda i, j: (0, j)),
                  pl.BlockSpec((1, QK_PAD), lambda i, j: (0, 0)),
                  tab_spec, tab_spec, tab_spec],
        out_specs=pl.BlockSpec((tm, tn), lambda i, j: (i, j)),
        out_shape=jax.ShapeDtypeStruct((m, n), BF16),
        compiler_params=_params(("parallel", "parallel"), blocks,
                                scratch_bytes=_nbytes((tm, tn), F32)),
        name="mla_q_proj",
    )(cq, wuq_p, gain_p, *tabs)


FLASH_BQ = 2048
FLASH_BK = 1024


def _mla_kv_kernel(a_ref, w_ref, kr_ref, gn_ref, gr_ref, cos_ref, slo_ref, shi_ref, k_ref, vt_ref, *, bk):
    x = _dot(a_ref[...], w_ref[...])
    kr = kr_ref[...]
    ss_rope = jnp.sum(kr * kr, axis=-1, keepdims=True)
    kr_roped = _rope_lane_tile(kr * gr_ref[...], cos_ref[...], slo_ref[...], shi_ref[...])
    width = QK_NOPE + V_HEAD
    for h in range(x.shape[1] // width):
        kn = x[:, h * width:h * width + QK_NOPE]
        ms = (jnp.sum(kn * kn, axis=-1, keepdims=True) + ss_rope) * (1.0 / QK_HEAD)
        rs = lax.rsqrt(ms + RMS_EPS)
        k_ref[:, h * QK_PAD:h * QK_PAD + LANES] = (kn * rs * gn_ref[...]).astype(k_ref.dtype)
        k_ref[:, h * QK_PAD + LANES:(h + 1) * QK_PAD] = (kr_roped * rs).astype(k_ref.dtype)
        for c in range(x.shape[0] // bk):
            v = x[c * bk:(c + 1) * bk, h * width + QK_NOPE:(h + 1) * width]
            vt_ref[h, c] = v.T.astype(vt_ref.dtype)


def _mla_kv(ckv, wukv, k_rope, g_nope, g_rope, tabs, bsz, seq, bk, tm=1024, tn=1024):
    m, k = ckv.shape
    n = wukv.shape[1]
    width = QK_NOPE + V_HEAD
    tm, tn = _tile(seq, tm), _tile(n, tn)
    assert tm % bk == 0
    hpt = tn // width
    nsb = seq // tm
    tab_spec = pl.BlockSpec((tm, LANES), lambda i, j: (i % nsb, 0))
    blocks = (_nbytes((tm, k), BF16) + _nbytes((k, tn), BF16) + _nbytes((tm, hpt * QK_PAD), BF16)
              + _nbytes((tm, hpt * V_HEAD), BF16) + 4 * _nbytes((tm, LANES), F32))
    return pl.pallas_call(
        functools.partial(_mla_kv_kernel, bk=bk),
        grid=(m // tm, n // tn),
        in_specs=[pl.BlockSpec((tm, k), lambda i, j: (i, 0)),
                  pl.BlockSpec((k, tn), lambda i, j: (0, j)),
                  pl.BlockSpec((tm, LANES), lambda i, j: (i, 0)),
                  pl.BlockSpec((1, LANES), lambda i, j: (0, 0)),
                  pl.BlockSpec((1, LANES), lambda i, j: (0, 0)),
                  tab_spec, tab_spec, tab_spec],
        out_specs=[pl.BlockSpec((tm, hpt * QK_PAD), lambda i, j: (i, j)),
                   pl.BlockSpec((None, hpt, tm // bk, V_HEAD, bk), lambda i, j: (i // nsb, j, i % nsb, 0, 0))],
        out_shape=[jax.ShapeDtypeStruct((m, (n // width) * QK_PAD), BF16),
                   jax.ShapeDtypeStruct((bsz, n // width, seq // bk, V_HEAD, bk), BF16)],
        compiler_params=_params(("parallel", "parallel"), blocks,
                                scratch_bytes=2 * _nbytes((tm, tn), F32)),
        name="mla_kv_proj",
    )(ckv, wukv, k_rope, g_nope, g_rope, *tabs)


def _dot_nt(a, b):
    return lax.dot_general(a, b, (((1,), (1,)), ((), ())), preferred_element_type=F32)


def _flash_kernel(q_ref, k_ref, vt_ref, o_ref, m_ref, l_ref, acc_ref, s_ref, *, bq, bk):
    qi = pl.program_id(2)
    half = bk // 2
    m_ref[...] = jnp.full(m_ref.shape, MASK_VALUE, F32)
    l_ref[...] = jnp.zeros(l_ref.shape, F32)
    acc_ref[...] = jnp.zeros(acc_ref.shape, F32)

    def scores(j, slot, c0=0):
        kb = k_ref[pl.ds(pl.multiple_of(j * bk, bk), bk), :]
        s_ref[slot, :, c0:] = _dot_nt(kb, q_ref[c0:, :])

    def softmax_pv(j, slot, r0=0, nr=bk, c0=0, diagonal=False):
        st = s_ref[slot, r0:r0 + nr, c0:]
        if diagonal:
            row = lax.broadcasted_iota(jnp.int32, (nr, nr), 0)
            col = lax.broadcasted_iota(jnp.int32, (nr, nr), 1)
            tri = jnp.where(row <= col, st[:, :nr], MASK_VALUE)
            st = tri if st.shape[1] == nr else jnp.concatenate([tri, st[:, nr:]], axis=1)
        m_old = m_ref[:, c0:]
        m_new = jnp.maximum(m_old, jnp.max(st, axis=0, keepdims=True))
        p = jnp.exp2(st - m_new)
        alpha = jnp.exp2(m_old - m_new)
        l_ref[:, c0:] = alpha * l_ref[:, c0:] + jnp.sum(p, axis=0, keepdims=True)
        acc_ref[:, c0:] = alpha * acc_ref[:, c0:] + _dot(vt_ref[j, :, r0:r0 + nr], p.astype(BF16))
        m_ref[:, c0:] = m_new

    def body(i, carry):
        scores(2 * i + 1, 1)
        softmax_pv(2 * i, 0)
        scores(2 * i + 2, 0)
        softmax_pv(2 * i + 1, 1)
        return carry

    scores(0, 0)
    lax.fori_loop(0, qi, body, 0)
    scores(2 * qi + 1, 1, c0=bk)
    for g in range(4):
        softmax_pv(2 * qi + g // 2, g // 2, r0=(g % 2) * half, nr=half, c0=g * half, diagonal=True)
    o_ref[...] = (acc_ref[...] / l_ref[...]).T.astype(o_ref.dtype)


def _mla_flash(q, k, vt, bq, bk):
    b, s, _ = q.shape
    h = q.shape[2] // QK_PAD
    assert s % bq == 0 and bq == 2 * bk
    blocks = (_nbytes((bq, QK_PAD), BF16) + _nbytes((s, QK_PAD), BF16) + _nbytes((s, V_HEAD), BF16)
              + _nbytes((bq, V_HEAD), BF16))
    scratch = 2 * _nbytes((8, bq), F32) + _nbytes((V_HEAD, bq), F32) + 6 * _nbytes((bk, bq), F32)
    return pl.pallas_call(
        functools.partial(_flash_kernel, bq=bq, bk=bk),
        grid=(b, h, s // bq),
        in_specs=[pl.BlockSpec((None, bq, QK_PAD), lambda bi, hi, qi: (bi, qi, hi)),
                  pl.BlockSpec((None, s, QK_PAD), lambda bi, hi, qi: (bi, 0, hi)),
                  pl.BlockSpec((None, None, s // bk, V_HEAD, bk), lambda bi, hi, qi: (bi, hi, 0, 0, 0))],
        out_specs=pl.BlockSpec((None, bq, V_HEAD), lambda bi, hi, qi: (bi, qi, hi)),
        out_shape=jax.ShapeDtypeStruct((b, s, h * V_HEAD), BF16),
        scratch_shapes=[pltpu.VMEM((1, bq), F32), pltpu.VMEM((1, bq), F32),
                        pltpu.VMEM((V_HEAD, bq), F32), pltpu.VMEM((2, bk, bq), F32)],
        compiler_params=_params(("parallel", "parallel", "arbitrary"), blocks, scratch_bytes=scratch),
        name="mla_flash_attention",
    )(q, k, vt)


def _t5_causal_bucket(dist):
    max_exact = NUM_BUCKETS // 2
    n = jnp.maximum(dist, 0)
    nf = jnp.maximum(n, 1).astype(F32)
    large = max_exact + (jnp.log(nf / max_exact) / math.log(MAX_DISTANCE / max_exact)
                         * (NUM_BUCKETS - max_exact)).astype(jnp.int32)
    large = jnp.minimum(large, NUM_BUCKETS - 1)
    return jnp.where(n < max_exact, n, large)


def _band_bias_kernel(tab_ref, bucket_ref, o_ref, *, heads):
    g = pl.program_id(0)
    h = pl.program_id(1)
    bucket = bucket_ref[...]
    acc = jnp.full(bucket.shape, MASK_VALUE, F32)
    for b in range(NUM_BUCKETS):
        acc = jnp.where(bucket == b, tab_ref[b, g * heads + h], acc)
    o_ref[...] = acc


def _band_bias(rel_bias, buckets, heads):
    g = buckets.shape[0]
    blk = buckets.shape[1:]
    blocks = _nbytes(blk, jnp.int32) + _nbytes(blk, F32)
    return pl.pallas_call(
        functools.partial(_band_bias_kernel, heads=heads),
        grid=(g, heads),
        in_specs=[pl.BlockSpec(memory_space=pltpu.SMEM),
                  pl.BlockSpec((None,) + blk, lambda gi, hi: (gi, 0, 0))],
        out_specs=pl.BlockSpec((None, None) + blk, lambda gi, hi: (gi, hi, 0, 0)),
        out_shape=jax.ShapeDtypeStruct((g, heads) + blk, F32),
        compiler_params=_params(("parallel", "parallel"), blocks),
        name="dilated_band_bias",
    )(rel_bias, buckets)


DIL_HEAD_GROUP = 4


def _dilated_kernel(q_ref, kp_ref, kc_ref, vp_ref, vc_ref, bias_ref, o_ref, lse_ref, s_ref, *, scale, heads):
    n = pl.program_id(2)
    has_prev = n > 0
    d = DIL_HEAD_DIM
    hg = s_ref.shape[1]
    lane = lax.broadcasted_iota(jnp.int32, (Q_BLOCK, LANES), 1)
    col = lax.broadcasted_iota(jnp.int32, (Q_BLOCK, 2 * Q_BLOCK), 1)
    keep = jnp.logical_or(has_prev, col >= Q_BLOCK)

    def scores(grp, slot):
        for i in range(hg):
            sl = slice((grp * hg + i) * d, (grp * hg + i + 1) * d)
            kcat = jnp.concatenate([kp_ref[:, sl], kc_ref[:, sl]], axis=0)
            s_ref[slot, i] = _dot_nt(q_ref[:, sl], kcat)

    def softmax_pv(grp, slot, lse_tile):
        for i in range(hg):
            h = grp * hg + i
            sl = slice(h * d, (h + 1) * d)
            s = jnp.where(keep, s_ref[slot, i] * scale + bias_ref[h], MASK_VALUE)
            m = jnp.max(s, axis=-1, keepdims=True)
            p = jnp.exp(s - m)
            den = jnp.sum(p, axis=-1, keepdims=True)
            vcat = jnp.concatenate([vp_ref[:, sl], vc_ref[:, sl]], axis=0)
            o_ref[:, sl] = _dot((p / den).astype(BF16), vcat).astype(o_ref.dtype)
            lse_tile = jnp.where(lane == h, m + jnp.log(den), lse_tile)
        return lse_tile

    lse_tile = jnp.zeros((Q_BLOCK, LANES), F32)
    n_grp = heads // hg
    scores(0, 0)
    for grp in range(n_grp):
        if grp + 1 < n_grp:
            scores(grp + 1, (grp + 1) % 2)
        lse_tile = softmax_pv(grp, grp % 2, lse_tile)
    lse_ref[...] = lse_tile


def _dilated_group(q, k, v, bias_g):
    b, dilation, length, hd = q.shape
    hg = math.gcd(DIL_HEAD_GROUP, DIL_HEADS)
    nb = length // Q_BLOCK
    cur = lambda bi, r, n: (bi, r, n, 0)
    prev = lambda bi, r, n: (bi, r, jnp.maximum(n - 1, 0), 0)
    blk = (None, None, Q_BLOCK, hd)
    blocks = 6 * _nbytes((Q_BLOCK, hd), BF16) + _nbytes(bias_g.shape, F32) + _nbytes((Q_BLOCK, LANES), F32)
    return pl.pallas_call(
        functools.partial(_dilated_kernel, scale=DIL_HEAD_DIM ** -0.5, heads=DIL_HEADS),
        grid=(b, dilation, nb),
        in_specs=[pl.BlockSpec(blk, cur),
                  pl.BlockSpec(blk, prev), pl.BlockSpec(blk, cur),
                  pl.BlockSpec(blk, prev), pl.BlockSpec(blk, cur),
                  pl.BlockSpec(bias_g.shape, lambda bi, r, n: (0, 0, 0))],
        out_specs=[pl.BlockSpec(blk, cur),
                   pl.BlockSpec((None, None, Q_BLOCK, LANES), cur)],
        out_shape=[jax.ShapeDtypeStruct((b, dilation, length, hd), BF16),
                   jax.ShapeDtypeStruct((b, dilation, length, LANES), F32)],
        scratch_shapes=[pltpu.VMEM((2, hg, Q_BLOCK, 2 * Q_BLOCK), F32)],
        compiler_params=_params(("parallel", "parallel", "arbitrary"), blocks,
                                scratch_bytes=2 * hg * _nbytes((Q_BLOCK, 2 * Q_BLOCK), F32)),
        name=f"dilated_attention_d{dilation}",
    )(q, k, k, v, v, bias_g)


def _combine_kernel(*refs, heads, dilations):
    ng = len(dilations)
    n_perm = sum(dil > 1 for dil in dilations)
    o_refs, l_refs = refs[:ng], refs[ng:2 * ng]
    p_refs = refs[2 * ng:2 * ng + n_perm]
    o_ref, lse_scr, o_scr = refs[2 * ng + n_perm:]
    tm = o_ref.shape[0]
    ip = 0
    for g, dil in enumerate(dilations):
        for r in range(dil):
            rows = pl.ds(r, tm // dil, stride=dil) if dil > 1 else slice(None)
            lse_scr[g, rows, :] = l_refs[g][r]
        og = o_refs[g][...].reshape(tm, o_ref.shape[1])
        if dil > 1:
            o_scr[g] = _dot(p_refs[ip][...], og)
            ip += 1
        else:
            o_scr[g] = og.astype(F32)
    lse = [lse_scr[g] for g in range(ng)]
    m = functools.reduce(jnp.maximum, lse)
    e = [jnp.exp(l - m) for l in lse]
    tot = functools.reduce(jnp.add, e)
    w = [x / tot for x in e]
    d = DIL_HEAD_DIM
    for h in range(heads):
        sl = slice(h * d, (h + 1) * d)
        o = functools.reduce(jnp.add, [w[g][:, h:h + 1] * o_scr[g, :, sl] for g in range(ng)])
        o_ref[:, sl] = o.astype(o_ref.dtype)


def _combine_groups(outs, lses, seq, tm=256):
    dilations = tuple(o.shape[1] for o in outs)
    bsz, hd = outs[0].shape[0], outs[0].shape[3]
    ng = len(outs)
    tm = _tile(seq, tm)
    nsb = seq // tm
    m = bsz * seq
    idx = lambda i: (i // nsb, 0, i % nsb, 0)
    o_specs = [pl.BlockSpec((None, dil, tm // dil, hd), idx) for dil in dilations]
    l_specs = [pl.BlockSpec((None, dil, tm // dil, LANES), idx) for dil in dilations]
    for dil in dilations:
        assert tm % (16 * dil) == 0
    perms = [_gather_perm(tm, dil).T for dil in dilations if dil > 1]
    p_specs = [pl.BlockSpec((tm, tm), lambda i: (0, 0)) for _ in perms]
    blocks = ((ng + 1) * _nbytes((tm, hd), BF16) + ng * _nbytes((tm, LANES), F32)
              + len(perms) * _nbytes((tm, tm), BF16))
    scratch = ng * (_nbytes((tm, LANES), F32) + _nbytes((tm, hd), F32))
    return pl.pallas_call(
        functools.partial(_combine_kernel, heads=DIL_HEADS, dilations=dilations),
        grid=(m // tm,),
        in_specs=o_specs + l_specs + p_specs,
        out_specs=pl.BlockSpec((tm, hd), lambda i: (i, 0)),
        out_shape=jax.ShapeDtypeStruct((m, hd), BF16),
        scratch_shapes=[pltpu.VMEM((ng, tm, LANES), F32), pltpu.VMEM((ng, tm, hd), F32)],
        compiler_params=_params(("parallel",), blocks, scratch_bytes=2 * scratch),
        name="dilated_combine",
    )(*outs, *lses, *perms)


GATE_UP_TILE = 512
GATE_UP_ROWS = 1024
DOWN_TILE = 512
DOWN_K_MAX = 6144


def _rope_tables(seq):
    half = QK_ROPE // 2
    inv = ROPE_THETA ** (-jnp.arange(half, dtype=F32) / half)
    ang = jnp.arange(seq).astype(F32)[:, None] * inv[None, :]
    cos, sin = jnp.cos(ang), jnp.sin(ang)
    zeros = jnp.zeros_like(cos)
    cos_t = jnp.concatenate([cos, cos, zeros, zeros], axis=1)
    sin_lo = jnp.concatenate([-sin, zeros, zeros, zeros], axis=1)
    sin_hi = jnp.concatenate([zeros, sin, zeros, zeros], axis=1)
    return cos_t, sin_lo, sin_hi


def _band_buckets():
    r = jnp.arange(Q_BLOCK)[:, None]
    c = jnp.arange(2 * Q_BLOCK)[None, :]
    steps = Q_BLOCK + r - c
    out = []
    for window, dilation in DIL_GROUPS:
        band = (steps >= 0) & (steps <= window // dilation)
        out.append(jnp.where(band, _t5_causal_bucket(steps * dilation), -1))
    return jnp.stack(out).astype(jnp.int32)


def _ffn_cast_specs(ffn_wg, ffn_wu, ffn_wd, lead):
    f = ffn_wg.shape[-1]
    fp = -(-f // GATE_UP_TILE) * GATE_UP_TILE
    return [(ffn_wg, lead, None, fp), (ffn_wu, lead, None, fp), (ffn_wd, lead, fp, None)]


def _ffn(h, xn, weights, next_cast_specs=()):
    wg, wu, wd = weights
    act, next_weights = _mm_swiglu(xn, wg, wu, tm=GATE_UP_ROWS, tn=GATE_UP_TILE, ride_specs=next_cast_specs)
    return _mm_residual(act, wd, h, 0.5, tn=DOWN_TILE, tk_max=DOWN_K_MAX), next_weights


def kernel(x, ffn_norm, ffn_wg, ffn_wu, ffn_wd, attn_norm, mla_wdq, mla_q_lora_norm, mla_wuq, mla_wdkv, mla_kv_lora_norm, mla_wukv, mla_q_norm, mla_k_norm, mla_wo, kv_src_norm, w_kv_shared, k_norm_shared, rel_bias, dil_wq, dil_q_norm, dil_wo):
    bsz, seq, d = x.shape
    m = bsz * seq
    depth = ffn_norm.shape[0]
    n_a = depth // 2
    h = x.reshape(m, d)
    tabs = _rope_tables(seq)
    hd = DIL_HEADS * DIL_HEAD_DIM
    k_sh = v_sh = None
    bias_all = None

    for window, dilation in DIL_GROUPS:
        assert seq % (dilation * Q_BLOCK) == 0 and window // dilation <= Q_BLOCK
    bq, bk = min(FLASH_BQ, seq), min(FLASH_BK, seq)

    dils = tuple(dilation for _, dilation in DIL_GROUPS)
    gathered = lambda t, dil: t.reshape(bsz, dil, seq // dil, hd)

    ffn_order = [(l, p) for l in range(depth) for p in (0, 1)]
    ffn_specs = {lp: _ffn_cast_specs(ffn_wg, ffn_wu, ffn_wd, lp) for lp in ffn_order}
    ffn_weights = [_cast_bf16(*spec) for spec in ffn_specs[ffn_order[0]]]

    extra_rides = {ffn_order[0]: ("w_kv_shared", (w_kv_shared, (), None, None))}
    if depth > n_a and len(ffn_order) > 1:
        extra_rides[ffn_order[1]] = ("dil_wq", (dil_wq, (0,), None, None))
    side = {}

    def ffn(h, xn, lp):
        nxt = ffn_order.index(lp) + 1
        specs = list(ffn_specs[ffn_order[nxt]]) if nxt < len(ffn_order) else []
        if lp in extra_rides:
            specs.append(extra_rides[lp][1])
        h, cast = _ffn(h, xn, ffn_weights, specs)
        if lp in extra_rides:
            side[extra_rides[lp][0]] = cast[-1]
        return h, cast[:3]

    for l in range(depth):
        if l == n_a:
            gains = jnp.stack([kv_src_norm] * N_GROUPS + [ffn_norm[l, 0]])
            *xs, xn = _rmsnorm(h, gains, dils + (1,), bsz, seq)
            wkv = side["w_kv_shared"] if "w_kv_shared" in side else _cast_bf16(w_kv_shared)
            k_sh, v_sh = [], []
            for g, dil in enumerate(dils):
                a_g = xs[g].reshape(m, d)
                gk = jnp.tile(k_norm_shared[g], DIL_HEADS)[None, :]
                k_sh.append(gathered(_mm_headnorm(a_g, wkv, gk, DIL_HEAD_DIM, col0=g * hd), dil))
                v_sh.append(gathered(_mm_plain(a_g, wkv, BF16, n=hd, col0=(N_GROUPS + g) * hd), dil))
            bias_all = _band_bias(rel_bias, _band_buckets(), DIL_HEADS)
        else:
            (xn,) = _rmsnorm(h, ffn_norm[l, 0][None, :], (1,), bsz, seq)
        h, ffn_weights = ffn(h, xn, (l, 0))

        if l < n_a:
            (xn,) = _rmsnorm(h, attn_norm[l][None, :], (1,), bsz, seq)
            a = l
            q_lora = mla_wdq.shape[2]
            kv_lora = mla_wdkv.shape[2] - QK_ROPE
            wcat = jnp.concatenate(
                [mla_wdq[a], mla_wdkv[a], jnp.zeros((d, LANES - QK_ROPE), F32)], axis=1).astype(BF16)
            cq, ckv, k_rope = _mla_down(xn, wcat, mla_q_lora_norm[a][None, :],
                                        mla_kv_lora_norm[a][None, :], q_lora, kv_lora)
            wuq_p = jnp.pad(mla_wuq[a].reshape(q_lora, MLA_HEADS, QK_HEAD),
                            ((0, 0), (0, 0), (0, QK_PAD - QK_HEAD))).reshape(q_lora, -1).astype(BF16)
            q_scale = QK_HEAD ** -0.5 * math.log2(math.e)
            gq_p = jnp.pad(mla_q_norm[a] * q_scale, (0, QK_PAD - QK_HEAD))[None, :]
            q = _mla_q(cq, wuq_p, gq_p, tabs, seq)
            g_nope = mla_k_norm[a][:QK_NOPE][None, :]
            g_rope = jnp.pad(mla_k_norm[a][QK_NOPE:], (0, LANES - QK_ROPE))[None, :]
            k, vt = _mla_kv(ckv, _cast_bf16(mla_wukv, (a,)), k_rope, g_nope, g_rope, tabs, bsz, seq, bk)
            o = _mla_flash(q.reshape(bsz, seq, -1), k.reshape(bsz, seq, -1), vt, bq, bk)
            h = _mm_residual(o.reshape(m, -1), _cast_bf16(mla_wo, (a,)), h, 1.0)
        else:
            bl = l - n_a
            xq = _rmsnorm(h, jnp.stack([attn_norm[l]] * N_GROUPS), dils, bsz, seq)
            wq = side["dil_wq"] if bl == 0 and "dil_wq" in side else _cast_bf16(dil_wq, (bl,))
            outs, lses = [], []
            for g, dil in enumerate(dils):
                gq = jnp.tile(dil_q_norm[bl][g], DIL_HEADS)[None, :]
                q_g = gathered(_mm_headnorm(xq[g].reshape(m, d), wq, gq, DIL_HEAD_DIM, col0=g * hd), dil)
                o_g, l_g = _dilated_group(q_g, k_sh[g], v_sh[g], bias_all[g])
                outs.append(o_g)
                lses.append(l_g)
            o = _combine_groups(outs, lses, seq)
            h = _mm_residual(o, _cast_bf16(dil_wo, (bl,)), h, 1.0)

        (xn,) = _rmsnorm(h, ffn_norm[l, 1][None, :], (1,), bsz, seq)
        h, ffn_weights = ffn(h, xn, (l, 1))

    return h.reshape(bsz, seq, d)
```

```python
import functools
import math

import jax
import jax.numpy as jnp
import numpy as np
from jax import lax
from jax.experimental import pallas as pl
from jax.experimental.pallas import tpu as pltpu

F32 = jnp.float32
BF16 = jnp.bfloat16

RMS_EPS = 1e-6
MLA_HEADS = 32
QK_NOPE = 128
QK_ROPE = 64
QK_HEAD = QK_NOPE + QK_ROPE
V_HEAD = 128
ROPE_THETA = 10000.0
DIL_GROUPS = ((128, 1), (512, 4), (2048, 16))
N_GROUPS = 3
DIL_HEADS = 32
DIL_HEAD_DIM = 128
NUM_BUCKETS = 32
MAX_DISTANCE = 2048
Q_BLOCK = 128

LANES = 128
QK_PAD = 2 * LANES
V7X_VMEM_BYTES = 64 * 1024 * 1024
VMEM_HEADROOM_BYTES = 8 * 1024 * 1024
MASK_VALUE = -1e30


def _nbytes(shape, dtype):
    return int(np.prod(shape)) * jnp.dtype(dtype).itemsize


def _params(semantics, block_bytes, scratch_bytes=0):
    need = 2 * block_bytes + scratch_bytes + VMEM_HEADROOM_BYTES
    limit = min(max(need, 32 * 1024 * 1024), V7X_VMEM_BYTES - 4 * 1024 * 1024)
    return pltpu.CompilerParams(dimension_semantics=semantics, vmem_limit_bytes=int(limit))


def _tile(dim, pref):
    if dim <= pref:
        return dim
    t = pref
    while dim % t:
        t //= 2
    return t


def _gather_perm(tm, dil):
    n = tm // dil
    i = np.arange(tm)
    p = np.zeros((tm, tm), np.float32)
    p[i, (i % n) * dil + i // n] = 1.0
    return jnp.asarray(p, BF16)


def _rmsnorm_kernel(x_ref, g_ref, *refs, dilations):
    n_perm = sum(dil > 1 for dil in dilations)
    p_refs, o_refs = refs[:n_perm], refs[n_perm:]
    x = x_ref[...]
    y = x * lax.rsqrt(jnp.mean(x * x, axis=-1, keepdims=True) + RMS_EPS)
    tm = x.shape[0]
    ip = 0
    for i, (o_ref, dil) in enumerate(zip(o_refs, dilations)):
        yg = (y * g_ref[i:i + 1, :]).astype(o_ref.dtype)
        if dil == 1:
            o_ref[...] = yg
        else:
            yp = _dot(p_refs[ip][...], yg).astype(o_ref.dtype)
            ip += 1
            n = tm // dil
            for r in range(dil):
                o_ref[r] = yp[r * n:(r + 1) * n, :]


def _rmsnorm(x, gains, dilations, bsz, seq, tm=256):
    m, d = x.shape
    n = gains.shape[0]
    tm = _tile(seq, tm)
    nsb = seq // tm
    out_specs, out_shape, perms = [], [], []
    for dil in dilations:
        if dil == 1:
            out_specs.append(pl.BlockSpec((tm, d), lambda i: (i, 0)))
            out_shape.append(jax.ShapeDtypeStruct((m, d), BF16))
        else:
            assert tm % (16 * dil) == 0
            perms.append(_gather_perm(tm, dil))
            out_specs.append(pl.BlockSpec((None, dil, tm // dil, d), lambda i: (i // nsb, 0, i % nsb, 0)))
            out_shape.append(jax.ShapeDtypeStruct((bsz, dil, seq // dil, d), BF16))
    blocks = (_nbytes((tm, d), F32) + n * _nbytes((tm, d), BF16) + _nbytes((n, d), F32)
              + len(perms) * _nbytes((tm, tm), BF16))
    return pl.pallas_call(
        functools.partial(_rmsnorm_kernel, dilations=tuple(dilations)),
        grid=(m // tm,),
        in_specs=[pl.BlockSpec((tm, d), lambda i: (i, 0)),
                  pl.BlockSpec((n, d), lambda i: (0, 0))]
                 + [pl.BlockSpec((tm, tm), lambda i: (0, 0)) for _ in perms],
        out_specs=out_specs,
        out_shape=out_shape,
        compiler_params=_params(("parallel",), blocks, scratch_bytes=3 * _nbytes((tm, d), F32)),
        name="rmsnorm",
    )(x, gains, *perms)


CAST_BLOCK_BYTES = 12 * 1024 * 1024


def _cast_block(x_ref, o_ref, block, rows_in, pad_rows):
    tr, cols_in = x_ref.shape
    x = x_ref[...].astype(o_ref.dtype)
    if o_ref.shape[1] > cols_in:
        o_ref[:, cols_in:] = jnp.zeros((tr, o_ref.shape[1] - cols_in), o_ref.dtype)
    if pad_rows:
        row = block * tr + lax.broadcasted_iota(jnp.int32, x.shape, 0)
        x = jnp.where(row < rows_in, x, jnp.zeros_like(x))
    o_ref[:, :cols_in] = x


def _cast_kernel(x_ref, o_ref, *, rows_in, pad_rows):
    _cast_block(x_ref, o_ref, pl.program_id(0), rows_in, pad_rows)


def _cast_bf16(w, lead=(), rows_out=None, cols_out=None):
    r, c = w.shape[-2:]
    rows_out = rows_out or r
    cols_out = cols_out or c
    tr = 8
    while tr * 2 * c * 4 <= CAST_BLOCK_BYTES and r % (tr * 2) == 0 and rows_out % (tr * 2) == 0:
        tr *= 2
    assert r % tr == 0 and rows_out % tr == 0 and c % LANES == 0
    n_in = r // tr
    squeeze = (None,) * len(lead)
    blocks = _nbytes((tr, c), F32) + _nbytes((tr, cols_out), BF16)
    return pl.pallas_call(
        functools.partial(_cast_kernel, rows_in=r, pad_rows=rows_out > r),
        grid=(rows_out // tr,),
        in_specs=[pl.BlockSpec(squeeze + (tr, c), lambda i: tuple(lead) + (jnp.minimum(i, n_in - 1), 0))],
        out_specs=pl.BlockSpec((tr, cols_out), lambda i: (i, 0)),
        out_shape=jax.ShapeDtypeStruct((rows_out, cols_out), BF16),
        compiler_params=_params(("parallel",), blocks),
        name="cast_bf16",
    )(w)


class _RidingCast:
    def __init__(self, w, lead, rows_out, cols_out, n_steps):
        self.w, self.lead = w, tuple(lead)
        self.rows_in, self.cols_in = w.shape[-2:]
        self.rows_out, self.cols_out = rows_out or self.rows_in, cols_out or self.cols_in
        tr = 16
        while (self.rows_out % tr or self.rows_in % tr or self.rows_out // tr > n_steps):
            tr += 16
        self.tr = tr
        self.n_out, self.n_in = self.rows_out // tr, self.rows_in // tr

    def specs(self, step_of):
        squeeze = (None,) * len(self.lead)
        blk = lambda *g: jnp.minimum(step_of(*g), self.n_out - 1)
        return (pl.BlockSpec(squeeze + (self.tr, self.cols_in),
                             lambda *g: self.lead + (jnp.minimum(blk(*g), self.n_in - 1), 0)),
                pl.BlockSpec((self.tr, self.cols_out), lambda *g: (blk(*g), 0)),
                jax.ShapeDtypeStruct((self.rows_out, self.cols_out), BF16))

    def block_bytes(self):
        return _nbytes((self.tr, self.cols_in), F32) + _nbytes((self.tr, self.cols_out), BF16)

    def run(self, x_ref, o_ref, step):
        _cast_block(x_ref, o_ref, jnp.minimum(step, self.n_out - 1), self.rows_in,
                    self.rows_out > self.rows_in)


def _dot(a, b):
    return jnp.dot(a, b, preferred_element_type=F32)


def _swiglu_kernel(a_ref, wg_ref, wu_ref, *refs, rides):
    nr = len(rides)
    o_ref = refs[nr]
    step = pl.program_id(0) * pl.num_programs(1) + pl.program_id(1)
    for ride, src_ref, dst_ref in zip(rides, refs[:nr], refs[nr + 1:]):
        ride.run(src_ref, dst_ref, step)
    a = a_ref[...]
    g = _dot(a, wg_ref[...])
    u = _dot(a, wu_ref[...])
    o_ref[...] = (g * jax.nn.sigmoid(g) * u).astype(o_ref.dtype)


def _mm_swiglu(a, wg, wu, tm, tn, ride_specs=()):
    m, k = a.shape
    n = wg.shape[1]
    tm = _tile(m, tm)
    assert n % tn == 0
    nj = n // tn
    rides = [_RidingCast(*spec, n_steps=(m // tm) * nj) for spec in ride_specs]
    ride_io = [ride.specs(lambda i, j: i * nj + j) for ride in rides]
    blocks = (_nbytes((tm, k), BF16) + 2 * _nbytes((k, tn), BF16) + _nbytes((tm, tn), BF16)
              + sum(ride.block_bytes() for ride in rides))
    out = pl.pallas_call(
        functools.partial(_swiglu_kernel, rides=rides),
        grid=(m // tm, nj),
        in_specs=[pl.BlockSpec((tm, k), lambda i, j: (i, 0)),
                  pl.BlockSpec((k, tn), lambda i, j: (0, j)),
                  pl.BlockSpec((k, tn), lambda i, j: (0, j))] + [io[0] for io in ride_io],
        out_specs=[pl.BlockSpec((tm, tn), lambda i, j: (i, j))] + [io[1] for io in ride_io],
        out_shape=[jax.ShapeDtypeStruct((m, n), BF16)] + [io[2] for io in ride_io],
        compiler_params=_params(("arbitrary", "arbitrary") if rides else ("parallel", "parallel"), blocks,
                                scratch_bytes=3 * _nbytes((tm, tn), F32)),
        name="ffn_gate_up",
    )(a, wg, wu, *[ride.w for ride in rides])
    return out[0], out[1:]


def _residual_kernel(a_ref, b_ref, r_ref, o_ref, *, scale):
    o_ref[...] = r_ref[...] + scale * _dot(a_ref[...], b_ref[...])


def _mm_residual(a, b, res, scale, tm=1024, tn=1024, tk_max=4096):
    m, kdim = a.shape
    n = b.shape[1]
    tm, tn = _tile(m, tm), _tile(n, tn)
    nk = 1
    while kdim // nk > tk_max or kdim % nk or (kdim // nk) % LANES:
        nk += 1
    tk = kdim // nk
    blocks = (_nbytes((tm, tk), BF16) + _nbytes((tk, tn), BF16) + 2 * _nbytes((tm, tn), F32))
    out = res
    for kc in range(nk):
        out = pl.pallas_call(
            functools.partial(_residual_kernel, scale=scale),
            grid=(m // tm, n // tn),
            in_specs=[pl.BlockSpec((tm, tk), lambda i, j, kc=kc: (i, kc)),
                      pl.BlockSpec((tk, tn), lambda i, j, kc=kc: (kc, j)),
                      pl.BlockSpec((tm, tn), lambda i, j: (i, j))],
            out_specs=pl.BlockSpec((tm, tn), lambda i, j: (i, j)),
            out_shape=jax.ShapeDtypeStruct((m, n), F32),
            compiler_params=_params(("parallel", "parallel"), blocks,
                                    scratch_bytes=_nbytes((tm, tn), F32)),
            name="matmul_residual",
        )(a, b, out)
    return out


def _plain_kernel(a_ref, b_ref, o_ref):
    o_ref[...] = _dot(a_ref[...], b_ref[...]).astype(o_ref.dtype)


def _mm_plain(a, b, out_dtype, n=None, col0=0, tm=1024, tn=1024):
    m, k = a.shape
    n = n or b.shape[1]
    tm, tn = _tile(m, tm), _tile(n, tn)
    assert col0 % tn == 0
    joff = col0 // tn
    blocks = _nbytes((tm, k), BF16) + _nbytes((k, tn), BF16) + _nbytes((tm, tn), out_dtype)
    return pl.pallas_call(
        _plain_kernel,
        grid=(m // tm, n // tn),
        in_specs=[pl.BlockSpec((tm, k), lambda i, j: (i, 0)),
                  pl.BlockSpec((k, tn), lambda i, j: (0, j + joff))],
        out_specs=pl.BlockSpec((tm, tn), lambda i, j: (i, j)),
        out_shape=jax.ShapeDtypeStruct((m, n), out_dtype),
        compiler_params=_params(("parallel", "parallel"), blocks,
                                scratch_bytes=_nbytes((tm, tn), F32)),
        name="matmul_plain",
    )(a, b)


def _headnorm_kernel(a_ref, b_ref, g_ref, o_ref, *, head_dim):
    x = _dot(a_ref[...], b_ref[...])
    tn = x.shape[1]
    for h in range(tn // head_dim):
        sl = slice(h * head_dim, (h + 1) * head_dim)
        xh = x[:, sl]
        y = xh * lax.rsqrt(jnp.mean(xh * xh, axis=-1, keepdims=True) + RMS_EPS)
        o_ref[:, sl] = (y * g_ref[:, sl]).astype(o_ref.dtype)


def _mm_headnorm(a, b, gain_row, head_dim, col0=0, tm=1024, tn=1024):
    m, k = a.shape
    n = gain_row.shape[1]
    tm, tn = _tile(m, tm), _tile(n, tn)
    assert col0 % tn == 0
    joff = col0 // tn
    blocks = (_nbytes((tm, k), BF16) + _nbytes((k, tn), BF16) + _nbytes((tm, tn), BF16)
              + _nbytes((8, tn), F32))
    return pl.pallas_call(
        functools.partial(_headnorm_kernel, head_dim=head_dim),
        grid=(m // tm, n // tn),
        in_specs=[pl.BlockSpec((tm, k), lambda i, j: (i, 0)),
                  pl.BlockSpec((k, tn), lambda i, j: (0, j + joff)),
                  pl.BlockSpec((1, tn), lambda i, j: (0, j))],
        out_specs=pl.BlockSpec((tm, tn), lambda i, j: (i, j)),
        out_shape=jax.ShapeDtypeStruct((m, n), BF16),
        compiler_params=_params(("parallel", "parallel"), blocks,
                                scratch_bytes=_nbytes((tm, tn), F32)),
        name="matmul_headnorm",
    )(a, b, gain_row)


def _mla_down_kernel(a_ref, w_ref, gq_ref, gkv_ref, cq_ref, ckv_ref, kr_ref, *, q_lora, kv_lora):
    x = _dot(a_ref[...], w_ref[...])
    xq = x[:, :q_lora]
    cq = xq * lax.rsqrt(jnp.mean(xq * xq, axis=-1, keepdims=True) + RMS_EPS) * gq_ref[...]
    cq_ref[...] = cq.astype(cq_ref.dtype)
    xkv = x[:, q_lora:q_lora + kv_lora]
    ckv = xkv * lax.rsqrt(jnp.mean(xkv * xkv, axis=-1, keepdims=True) + RMS_EPS) * gkv_ref[...]
    ckv_ref[...] = ckv.astype(ckv_ref.dtype)
    kr_ref[...] = x[:, q_lora + kv_lora:]


def _mla_down(a, wcat, gq, gkv, q_lora, kv_lora, tm=512):
    m, k = a.shape
    n = wcat.shape[1]
    nr = n - q_lora - kv_lora
    tm = _tile(m, tm)
    blocks = (_nbytes((tm, k), BF16) + _nbytes((k, n), BF16) + _nbytes((tm, q_lora), BF16)
              + _nbytes((tm, kv_lora), BF16) + _nbytes((tm, nr), F32))
    return pl.pallas_call(
        functools.partial(_mla_down_kernel, q_lora=q_lora, kv_lora=kv_lora),
        grid=(m // tm,),
        in_specs=[pl.BlockSpec((tm, k), lambda i: (i, 0)),
                  pl.BlockSpec((k, n), lambda i: (0, 0)),
                  pl.BlockSpec((1, q_lora), lambda i: (0, 0)),
                  pl.BlockSpec((1, kv_lora), lambda i: (0, 0))],
        out_specs=[pl.BlockSpec((tm, q_lora), lambda i: (i, 0)),
                   pl.BlockSpec((tm, kv_lora), lambda i: (i, 0)),
                   pl.BlockSpec((tm, nr), lambda i: (i, 0))],
        out_shape=[jax.ShapeDtypeStruct((m, q_lora), BF16),
                   jax.ShapeDtypeStruct((m, kv_lora), BF16),
                   jax.ShapeDtypeStruct((m, nr), F32)],
        compiler_params=_params(("parallel",), blocks, scratch_bytes=_nbytes((tm, n), F32)),
        name="mla_down",
    )(a, wcat, gq, gkv)


def _mla_q_kernel(a_ref, w_ref, gn_ref, gr_ref, ta_ref, o_ref):
    x = _dot(a_ref[...], w_ref[...])
    rope_scale = gr_ref[...] * ta_ref[...]
    for h in range(x.shape[1] // QK_PAD):
        xn = x[:, h * QK_PAD:h * QK_PAD + LANES]
        xr = x[:, h * QK_PAD + LANES:(h + 1) * QK_PAD]
        ms = jnp.sum(xn * xn + 0.5 * (xr * xr), axis=-1, keepdims=True) * (1.0 / QK_HEAD)
        rs = lax.rsqrt(ms + RMS_EPS)
        o_ref[:, h * QK_PAD:h * QK_PAD + LANES] = (xn * rs * gn_ref[...]).astype(o_ref.dtype)
        o_ref[:, h * QK_PAD + LANES:(h + 1) * QK_PAD] = (xr * rs * rope_scale).astype(o_ref.dtype)


def _mla_q(cq, wuq_p, g_nope, g_rope, t_a, seq, tm=1024, tn=1024):
    m, k = cq.shape
    n = wuq_p.shape[1]
    tm, tn = _tile(seq, tm), _tile(n, tn)
    nsb = seq // tm
    row_spec = pl.BlockSpec((1, LANES), lambda i, j: (0, 0))
    blocks = (_nbytes((tm, k), BF16) + _nbytes((k, tn), BF16) + _nbytes((tm, tn), BF16)
              + _nbytes((tm, LANES), F32))
    return pl.pallas_call(
        _mla_q_kernel,
        grid=(m // tm, n // tn),
        in_specs=[pl.BlockSpec((tm, k), lambda i, j: (i, 0)),
                  pl.BlockSpec((k, tn), lambda i, j: (0, j)),
                  row_spec, row_spec,
                  pl.BlockSpec((tm, LANES), lambda i, j: (i % nsb, 0))],
        out_specs=pl.BlockSpec((tm, tn), lambda i, j: (i, j)),
        out_shape=jax.ShapeDtypeStruct((m, n), BF16),
        compiler_params=_params(("parallel", "parallel"), blocks,
                                scratch_bytes=_nbytes((tm, tn), F32)),
        name="mla_q_proj",
    )(cq, wuq_p, g_nope, g_rope, t_a)


FLASH_BQ = 2048
FLASH_BK = 1024
FLASH_COL_CHUNK = 512


def _mla_kv_kernel(a_ref, w_ref, kr_ref, gn_ref, ga_ref, gb_ref, ta_ref, tb_ref, k_ref, vt_ref, *, bk):
    x = _dot(a_ref[...], w_ref[...])
    kra = kr_ref[:, :LANES]
    krb = kr_ref[:, LANES:]
    ss_rope = 0.5 * jnp.sum(kra * kra, axis=-1, keepdims=True)
    kr_roped = kra * (ga_ref[...] * ta_ref[...]) + krb * (gb_ref[...] * tb_ref[...])
    width = QK_NOPE + V_HEAD
    for h in range(x.shape[1] // width):
        kn = x[:, h * width:h * width + QK_NOPE]
        ms = (jnp.sum(kn * kn, axis=-1, keepdims=True) + ss_rope) * (1.0 / QK_HEAD)
        rs = lax.rsqrt(ms + RMS_EPS)
        k_ref[:, h * QK_PAD:h * QK_PAD + LANES] = (kn * rs * gn_ref[...]).astype(k_ref.dtype)
        k_ref[:, h * QK_PAD + LANES:(h + 1) * QK_PAD] = (kr_roped * rs).astype(k_ref.dtype)
        for c in range(x.shape[0] // bk):
            v = x[c * bk:(c + 1) * bk, h * width + QK_NOPE:(h + 1) * width]
            vt_ref[h, c] = v.T.astype(vt_ref.dtype)


def _mla_kv(ckv, wukv, k_rope, g_nope, g_a, g_b, t_a, t_b, bsz, seq, bk, tm=1024, tn=1024):
    m, k = ckv.shape
    n = wukv.shape[1]
    width = QK_NOPE + V_HEAD
    tm, tn = _tile(seq, tm), _tile(n, tn)
    assert tm % bk == 0
    hpt = tn // width
    nsb = seq // tm
    tab_spec = pl.BlockSpec((tm, LANES), lambda i, j: (i % nsb, 0))
    row_spec = pl.BlockSpec((1, LANES), lambda i, j: (0, 0))
    blocks = (_nbytes((tm, k), BF16) + _nbytes((k, tn), BF16) + _nbytes((tm, hpt * QK_PAD), BF16)
              + _nbytes((tm, hpt * V_HEAD), BF16) + 4 * _nbytes((tm, LANES), F32))
    return pl.pallas_call(
        functools.partial(_mla_kv_kernel, bk=bk),
        grid=(m // tm, n // tn),
        in_specs=[pl.BlockSpec((tm, k), lambda i, j: (i, 0)),
                  pl.BlockSpec((k, tn), lambda i, j: (0, j)),
                  pl.BlockSpec((tm, 2 * LANES), lambda i, j: (i, 0)),
                  row_spec, row_spec, row_spec,
                  tab_spec, tab_spec],
        out_specs=[pl.BlockSpec((tm, hpt * QK_PAD), lambda i, j: (i, j)),
                   pl.BlockSpec((None, hpt, tm // bk, V_HEAD, bk), lambda i, j: (i // nsb, j, i % nsb, 0, 0))],
        out_shape=[jax.ShapeDtypeStruct((m, (n // width) * QK_PAD), BF16),
                   jax.ShapeDtypeStruct((bsz, n // width, seq // bk, V_HEAD, bk), BF16)],
        compiler_params=_params(("parallel", "parallel"), blocks,
                                scratch_bytes=2 * _nbytes((tm, tn), F32)),
        name="mla_kv_proj",
    )(ckv, wukv, k_rope, g_nope, g_a, g_b, t_a, t_b)


def _dot_nt(a, b):
    return lax.dot_general(a, b, (((1,), (1,)), ((), ())), preferred_element_type=F32)


def _flash_kernel(q_ref, k_ref, vt_ref, o_ref, m_ref, l_ref, acc_ref, s_ref, *, bq, bk):
    qi = pl.program_id(2)
    half = bk // 2
    m_ref[...] = jnp.full(m_ref.shape, MASK_VALUE, F32)
    l_ref[...] = jnp.zeros(l_ref.shape, F32)
    acc_ref[...] = jnp.zeros(acc_ref.shape, F32)

    def scores(j, slot, c0=0):
        kb = k_ref[pl.ds(pl.multiple_of(j * bk, bk), bk), :]
        s_ref[slot, :, c0:] = _dot_nt(kb, q_ref[c0:, :])

    def softmax_pv(j, slot, r0=0, nr=bk, c0=0, diagonal=False):
        st = s_ref[slot, r0:r0 + nr, c0:]
        if diagonal:
            row = lax.broadcasted_iota(jnp.int32, (nr, nr), 0)
            col = lax.broadcasted_iota(jnp.int32, (nr, nr), 1)
            tri = jnp.where(row <= col, st[:, :nr], MASK_VALUE)
            st = tri if st.shape[1] == nr else jnp.concatenate([tri, st[:, nr:]], axis=1)
        m_old = m_ref[:, c0:]
        m_new = jnp.maximum(m_old, jnp.max(st, axis=0, keepdims=True))
        p = jnp.exp2(st - m_new)
        alpha = jnp.exp2(m_old - m_new)
        l_ref[:, c0:] = alpha * l_ref[:, c0:] + jnp.sum(p, axis=0, keepdims=True)
        acc_ref[:, c0:] = alpha * acc_ref[:, c0:] + _dot(vt_ref[j, :, r0:r0 + nr], p.astype(BF16))
        m_ref[:, c0:] = m_new

    def softmax_pv_chunked(j, slot):
        for c0 in range(0, bq, FLASH_COL_CHUNK):
            c1 = c0 + FLASH_COL_CHUNK
            st = s_ref[slot, :, c0:c1]
            m_old = m_ref[:, c0:c1]
            m_new = jnp.maximum(m_old, jnp.max(st, axis=0, keepdims=True))
            p = jnp.exp2(st - m_new)
            alpha = jnp.exp2(m_old - m_new)
            l_ref[:, c0:c1] = alpha * l_ref[:, c0:c1] + jnp.sum(p, axis=0, keepdims=True)
            acc_ref[:, c0:c1] = alpha * acc_ref[:, c0:c1] + _dot(vt_ref[j], p.astype(BF16))
            m_ref[:, c0:c1] = m_new

    def body(i, carry):
        scores(2 * i + 1, 1)
        softmax_pv_chunked(2 * i, 0)
        scores(2 * i + 2, 0)
        softmax_pv_chunked(2 * i + 1, 1)
        return carry

    scores(0, 0)
    lax.fori_loop(0, qi, body, 0)
    scores(2 * qi + 1, 1, c0=bk)
    for g in range(4):
        softmax_pv(2 * qi + g // 2, g // 2, r0=(g % 2) * half, nr=half, c0=g * half, diagonal=True)
    o_ref[...] = (acc_ref[...] / l_ref[...]).T.astype(o_ref.dtype)


def _mla_flash(q, k, vt, bq, bk):
    b, s, _ = q.shape
    h = q.shape[2] // QK_PAD
    assert s % bq == 0 and bq == 2 * bk
    blocks = (_nbytes((bq, QK_PAD), BF16) + _nbytes((s, QK_PAD), BF16) + _nbytes((s, V_HEAD), BF16)
              + _nbytes((bq, V_HEAD), BF16))
    scratch = 2 * _nbytes((8, bq), F32) + _nbytes((V_HEAD, bq), F32) + 6 * _nbytes((bk, bq), F32)
    return pl.pallas_call(
        functools.partial(_flash_kernel, bq=bq, bk=bk),
        grid=(b, h, s // bq),
        in_specs=[pl.BlockSpec((None, bq, QK_PAD), lambda bi, hi, qi: (bi, qi, hi)),
                  pl.BlockSpec((None, s, QK_PAD), lambda bi, hi, qi: (bi, 0, hi)),
                  pl.BlockSpec((None, None, s // bk, V_HEAD, bk), lambda bi, hi, qi: (bi, hi, 0, 0, 0))],
        out_specs=pl.BlockSpec((None, bq, V_HEAD), lambda bi, hi, qi: (bi, qi, hi)),
        out_shape=jax.ShapeDtypeStruct((b, s, h * V_HEAD), BF16),
        scratch_shapes=[pltpu.VMEM((1, bq), F32), pltpu.VMEM((1, bq), F32),
                        pltpu.VMEM((V_HEAD, bq), F32), pltpu.VMEM((2, bk, bq), F32)],
        compiler_params=_params(("parallel", "parallel", "arbitrary"), blocks, scratch_bytes=scratch),
        name="mla_flash_attention",
    )(q, k, vt)


def _t5_causal_bucket(dist):
    max_exact = NUM_BUCKETS // 2
    n = jnp.maximum(dist, 0)
    nf = jnp.maximum(n, 1).astype(F32)
    large = max_exact + (jnp.log(nf / max_exact) / math.log(MAX_DISTANCE / max_exact)
                         * (NUM_BUCKETS - max_exact)).astype(jnp.int32)
    large = jnp.minimum(large, NUM_BUCKETS - 1)
    return jnp.where(n < max_exact, n, large)


def _band_bias_kernel(tab_ref, bucket_ref, o_ref, *, heads):
    g = pl.program_id(0)
    h = pl.program_id(1)
    bucket = bucket_ref[...]
    acc = jnp.full(bucket.shape, MASK_VALUE, F32)
    for b in range(NUM_BUCKETS):
        acc = jnp.where(bucket == b, tab_ref[b, g * heads + h], acc)
    o_ref[...] = acc


def _band_bias(rel_bias, buckets, heads):
    g = buckets.shape[0]
    blk = buckets.shape[1:]
    blocks = _nbytes(blk, jnp.int32) + _nbytes(blk, F32)
    return pl.pallas_call(
        functools.partial(_band_bias_kernel, heads=heads),
        grid=(g, heads),
        in_specs=[pl.BlockSpec(memory_space=pltpu.SMEM),
                  pl.BlockSpec((None,) + blk, lambda gi, hi: (gi, 0, 0))],
        out_specs=pl.BlockSpec((None, None) + blk, lambda gi, hi: (gi, hi, 0, 0)),
        out_shape=jax.ShapeDtypeStruct((g, heads) + blk, F32),
        compiler_params=_params(("parallel", "parallel"), blocks),
        name="dilated_band_bias",
    )(rel_bias, buckets)


DIL_HEAD_GROUP = 4


def _dilated_kernel(q_ref, kp_ref, kc_ref, vp_ref, vc_ref, bias_ref, o_ref, lse_ref, s_ref, *, scale, heads):
    n = pl.program_id(2)
    has_prev = n > 0
    d = DIL_HEAD_DIM
    hg = s_ref.shape[1]
    lane = lax.broadcasted_iota(jnp.int32, (Q_BLOCK, LANES), 1)
    col = lax.broadcasted_iota(jnp.int32, (Q_BLOCK, 2 * Q_BLOCK), 1)
    keep = jnp.logical_or(has_prev, col >= Q_BLOCK)

    def scores(grp, slot):
        for i in range(hg):
            sl = slice((grp * hg + i) * d, (grp * hg + i + 1) * d)
            kcat = jnp.concatenate([kp_ref[:, sl], kc_ref[:, sl]], axis=0)
            s_ref[slot, i] = _dot_nt(q_ref[:, sl], kcat)

    def softmax_pv(grp, slot, lse_tile):
        for i in range(hg):
            h = grp * hg + i
            sl = slice(h * d, (h + 1) * d)
            s = jnp.where(keep, s_ref[slot, i] * scale + bias_ref[h], MASK_VALUE)
            m = jnp.max(s, axis=-1, keepdims=True)
            p = jnp.exp(s - m)
            den = jnp.sum(p, axis=-1, keepdims=True)
            vcat = jnp.concatenate([vp_ref[:, sl], vc_ref[:, sl]], axis=0)
            o_ref[:, sl] = _dot((p / den).astype(BF16), vcat).astype(o_ref.dtype)
            lse_tile = jnp.where(lane == h, m + jnp.log(den), lse_tile)
        return lse_tile

    lse_tile = jnp.zeros((Q_BLOCK, LANES), F32)
    n_grp = heads // hg
    scores(0, 0)
    for grp in range(n_grp):
        if grp + 1 < n_grp:
            scores(grp + 1, (grp + 1) % 2)
        lse_tile = softmax_pv(grp, grp % 2, lse_tile)
    lse_ref[...] = lse_tile


def _dilated_group(q, k, v, bias_g):
    b, dilation, length, hd = q.shape
    hg = math.gcd(DIL_HEAD_GROUP, DIL_HEADS)
    nb = length // Q_BLOCK
    cur = lambda bi, r, n: (bi, r, n, 0)
    prev = lambda bi, r, n: (bi, r, jnp.maximum(n - 1, 0), 0)
    blk = (None, None, Q_BLOCK, hd)
    blocks = 6 * _nbytes((Q_BLOCK, hd), BF16) + _nbytes(bias_g.shape, F32) + _nbytes((Q_BLOCK, LANES), F32)
    return pl.pallas_call(
        functools.partial(_dilated_kernel, scale=DIL_HEAD_DIM ** -0.5, heads=DIL_HEADS),
        grid=(b, dilation, nb),
        in_specs=[pl.BlockSpec(blk, cur),
                  pl.BlockSpec(blk, prev), pl.BlockSpec(blk, cur),
                  pl.BlockSpec(blk, prev), pl.BlockSpec(blk, cur),
                  pl.BlockSpec(bias_g.shape, lambda bi, r, n: (0, 0, 0))],
        out_specs=[pl.BlockSpec(blk, cur),
                   pl.BlockSpec((None, None, Q_BLOCK, LANES), cur)],
        out_shape=[jax.ShapeDtypeStruct((b, dilation, length, hd), BF16),
                   jax.ShapeDtypeStruct((b, dilation, length, LANES), F32)],
        scratch_shapes=[pltpu.VMEM((2, hg, Q_BLOCK, 2 * Q_BLOCK), F32)],
        compiler_params=_params(("parallel", "parallel", "arbitrary"), blocks,
                                scratch_bytes=2 * hg * _nbytes((Q_BLOCK, 2 * Q_BLOCK), F32)),
        name=f"dilated_attention_d{dilation}",
    )(q, k, k, v, v, bias_g)


def _combine_kernel(*refs, heads, dilations):
    ng = len(dilations)
    n_perm = sum(dil > 1 for dil in dilations)
    o_refs, l_refs = refs[:ng], refs[ng:2 * ng]
    p_refs = refs[2 * ng:2 * ng + n_perm]
    o_ref, lse_scr, o_scr = refs[2 * ng + n_perm:]
    tm = o_ref.shape[0]
    ip = 0
    for g, dil in enumerate(dilations):
        for r in range(dil):
            rows = pl.ds(r, tm // dil, stride=dil) if dil > 1 else slice(None)
            lse_scr[g, rows, :] = l_refs[g][r]
        og = o_refs[g][...].reshape(tm, o_ref.shape[1])
        if dil > 1:
            o_scr[g] = _dot(p_refs[ip][...], og)
            ip += 1
        else:
            o_scr[g] = og.astype(F32)
    lse = [lse_scr[g] for g in range(ng)]
    m = functools.reduce(jnp.maximum, lse)
    e = [jnp.exp(l - m) for l in lse]
    tot = functools.reduce(jnp.add, e)
    w = [x / tot for x in e]
    d = DIL_HEAD_DIM
    for h in range(heads):
        sl = slice(h * d, (h + 1) * d)
        o = functools.reduce(jnp.add, [w[g][:, h:h + 1] * o_scr[g, :, sl] for g in range(ng)])
        o_ref[:, sl] = o.astype(o_ref.dtype)


def _combine_groups(outs, lses, seq, tm=256):
    dilations = tuple(o.shape[1] for o in outs)
    bsz, hd = outs[0].shape[0], outs[0].shape[3]
    ng = len(outs)
    tm = _tile(seq, tm)
    nsb = seq // tm
    m = bsz * seq
    idx = lambda i: (i // nsb, 0, i % nsb, 0)
    o_specs = [pl.BlockSpec((None, dil, tm // dil, hd), idx) for dil in dilations]
    l_specs = [pl.BlockSpec((None, dil, tm // dil, LANES), idx) for dil in dilations]
    for dil in dilations:
        assert tm % (16 * dil) == 0
    perms = [_gather_perm(tm, dil).T for dil in dilations if dil > 1]
    p_specs = [pl.BlockSpec((tm, tm), lambda i: (0, 0)) for _ in perms]
    blocks = ((ng + 1) * _nbytes((tm, hd), BF16) + ng * _nbytes((tm, LANES), F32)
              + len(perms) * _nbytes((tm, tm), BF16))
    scratch = ng * (_nbytes((tm, LANES), F32) + _nbytes((tm, hd), F32))
    return pl.pallas_call(
        functools.partial(_combine_kernel, heads=DIL_HEADS, dilations=dilations),
        grid=(m // tm,),
        in_specs=o_specs + l_specs + p_specs,
        out_specs=pl.BlockSpec((tm, hd), lambda i: (i, 0)),
        out_shape=jax.ShapeDtypeStruct((m, hd), BF16),
        scratch_shapes=[pltpu.VMEM((ng, tm, LANES), F32), pltpu.VMEM((ng, tm, hd), F32)],
        compiler_params=_params(("parallel",), blocks, scratch_bytes=2 * scratch),
        name="dilated_combine",
    )(*outs, *lses, *perms)


GATE_UP_TILE = 512
GATE_UP_ROWS = 1024
DOWN_TILE = 512
DOWN_K_MAX = 6144


def _rope_tables(seq):
    half = QK_ROPE // 2
    inv = ROPE_THETA ** (-jnp.arange(half, dtype=F32) / half)
    ang = jnp.arange(seq).astype(F32)[:, None] * inv[None, :]
    cos, sin = jnp.cos(ang), jnp.sin(ang)
    t_a = jnp.concatenate([cos, sin, cos, sin], axis=1)
    t_b = jnp.concatenate([-sin, cos, sin, -cos], axis=1)
    return t_a, t_b


def _pairs(x, order):
    half = QK_ROPE // 2
    parts = (x[..., :half], x[..., half:])
    return jnp.concatenate([parts[i] for i in order], axis=-1)


def _band_buckets():
    r = jnp.arange(Q_BLOCK)[:, None]
    c = jnp.arange(2 * Q_BLOCK)[None, :]
    steps = Q_BLOCK + r - c
    out = []
    for window, dilation in DIL_GROUPS:
        band = (steps >= 0) & (steps <= window // dilation)
        out.append(jnp.where(band, _t5_causal_bucket(steps * dilation), -1))
    return jnp.stack(out).astype(jnp.int32)


def _ffn_cast_specs(ffn_wg, ffn_wu, ffn_wd, lead):
    f = ffn_wg.shape[-1]
    fp = -(-f // GATE_UP_TILE) * GATE_UP_TILE
    return [(ffn_wg, lead, None, fp), (ffn_wu, lead, None, fp), (ffn_wd, lead, fp, None)]


def _ffn(h, xn, weights, next_cast_specs=()):
    wg, wu, wd = weights
    act, next_weights = _mm_swiglu(xn, wg, wu, tm=GATE_UP_ROWS, tn=GATE_UP_TILE, ride_specs=next_cast_specs)
    return _mm_residual(act, wd, h, 0.5, tn=DOWN_TILE, tk_max=DOWN_K_MAX), next_weights


def kernel(x, ffn_norm, ffn_wg, ffn_wu, ffn_wd, attn_norm, mla_wdq, mla_q_lora_norm, mla_wuq, mla_wdkv, mla_kv_lora_norm, mla_wukv, mla_q_norm, mla_k_norm, mla_wo, kv_src_norm, w_kv_shared, k_norm_shared, rel_bias, dil_wq, dil_q_norm, dil_wo):
    bsz, seq, d = x.shape
    m = bsz * seq
    depth = ffn_norm.shape[0]
    n_a = depth // 2
    h = x.reshape(m, d)
    t_a, t_b = _rope_tables(seq)
    hd = DIL_HEADS * DIL_HEAD_DIM
    k_sh = v_sh = None
    bias_all = None

    for window, dilation in DIL_GROUPS:
        assert seq % (dilation * Q_BLOCK) == 0 and window // dilation <= Q_BLOCK
    bq, bk = min(FLASH_BQ, seq), min(FLASH_BK, seq)

    dils = tuple(dilation for _, dilation in DIL_GROUPS)
    gathered = lambda t, dil: t.reshape(bsz, dil, seq // dil, hd)

    ffn_order = [(l, p) for l in range(depth) for p in (0, 1)]
    ffn_specs = {lp: _ffn_cast_specs(ffn_wg, ffn_wu, ffn_wd, lp) for lp in ffn_order}
    ffn_weights = [_cast_bf16(*spec) for spec in ffn_specs[ffn_order[0]]]

    extra_rides = {ffn_order[0]: ("w_kv_shared", (w_kv_shared, (), None, None))}
    if depth > n_a and len(ffn_order) > 1:
        extra_rides[ffn_order[1]] = ("dil_wq", (dil_wq, (0,), None, None))
    side = {}

    def ffn(h, xn, lp):
        nxt = ffn_order.index(lp) + 1
        specs = list(ffn_specs[ffn_order[nxt]]) if nxt < len(ffn_order) else []
        if lp in extra_rides:
            specs.append(extra_rides[lp][1])
        h, cast = _ffn(h, xn, ffn_weights, specs)
        if lp in extra_rides:
            side[extra_rides[lp][0]] = cast[-1]
        return h, cast[:3]

    for l in range(depth):
        if l == n_a:
            gains = jnp.stack([kv_src_norm] * N_GROUPS + [ffn_norm[l, 0]])
            *xs, xn = _rmsnorm(h, gains, dils + (1,), bsz, seq)
            wkv = side["w_kv_shared"] if "w_kv_shared" in side else _cast_bf16(w_kv_shared)
            k_sh, v_sh = [], []
            for g, dil in enumerate(dils):
                a_g = xs[g].reshape(m, d)
                gk = jnp.tile(k_norm_shared[g], DIL_HEADS)[None, :]
                k_sh.append(gathered(_mm_headnorm(a_g, wkv, gk, DIL_HEAD_DIM, col0=g * hd), dil))
                v_sh.append(gathered(_mm_plain(a_g, wkv, BF16, n=hd, col0=(N_GROUPS + g) * hd), dil))
            bias_all = _band_bias(rel_bias, _band_buckets(), DIL_HEADS)
        else:
            (xn,) = _rmsnorm(h, ffn_norm[l, 0][None, :], (1,), bsz, seq)
        h, ffn_weights = ffn(h, xn, (l, 0))

        if l < n_a:
            (xn,) = _rmsnorm(h, attn_norm[l][None, :], (1,), bsz, seq)
            a = l
            q_lora = mla_wdq.shape[2]
            kv_lora = mla_wdkv.shape[2] - QK_ROPE
            w_rope = mla_wdkv[a][:, kv_lora:]
            wcat = jnp.concatenate([mla_wdq[a], mla_wdkv[a][:, :kv_lora], _pairs(w_rope, (0, 0, 1, 1)),
                                    _pairs(w_rope, (1, 1, 0, 0))], axis=1).astype(BF16)
            cq, ckv, k_rope = _mla_down(xn, wcat, mla_q_lora_norm[a][None, :],
                                        mla_kv_lora_norm[a][None, :], q_lora, kv_lora)
            wuq_h = mla_wuq[a].reshape(q_lora, MLA_HEADS, QK_HEAD)
            wuq_p = jnp.concatenate([wuq_h[..., :QK_NOPE], _pairs(wuq_h[..., QK_NOPE:], (0, 0, 1, 1))],
                                    axis=-1).reshape(q_lora, -1).astype(BF16)
            gq = mla_q_norm[a] * (QK_HEAD ** -0.5 * math.log2(math.e))
            q = _mla_q(cq, wuq_p, gq[None, :QK_NOPE], _pairs(gq[QK_NOPE:], (0, 0, 1, 1))[None, :], t_a, seq)
            gk = mla_k_norm[a]
            k, vt = _mla_kv(ckv, _cast_bf16(mla_wukv, (a,)), k_rope, gk[None, :QK_NOPE],
                            _pairs(gk[QK_NOPE:], (0, 0, 1, 1))[None, :], _pairs(gk[QK_NOPE:], (1, 1, 0, 0))[None, :],
                            t_a, t_b, bsz, seq, bk)
            o = _mla_flash(q.reshape(bsz, seq, -1), k.reshape(bsz, seq, -1), vt, bq, bk)
            h = _mm_residual(o.reshape(m, -1), _cast_bf16(mla_wo, (a,)), h, 1.0)
        else:
            bl = l - n_a
            xq = _rmsnorm(h, jnp.stack([attn_norm[l]] * N_GROUPS), dils, bsz, seq)
            wq = side["dil_wq"] if bl == 0 and "dil_wq" in side else _cast_bf16(dil_wq, (bl,))
            outs, lses = [], []
            for g, dil in enumerate(dils):
                gq = jnp.tile(dil_q_norm[bl][g], DIL_HEADS)[None, :]
                q_g = gathered(_mm_headnorm(xq[g].reshape(m, d), wq, gq, DIL_HEAD_DIM, col0=g * hd), dil)
                o_g, l_g = _dilated_group(q_g, k_sh[g], v_sh[g], bias_all[g])
                outs.append(o_g)
                lses.append(l_g)
            o = _combine_groups(outs, lses, seq)
            h = _mm_residual(o, _cast_bf16(dil_wo, (bl,)), h, 1.0)

        (xn,) = _rmsnorm(h, ffn_norm[l, 1][None, :], (1,), bsz, seq)
        h, ffn_weights = ffn(h, xn, (l, 1))

    return h.reshape(bsz, seq, d)
```

```python
import functools
import math

import jax
import jax.numpy as jnp
import numpy as np
from jax import lax
from jax.experimental import pallas as pl
from jax.experimental.pallas import tpu as pltpu

F32 = jnp.float32
BF16 = jnp.bfloat16

RMS_EPS = 1e-6
MLA_HEADS = 32
QK_NOPE = 128
QK_ROPE = 64
QK_HEAD = QK_NOPE + QK_ROPE
V_HEAD = 128
ROPE_THETA = 10000.0
DIL_GROUPS = ((128, 1), (512, 4), (2048, 16))
N_GROUPS = 3
DIL_HEADS = 32
DIL_HEAD_DIM = 128
NUM_BUCKETS = 32
MAX_DISTANCE = 2048
Q_BLOCK = 128

LANES = 128
QK_PAD = 2 * LANES
V7X_VMEM_BYTES = 64 * 1024 * 1024
VMEM_HEADROOM_BYTES = 8 * 1024 * 1024
MASK_VALUE = -1e30


def _nbytes(shape, dtype):
    return int(np.prod(shape)) * jnp.dtype(dtype).itemsize


def _params(semantics, block_bytes, scratch_bytes=0):
    need = 2 * block_bytes + scratch_bytes + VMEM_HEADROOM_BYTES
    limit = min(max(need, 32 * 1024 * 1024), V7X_VMEM_BYTES - 4 * 1024 * 1024)
    return pltpu.CompilerParams(dimension_semantics=semantics, vmem_limit_bytes=int(limit))


def _tile(dim, pref):
    if dim <= pref:
        return dim
    t = pref
    while dim % t:
        t //= 2
    return t


def _gather_perm(tm, dil):
    n = tm // dil
    i = np.arange(tm)
    p = np.zeros((tm, tm), np.float32)
    p[i, (i % n) * dil + i // n] = 1.0
    return jnp.asarray(p, BF16)


def _rmsnorm_kernel(x_ref, g_ref, *refs, dilations):
    n_perm = sum(dil > 1 for dil in dilations)
    p_refs, o_refs = refs[:n_perm], refs[n_perm:]
    x = x_ref[...]
    y = x * lax.rsqrt(jnp.mean(x * x, axis=-1, keepdims=True) + RMS_EPS)
    tm = x.shape[0]
    ip = 0
    for i, (o_ref, dil) in enumerate(zip(o_refs, dilations)):
        yg = (y * g_ref[i:i + 1, :]).astype(o_ref.dtype)
        if dil == 1:
            o_ref[...] = yg
        else:
            yp = _dot(p_refs[ip][...], yg).astype(o_ref.dtype)
            ip += 1
            n = tm // dil
            for r in range(dil):
                o_ref[r] = yp[r * n:(r + 1) * n, :]


def _rmsnorm(x, gains, dilations, bsz, seq, tm=256):
    m, d = x.shape
    n = gains.shape[0]
    tm = _tile(seq, tm)
    nsb = seq // tm
    out_specs, out_shape, perms = [], [], []
    for dil in dilations:
        if dil == 1:
            out_specs.append(pl.BlockSpec((tm, d), lambda i: (i, 0)))
            out_shape.append(jax.ShapeDtypeStruct((m, d), BF16))
        else:
            assert tm % (16 * dil) == 0
            perms.append(_gather_perm(tm, dil))
            out_specs.append(pl.BlockSpec((None, dil, tm // dil, d), lambda i: (i // nsb, 0, i % nsb, 0)))
            out_shape.append(jax.ShapeDtypeStruct((bsz, dil, seq // dil, d), BF16))
    blocks = (_nbytes((tm, d), F32) + n * _nbytes((tm, d), BF16) + _nbytes((n, d), F32)
              + len(perms) * _nbytes((tm, tm), BF16))
    return pl.pallas_call(
        functools.partial(_rmsnorm_kernel, dilations=tuple(dilations)),
        grid=(m // tm,),
        in_specs=[pl.BlockSpec((tm, d), lambda i: (i, 0)),
                  pl.BlockSpec((n, d), lambda i: (0, 0))]
                 + [pl.BlockSpec((tm, tm), lambda i: (0, 0)) for _ in perms],
        out_specs=out_specs,
        out_shape=out_shape,
        compiler_params=_params(("parallel",), blocks, scratch_bytes=3 * _nbytes((tm, d), F32)),
        name="rmsnorm",
    )(x, gains, *perms)


CAST_BLOCK_BYTES = 12 * 1024 * 1024


def _cast_block(x_ref, o_ref, block, rows_in, pad_rows):
    tr, cols_in = x_ref.shape
    x = x_ref[...].astype(o_ref.dtype)
    if o_ref.shape[1] > cols_in:
        o_ref[:, cols_in:] = jnp.zeros((tr, o_ref.shape[1] - cols_in), o_ref.dtype)
    if pad_rows:
        row = block * tr + lax.broadcasted_iota(jnp.int32, x.shape, 0)
        x = jnp.where(row < rows_in, x, jnp.zeros_like(x))
    o_ref[:, :cols_in] = x


def _cast_kernel(x_ref, o_ref, *, rows_in, pad_rows):
    _cast_block(x_ref, o_ref, pl.program_id(0), rows_in, pad_rows)


def _cast_bf16(w, lead=(), rows_out=None, cols_out=None):
    r, c = w.shape[-2:]
    rows_out = rows_out or r
    cols_out = cols_out or c
    tr = 8
    while tr * 2 * c * 4 <= CAST_BLOCK_BYTES and r % (tr * 2) == 0 and rows_out % (tr * 2) == 0:
        tr *= 2
    assert r % tr == 0 and rows_out % tr == 0 and c % LANES == 0
    n_in = r // tr
    squeeze = (None,) * len(lead)
    blocks = _nbytes((tr, c), F32) + _nbytes((tr, cols_out), BF16)
    return pl.pallas_call(
        functools.partial(_cast_kernel, rows_in=r, pad_rows=rows_out > r),
        grid=(rows_out // tr,),
        in_specs=[pl.BlockSpec(squeeze + (tr, c), lambda i: tuple(lead) + (jnp.minimum(i, n_in - 1), 0))],
        out_specs=pl.BlockSpec((tr, cols_out), lambda i: (i, 0)),
        out_shape=jax.ShapeDtypeStruct((rows_out, cols_out), BF16),
        compiler_params=_params(("parallel",), blocks),
        name="cast_bf16",
    )(w)


class _RidingCast:
    def __init__(self, w, lead, rows_out, cols_out, n_steps):
        self.w, self.lead = w, tuple(lead)
        self.rows_in, self.cols_in = w.shape[-2:]
        self.rows_out, self.cols_out = rows_out or self.rows_in, cols_out or self.cols_in
        tr = 16
        while (self.rows_out % tr or self.rows_in % tr or self.rows_out // tr > n_steps):
            tr += 16
        self.tr = tr
        self.n_out, self.n_in = self.rows_out // tr, self.rows_in // tr

    def specs(self, step_of):
        squeeze = (None,) * len(self.lead)
        blk = lambda *g: jnp.minimum(step_of(*g), self.n_out - 1)
        return (pl.BlockSpec(squeeze + (self.tr, self.cols_in),
                             lambda *g: self.lead + (jnp.minimum(blk(*g), self.n_in - 1), 0)),
                pl.BlockSpec((self.tr, self.cols_out), lambda *g: (blk(*g), 0)),
                jax.ShapeDtypeStruct((self.rows_out, self.cols_out), BF16))

    def block_bytes(self):
        return _nbytes((self.tr, self.cols_in), F32) + _nbytes((self.tr, self.cols_out), BF16)

    def run(self, x_ref, o_ref, step):
        _cast_block(x_ref, o_ref, jnp.minimum(step, self.n_out - 1), self.rows_in,
                    self.rows_out > self.rows_in)


def _dot(a, b):
    return jnp.dot(a, b, preferred_element_type=F32)


def _swiglu_kernel(a_ref, wg_ref, wu_ref, *refs, rides):
    nr = len(rides)
    o_ref = refs[nr]
    step = pl.program_id(0) * pl.num_programs(1) + pl.program_id(1)
    for ride, src_ref, dst_ref in zip(rides, refs[:nr], refs[nr + 1:]):
        ride.run(src_ref, dst_ref, step)
    a = a_ref[...]
    g = _dot(a, wg_ref[...])
    u = _dot(a, wu_ref[...])
    o_ref[...] = (g * jax.nn.sigmoid(g) * u).astype(o_ref.dtype)


def _mm_swiglu(a, wg, wu, tm, tn, ride_specs=()):
    m, k = a.shape
    n = wg.shape[1]
    tm = _tile(m, tm)
    assert n % tn == 0
    nj = n // tn
    rides = [_RidingCast(*spec, n_steps=(m // tm) * nj) for spec in ride_specs]
    ride_io = [ride.specs(lambda i, j: i * nj + j) for ride in rides]
    blocks = (_nbytes((tm, k), BF16) + 2 * _nbytes((k, tn), BF16) + _nbytes((tm, tn), BF16)
              + sum(ride.block_bytes() for ride in rides))
    out = pl.pallas_call(
        functools.partial(_swiglu_kernel, rides=rides),
        grid=(m // tm, nj),
        in_specs=[pl.BlockSpec((tm, k), lambda i, j: (i, 0)),
                  pl.BlockSpec((k, tn), lambda i, j: (0, j)),
                  pl.BlockSpec((k, tn), lambda i, j: (0, j))] + [io[0] for io in ride_io],
        out_specs=[pl.BlockSpec((tm, tn), lambda i, j: (i, j))] + [io[1] for io in ride_io],
        out_shape=[jax.ShapeDtypeStruct((m, n), BF16)] + [io[2] for io in ride_io],
        compiler_params=_params(("arbitrary", "arbitrary") if rides else ("parallel", "parallel"), blocks,
                                scratch_bytes=3 * _nbytes((tm, tn), F32)),
        name="ffn_gate_up",
    )(a, wg, wu, *[ride.w for ride in rides])
    return out[0], out[1:]


def _residual_kernel(a_ref, b_ref, r_ref, o_ref, *, scale):
    o_ref[...] = r_ref[...] + scale * _dot(a_ref[...], b_ref[...])


def _mm_residual(a, b, res, scale, tm=1024, tn=1024, tk_max=4096):
    m, kdim = a.shape
    n = b.shape[1]
    tm, tn = _tile(m, tm), _tile(n, tn)
    nk = 1
    while kdim // nk > tk_max or kdim % nk or (kdim // nk) % LANES:
        nk += 1
    tk = kdim // nk
    blocks = (_nbytes((tm, tk), BF16) + _nbytes((tk, tn), BF16) + 2 * _nbytes((tm, tn), F32))
    out = res
    for kc in range(nk):
        out = pl.pallas_call(
            functools.partial(_residual_kernel, scale=scale),
            grid=(m // tm, n // tn),
            in_specs=[pl.BlockSpec((tm, tk), lambda i, j, kc=kc: (i, kc)),
                      pl.BlockSpec((tk, tn), lambda i, j, kc=kc: (kc, j)),
                      pl.BlockSpec((tm, tn), lambda i, j: (i, j))],
            out_specs=pl.BlockSpec((tm, tn), lambda i, j: (i, j)),
            out_shape=jax.ShapeDtypeStruct((m, n), F32),
            compiler_params=_params(("parallel", "parallel"), blocks,
                                    scratch_bytes=_nbytes((tm, tn), F32)),
            name="matmul_residual",
        )(a, b, out)
    return out


def _plain_kernel(a_ref, b_ref, o_ref):
    o_ref[...] = _dot(a_ref[...], b_ref[...]).astype(o_ref.dtype)


def _mm_plain(a, b, out_dtype, n=None, col0=0, tm=1024, tn=1024):
    m, k = a.shape
    n = n or b.shape[1]
    tm, tn = _tile(m, tm), _tile(n, tn)
    assert col0 % tn == 0
    joff = col0 // tn
    blocks = _nbytes((tm, k), BF16) + _nbytes((k, tn), BF16) + _nbytes((tm, tn), out_dtype)
    return pl.pallas_call(
        _plain_kernel,
        grid=(m // tm, n // tn),
        in_specs=[pl.BlockSpec((tm, k), lambda i, j: (i, 0)),
                  pl.BlockSpec((k, tn), lambda i, j: (0, j + joff))],
        out_specs=pl.BlockSpec((tm, tn), lambda i, j: (i, j)),
        out_shape=jax.ShapeDtypeStruct((m, n), out_dtype),
        compiler_params=_params(("parallel", "parallel"), blocks,
                                scratch_bytes=_nbytes((tm, tn), F32)),
        name="matmul_plain",
    )(a, b)


def _headnorm_kernel(a_ref, b_ref, g_ref, o_ref, *, head_dim):
    x = _dot(a_ref[...], b_ref[...])
    tn = x.shape[1]
    for h in range(tn // head_dim):
        sl = slice(h * head_dim, (h + 1) * head_dim)
        xh = x[:, sl]
        y = xh * lax.rsqrt(jnp.mean(xh * xh, axis=-1, keepdims=True) + RMS_EPS)
        o_ref[:, sl] = (y * g_ref[:, sl]).astype(o_ref.dtype)


def _mm_headnorm(a, b, gain_row, head_dim, col0=0, tm=1024, tn=1024):
    m, k = a.shape
    n = gain_row.shape[1]
    tm, tn = _tile(m, tm), _tile(n, tn)
    assert col0 % tn == 0
    joff = col0 // tn
    blocks = (_nbytes((tm, k), BF16) + _nbytes((k, tn), BF16) + _nbytes((tm, tn), BF16)
              + _nbytes((8, tn), F32))
    return pl.pallas_call(
        functools.partial(_headnorm_kernel, head_dim=head_dim),
        grid=(m // tm, n // tn),
        in_specs=[pl.BlockSpec((tm, k), lambda i, j: (i, 0)),
                  pl.BlockSpec((k, tn), lambda i, j: (0, j + joff)),
                  pl.BlockSpec((1, tn), lambda i, j: (0, j))],
        out_specs=pl.BlockSpec((tm, tn), lambda i, j: (i, j)),
        out_shape=jax.ShapeDtypeStruct((m, n), BF16),
        compiler_params=_params(("parallel", "parallel"), blocks,
                                scratch_bytes=_nbytes((tm, tn), F32)),
        name="matmul_headnorm",
    )(a, b, gain_row)


def _mla_down_kernel(a_ref, w_ref, gq_ref, gkv_ref, cq_ref, ckv_ref, kr_ref, *, q_lora, kv_lora):
    x = _dot(a_ref[...], w_ref[...])
    xq = x[:, :q_lora]
    cq = xq * lax.rsqrt(jnp.mean(xq * xq, axis=-1, keepdims=True) + RMS_EPS) * gq_ref[...]
    cq_ref[...] = cq.astype(cq_ref.dtype)
    xkv = x[:, q_lora:q_lora + kv_lora]
    ckv = xkv * lax.rsqrt(jnp.mean(xkv * xkv, axis=-1, keepdims=True) + RMS_EPS) * gkv_ref[...]
    ckv_ref[...] = ckv.astype(ckv_ref.dtype)
    kr_ref[...] = x[:, q_lora + kv_lora:]


def _mla_down(a, wcat, gq, gkv, q_lora, kv_lora, tm=512):
    m, k = a.shape
    n = wcat.shape[1]
    nr = n - q_lora - kv_lora
    tm = _tile(m, tm)
    blocks = (_nbytes((tm, k), BF16) + _nbytes((k, n), BF16) + _nbytes((tm, q_lora), BF16)
              + _nbytes((tm, kv_lora), BF16) + _nbytes((tm, nr), F32))
    return pl.pallas_call(
        functools.partial(_mla_down_kernel, q_lora=q_lora, kv_lora=kv_lora),
        grid=(m // tm,),
        in_specs=[pl.BlockSpec((tm, k), lambda i: (i, 0)),
                  pl.BlockSpec((k, n), lambda i: (0, 0)),
                  pl.BlockSpec((1, q_lora), lambda i: (0, 0)),
                  pl.BlockSpec((1, kv_lora), lambda i: (0, 0))],
        out_specs=[pl.BlockSpec((tm, q_lora), lambda i: (i, 0)),
                   pl.BlockSpec((tm, kv_lora), lambda i: (i, 0)),
                   pl.BlockSpec((tm, nr), lambda i: (i, 0))],
        out_shape=[jax.ShapeDtypeStruct((m, q_lora), BF16),
                   jax.ShapeDtypeStruct((m, kv_lora), BF16),
                   jax.ShapeDtypeStruct((m, nr), F32)],
        compiler_params=_params(("parallel",), blocks, scratch_bytes=_nbytes((tm, n), F32)),
        name="mla_down",
    )(a, wcat, gq, gkv)


def _mla_q_kernel(a_ref, w_ref, gn_ref, gr_ref, ta_ref, o_ref):
    x = _dot(a_ref[...], w_ref[...])
    rope_scale = gr_ref[...] * ta_ref[...]
    for h in range(x.shape[1] // QK_PAD):
        xn = x[:, h * QK_PAD:h * QK_PAD + LANES]
        xr = x[:, h * QK_PAD + LANES:(h + 1) * QK_PAD]
        ms = jnp.sum(xn * xn + 0.5 * (xr * xr), axis=-1, keepdims=True) * (1.0 / QK_HEAD)
        rs = lax.rsqrt(ms + RMS_EPS)
        o_ref[:, h * QK_PAD:h * QK_PAD + LANES] = (xn * rs * gn_ref[...]).astype(o_ref.dtype)
        o_ref[:, h * QK_PAD + LANES:(h + 1) * QK_PAD] = (xr * rs * rope_scale).astype(o_ref.dtype)


def _mla_q(cq, wuq_p, g_nope, g_rope, t_a, seq, tm=1024, tn=1024):
    m, k = cq.shape
    n = wuq_p.shape[1]
    tm, tn = _tile(seq, tm), _tile(n, tn)
    nsb = seq // tm
    row_spec = pl.BlockSpec((1, LANES), lambda i, j: (0, 0))
    blocks = (_nbytes((tm, k), BF16) + _nbytes((k, tn), BF16) + _nbytes((tm, tn), BF16)
              + _nbytes((tm, LANES), F32))
    return pl.pallas_call(
        _mla_q_kernel,
        grid=(m // tm, n // tn),
        in_specs=[pl.BlockSpec((tm, k), lambda i, j: (i, 0)),
                  pl.BlockSpec((k, tn), lambda i, j: (0, j)),
                  row_spec, row_spec,
                  pl.BlockSpec((tm, LANES), lambda i, j: (i % nsb, 0))],
        out_specs=pl.BlockSpec((tm, tn), lambda i, j: (i, j)),
        out_shape=jax.ShapeDtypeStruct((m, n), BF16),
        compiler_params=_params(("parallel", "parallel"), blocks,
                                scratch_bytes=_nbytes((tm, tn), F32)),
        name="mla_q_proj",
    )(cq, wuq_p, g_nope, g_rope, t_a)


FLASH_BQ = 2048
FLASH_BK = 1024
FLASH_COL_CHUNK = 512
VT_ROWS = V_HEAD + 16


def _mla_kv_kernel(a_ref, w_ref, kr_ref, gn_ref, ga_ref, gb_ref, ta_ref, tb_ref, k_ref, vt_ref, *, bk):
    x = _dot(a_ref[...], w_ref[...])
    kra = kr_ref[:, :LANES]
    krb = kr_ref[:, LANES:]
    ss_rope = 0.5 * jnp.sum(kra * kra, axis=-1, keepdims=True)
    kr_roped = kra * (ga_ref[...] * ta_ref[...]) + krb * (gb_ref[...] * tb_ref[...])
    width = QK_NOPE + V_HEAD
    for h in range(x.shape[1] // width):
        kn = x[:, h * width:h * width + QK_NOPE]
        ms = (jnp.sum(kn * kn, axis=-1, keepdims=True) + ss_rope) * (1.0 / QK_HEAD)
        rs = lax.rsqrt(ms + RMS_EPS)
        k_ref[:, h * QK_PAD:h * QK_PAD + LANES] = (kn * rs * gn_ref[...]).astype(k_ref.dtype)
        k_ref[:, h * QK_PAD + LANES:(h + 1) * QK_PAD] = (kr_roped * rs).astype(k_ref.dtype)
        for c in range(x.shape[0] // bk):
            v = x[c * bk:(c + 1) * bk, h * width + QK_NOPE:(h + 1) * width]
            vt_ref[h, c, :V_HEAD] = v.T.astype(vt_ref.dtype)
            row = lax.broadcasted_iota(jnp.int32, (VT_ROWS - V_HEAD, bk), 0)
            vt_ref[h, c, V_HEAD:] = jnp.where(row == 0, 1.0, 0.0).astype(vt_ref.dtype)


def _mla_kv(ckv, wukv, k_rope, g_nope, g_a, g_b, t_a, t_b, bsz, seq, bk, tm=1024, tn=1024):
    m, k = ckv.shape
    n = wukv.shape[1]
    width = QK_NOPE + V_HEAD
    tm, tn = _tile(seq, tm), _tile(n, tn)
    assert tm % bk == 0
    hpt = tn // width
    nsb = seq // tm
    tab_spec = pl.BlockSpec((tm, LANES), lambda i, j: (i % nsb, 0))
    row_spec = pl.BlockSpec((1, LANES), lambda i, j: (0, 0))
    blocks = (_nbytes((tm, k), BF16) + _nbytes((k, tn), BF16) + _nbytes((tm, hpt * QK_PAD), BF16)
              + _nbytes((tm, hpt * V_HEAD), BF16) + 4 * _nbytes((tm, LANES), F32))
    return pl.pallas_call(
        functools.partial(_mla_kv_kernel, bk=bk),
        grid=(m // tm, n // tn),
        in_specs=[pl.BlockSpec((tm, k), lambda i, j: (i, 0)),
                  pl.BlockSpec((k, tn), lambda i, j: (0, j)),
                  pl.BlockSpec((tm, 2 * LANES), lambda i, j: (i, 0)),
                  row_spec, row_spec, row_spec,
                  tab_spec, tab_spec],
        out_specs=[pl.BlockSpec((tm, hpt * QK_PAD), lambda i, j: (i, j)),
                   pl.BlockSpec((None, hpt, tm // bk, VT_ROWS, bk), lambda i, j: (i // nsb, j, i % nsb, 0, 0))],
        out_shape=[jax.ShapeDtypeStruct((m, (n // width) * QK_PAD), BF16),
                   jax.ShapeDtypeStruct((bsz, n // width, seq // bk, VT_ROWS, bk), BF16)],
        compiler_params=_params(("parallel", "parallel"), blocks,
                                scratch_bytes=2 * _nbytes((tm, tn), F32)),
        name="mla_kv_proj",
    )(ckv, wukv, k_rope, g_nope, g_a, g_b, t_a, t_b)


def _dot_nt(a, b):
    return lax.dot_general(a, b, (((1,), (1,)), ((), ())), preferred_element_type=F32)


def _flash_kernel(q_ref, k_ref, vt_ref, o_ref, m_ref, acc_ref, s_ref, *, bq, bk):
    qi = pl.program_id(2)
    half = bk // 2
    m_ref[...] = jnp.full(m_ref.shape, MASK_VALUE, F32)
    acc_ref[...] = jnp.zeros(acc_ref.shape, F32)

    def scores(j, slot, c0=0):
        kb = k_ref[pl.ds(pl.multiple_of(j * bk, bk), bk), :]
        s_ref[slot, :, c0:] = _dot_nt(kb, q_ref[c0:, :])

    def softmax_pv(j, slot, r0=0, nr=bk, c0=0, diagonal=False):
        st = s_ref[slot, r0:r0 + nr, c0:]
        if diagonal:
            row = lax.broadcasted_iota(jnp.int32, (nr, nr), 0)
            col = lax.broadcasted_iota(jnp.int32, (nr, nr), 1)
            tri = jnp.where(row <= col, st[:, :nr], MASK_VALUE)
            st = tri if st.shape[1] == nr else jnp.concatenate([tri, st[:, nr:]], axis=1)
        m_old = m_ref[:, c0:]
        m_new = jnp.maximum(m_old, jnp.max(st, axis=0, keepdims=True))
        p = jnp.exp2(st - m_new)
        alpha = jnp.exp2(m_old - m_new)
        acc_ref[:, c0:] = alpha * acc_ref[:, c0:] + _dot(vt_ref[j, :, r0:r0 + nr], p.astype(BF16))
        m_ref[:, c0:] = m_new

    def softmax_pv_chunked(j, slot):
        for c0 in range(0, bq, FLASH_COL_CHUNK):
            c1 = c0 + FLASH_COL_CHUNK
            st = s_ref[slot, :, c0:c1]
            m_old = m_ref[:, c0:c1]
            m_new = jnp.maximum(m_old, jnp.max(st, axis=0, keepdims=True))
            p = jnp.exp2(st - m_new)
            alpha = jnp.exp2(m_old - m_new)
            acc_ref[:, c0:c1] = alpha * acc_ref[:, c0:c1] + _dot(vt_ref[j], p.astype(BF16))
            m_ref[:, c0:c1] = m_new

    def body(i, carry):
        scores(2 * i + 1, 1)
        softmax_pv_chunked(2 * i, 0)
        scores(2 * i + 2, 0)
        softmax_pv_chunked(2 * i + 1, 1)
        return carry

    scores(0, 0)
    lax.fori_loop(0, qi, body, 0)
    scores(2 * qi + 1, 1, c0=bk)
    for g in range(4):
        softmax_pv(2 * qi + g // 2, g // 2, r0=(g % 2) * half, nr=half, c0=g * half, diagonal=True)
    o_ref[...] = (acc_ref[:V_HEAD] / acc_ref[V_HEAD:V_HEAD + 1]).T.astype(o_ref.dtype)


def _mla_flash(q, k, vt, bq, bk):
    b, s, _ = q.shape
    h = q.shape[2] // QK_PAD
    assert s % bq == 0 and bq == 2 * bk
    blocks = (_nbytes((bq, QK_PAD), BF16) + _nbytes((s, QK_PAD), BF16) + _nbytes((s, V_HEAD), BF16)
              + _nbytes((bq, V_HEAD), BF16))
    scratch = 2 * _nbytes((8, bq), F32) + _nbytes((V_HEAD, bq), F32) + 6 * _nbytes((bk, bq), F32)
    return pl.pallas_call(
        functools.partial(_flash_kernel, bq=bq, bk=bk),
        grid=(b, h, s // bq),
        in_specs=[pl.BlockSpec((None, bq, QK_PAD), lambda bi, hi, qi: (bi, qi, hi)),
                  pl.BlockSpec((None, s, QK_PAD), lambda bi, hi, qi: (bi, 0, hi)),
                  pl.BlockSpec((None, None, s // bk, VT_ROWS, bk), lambda bi, hi, qi: (bi, hi, 0, 0, 0))],
        out_specs=pl.BlockSpec((None, bq, V_HEAD), lambda bi, hi, qi: (bi, qi, hi)),
        out_shape=jax.ShapeDtypeStruct((b, s, h * V_HEAD), BF16),
        scratch_shapes=[pltpu.VMEM((1, bq), F32),
                        pltpu.VMEM((VT_ROWS, bq), F32), pltpu.VMEM((2, bk, bq), F32)],
        compiler_params=_params(("parallel", "parallel", "arbitrary"), blocks, scratch_bytes=scratch),
        name="mla_flash_attention",
    )(q, k, vt)


def _t5_causal_bucket(dist):
    max_exact = NUM_BUCKETS // 2
    n = jnp.maximum(dist, 0)
    nf = jnp.maximum(n, 1).astype(F32)
    large = max_exact + (jnp.log(nf / max_exact) / math.log(MAX_DISTANCE / max_exact)
                         * (NUM_BUCKETS - max_exact)).astype(jnp.int32)
    large = jnp.minimum(large, NUM_BUCKETS - 1)
    return jnp.where(n < max_exact, n, large)


def _band_bias_kernel(tab_ref, bucket_ref, o_ref, *, heads):
    g = pl.program_id(0)
    h = pl.program_id(1)
    bucket = bucket_ref[...]
    acc = jnp.full(bucket.shape, MASK_VALUE, F32)
    for b in range(NUM_BUCKETS):
        acc = jnp.where(bucket == b, tab_ref[b, g * heads + h], acc)
    o_ref[...] = acc


def _band_bias(rel_bias, buckets, heads):
    g = buckets.shape[0]
    blk = buckets.shape[1:]
    blocks = _nbytes(blk, jnp.int32) + _nbytes(blk, F32)
    return pl.pallas_call(
        functools.partial(_band_bias_kernel, heads=heads),
        grid=(g, heads),
        in_specs=[pl.BlockSpec(memory_space=pltpu.SMEM),
                  pl.BlockSpec((None,) + blk, lambda gi, hi: (gi, 0, 0))],
        out_specs=pl.BlockSpec((None, None) + blk, lambda gi, hi: (gi, hi, 0, 0)),
        out_shape=jax.ShapeDtypeStruct((g, heads) + blk, F32),
        compiler_params=_params(("parallel", "parallel"), blocks),
        name="dilated_band_bias",
    )(rel_bias, buckets)


DIL_HEAD_GROUP = 4


def _dilated_kernel(q_ref, kp_ref, kc_ref, vp_ref, vc_ref, bias_ref, o_ref, lse_ref, s_ref, *, scale, heads):
    n = pl.program_id(2)
    has_prev = n > 0
    d = DIL_HEAD_DIM
    hg = s_ref.shape[1]
    lane = lax.broadcasted_iota(jnp.int32, (Q_BLOCK, LANES), 1)
    col = lax.broadcasted_iota(jnp.int32, (Q_BLOCK, 2 * Q_BLOCK), 1)
    keep = jnp.logical_or(has_prev, col >= Q_BLOCK)

    def scores(grp, slot):
        for i in range(hg):
            sl = slice((grp * hg + i) * d, (grp * hg + i + 1) * d)
            kcat = jnp.concatenate([kp_ref[:, sl], kc_ref[:, sl]], axis=0)
            s_ref[slot, i] = _dot_nt(q_ref[:, sl], kcat)

    def softmax_pv(grp, slot, lse_tile):
        for i in range(hg):
            h = grp * hg + i
            sl = slice(h * d, (h + 1) * d)
            s = jnp.where(keep, s_ref[slot, i] * scale + bias_ref[h], MASK_VALUE)
            m = jnp.max(s, axis=-1, keepdims=True)
            p = jnp.exp(s - m)
            den = jnp.sum(p, axis=-1, keepdims=True)
            vcat = jnp.concatenate([vp_ref[:, sl], vc_ref[:, sl]], axis=0)
            o_ref[:, sl] = _dot((p / den).astype(BF16), vcat).astype(o_ref.dtype)
            lse_tile = jnp.where(lane == h, m + jnp.log(den), lse_tile)
        return lse_tile

    lse_tile = jnp.zeros((Q_BLOCK, LANES), F32)
    n_grp = heads // hg
    scores(0, 0)
    for grp in range(n_grp):
        if grp + 1 < n_grp:
            scores(grp + 1, (grp + 1) % 2)
        lse_tile = softmax_pv(grp, grp % 2, lse_tile)
    lse_ref[...] = lse_tile


def _dilated_group(q, k, v, bias_g):
    b, dilation, length, hd = q.shape
    hg = math.gcd(DIL_HEAD_GROUP, DIL_HEADS)
    nb = length // Q_BLOCK
    cur = lambda bi, r, n: (bi, r, n, 0)
    prev = lambda bi, r, n: (bi, r, jnp.maximum(n - 1, 0), 0)
    blk = (None, None, Q_BLOCK, hd)
    blocks = 6 * _nbytes((Q_BLOCK, hd), BF16) + _nbytes(bias_g.shape, F32) + _nbytes((Q_BLOCK, LANES), F32)
    return pl.pallas_call(
        functools.partial(_dilated_kernel, scale=DIL_HEAD_DIM ** -0.5, heads=DIL_HEADS),
        grid=(b, dilation, nb),
        in_specs=[pl.BlockSpec(blk, cur),
                  pl.BlockSpec(blk, prev), pl.BlockSpec(blk, cur),
                  pl.BlockSpec(blk, prev), pl.BlockSpec(blk, cur),
                  pl.BlockSpec(bias_g.shape, lambda bi, r, n: (0, 0, 0))],
        out_specs=[pl.BlockSpec(blk, cur),
                   pl.BlockSpec((None, None, Q_BLOCK, LANES), cur)],
        out_shape=[jax.ShapeDtypeStruct((b, dilation, length, hd), BF16),
                   jax.ShapeDtypeStruct((b, dilation, length, LANES), F32)],
        scratch_shapes=[pltpu.VMEM((2, hg, Q_BLOCK, 2 * Q_BLOCK), F32)],
        compiler_params=_params(("parallel", "parallel", "arbitrary"), blocks,
                                scratch_bytes=2 * hg * _nbytes((Q_BLOCK, 2 * Q_BLOCK), F32)),
        name=f"dilated_attention_d{dilation}",
    )(q, k, k, v, v, bias_g)


def _combine_kernel(*refs, heads, dilations):
    ng = len(dilations)
    n_perm = sum(dil > 1 for dil in dilations)
    o_refs, l_refs = refs[:ng], refs[ng:2 * ng]
    p_refs = refs[2 * ng:2 * ng + n_perm]
    o_ref, lse_scr, o_scr = refs[2 * ng + n_perm:]
    tm = o_ref.shape[0]
    ip = 0
    for g, dil in enumerate(dilations):
        for r in range(dil):
            rows = pl.ds(r, tm // dil, stride=dil) if dil > 1 else slice(None)
            lse_scr[g, rows, :] = l_refs[g][r]
        og = o_refs[g][...].reshape(tm, o_ref.shape[1])
        if dil > 1:
            o_scr[g] = _dot(p_refs[ip][...], og)
            ip += 1
        else:
            o_scr[g] = og.astype(F32)
    lse = [lse_scr[g] for g in range(ng)]
    m = functools.reduce(jnp.maximum, lse)
    e = [jnp.exp(l - m) for l in lse]
    tot = functools.reduce(jnp.add, e)
    w = [x / tot for x in e]
    d = DIL_HEAD_DIM
    for h in range(heads):
        sl = slice(h * d, (h + 1) * d)
        o = functools.reduce(jnp.add, [w[g][:, h:h + 1] * o_scr[g, :, sl] for g in range(ng)])
        o_ref[:, sl] = o.astype(o_ref.dtype)


def _combine_groups(outs, lses, seq, tm=256):
    dilations = tuple(o.shape[1] for o in outs)
    bsz, hd = outs[0].shape[0], outs[0].shape[3]
    ng = len(outs)
    tm = _tile(seq, tm)
    nsb = seq // tm
    m = bsz * seq
    idx = lambda i: (i // nsb, 0, i % nsb, 0)
    o_specs = [pl.BlockSpec((None, dil, tm // dil, hd), idx) for dil in dilations]
    l_specs = [pl.BlockSpec((None, dil, tm // dil, LANES), idx) for dil in dilations]
    for dil in dilations:
        assert tm % (16 * dil) == 0
    perms = [_gather_perm(tm, dil).T for dil in dilations if dil > 1]
    p_specs = [pl.BlockSpec((tm, tm), lambda i: (0, 0)) for _ in perms]
    blocks = ((ng + 1) * _nbytes((tm, hd), BF16) + ng * _nbytes((tm, LANES), F32)
              + len(perms) * _nbytes((tm, tm), BF16))
    scratch = ng * (_nbytes((tm, LANES), F32) + _nbytes((tm, hd), F32))
    return pl.pallas_call(
        functools.partial(_combine_kernel, heads=DIL_HEADS, dilations=dilations),
        grid=(m // tm,),
        in_specs=o_specs + l_specs + p_specs,
        out_specs=pl.BlockSpec((tm, hd), lambda i: (i, 0)),
        out_shape=jax.ShapeDtypeStruct((m, hd), BF16),
        scratch_shapes=[pltpu.VMEM((ng, tm, LANES), F32), pltpu.VMEM((ng, tm, hd), F32)],
        compiler_params=_params(("parallel",), blocks, scratch_bytes=2 * scratch),
        name="dilated_combine",
    )(*outs, *lses, *perms)


GATE_UP_TILE = 512
GATE_UP_ROWS = 1024
DOWN_TILE = 512
DOWN_K_MAX = 6144


def _rope_tables(seq):
    half = QK_ROPE // 2
    inv = ROPE_THETA ** (-jnp.arange(half, dtype=F32) / half)
    ang = jnp.arange(seq).astype(F32)[:, None] * inv[None, :]
    cos, sin = jnp.cos(ang), jnp.sin(ang)
    t_a = jnp.concatenate([cos, sin, cos, sin], axis=1)
    t_b = jnp.concatenate([-sin, cos, sin, -cos], axis=1)
    return t_a, t_b


def _pairs(x, order):
    half = QK_ROPE // 2
    parts = (x[..., :half], x[..., half:])
    return jnp.concatenate([parts[i] for i in order], axis=-1)


def _band_buckets():
    r = jnp.arange(Q_BLOCK)[:, None]
    c = jnp.arange(2 * Q_BLOCK)[None, :]
    steps = Q_BLOCK + r - c
    out = []
    for window, dilation in DIL_GROUPS:
        band = (steps >= 0) & (steps <= window // dilation)
        out.append(jnp.where(band, _t5_causal_bucket(steps * dilation), -1))
    return jnp.stack(out).astype(jnp.int32)


def _ffn_cast_specs(ffn_wg, ffn_wu, ffn_wd, lead):
    f = ffn_wg.shape[-1]
    fp = -(-f // GATE_UP_TILE) * GATE_UP_TILE
    return [(ffn_wg, lead, None, fp), (ffn_wu, lead, None, fp), (ffn_wd, lead, fp, None)]


def _ffn(h, xn, weights, next_cast_specs=()):
    wg, wu, wd = weights
    act, next_weights = _mm_swiglu(xn, wg, wu, tm=GATE_UP_ROWS, tn=GATE_UP_TILE, ride_specs=next_cast_specs)
    return _mm_residual(act, wd, h, 0.5, tn=DOWN_TILE, tk_max=DOWN_K_MAX), next_weights


def kernel(x, ffn_norm, ffn_wg, ffn_wu, ffn_wd, attn_norm, mla_wdq, mla_q_lora_norm, mla_wuq, mla_wdkv, mla_kv_lora_norm, mla_wukv, mla_q_norm, mla_k_norm, mla_wo, kv_src_norm, w_kv_shared, k_norm_shared, rel_bias, dil_wq, dil_q_norm, dil_wo):
    bsz, seq, d = x.shape
    m = bsz * seq
    depth = ffn_norm.shape[0]
    n_a = depth // 2
    h = x.reshape(m, d)
    t_a, t_b = _rope_tables(seq)
    hd = DIL_HEADS * DIL_HEAD_DIM
    k_sh = v_sh = None
    bias_all = None

    for window, dilation in DIL_GROUPS:
        assert seq % (dilation * Q_BLOCK) == 0 and window // dilation <= Q_BLOCK
    bq, bk = min(FLASH_BQ, seq), min(FLASH_BK, seq)

    dils = tuple(dilation for _, dilation in DIL_GROUPS)
    gathered = lambda t, dil: t.reshape(bsz, dil, seq // dil, hd)

    ffn_order = [(l, p) for l in range(depth) for p in (0, 1)]
    ffn_specs = {lp: _ffn_cast_specs(ffn_wg, ffn_wu, ffn_wd, lp) for lp in ffn_order}
    ffn_weights = [_cast_bf16(*spec) for spec in ffn_specs[ffn_order[0]]]

    extra_rides = {ffn_order[0]: ("w_kv_shared", (w_kv_shared, (), None, None))}
    if depth > n_a and len(ffn_order) > 1:
        extra_rides[ffn_order[1]] = ("dil_wq", (dil_wq, (0,), None, None))
    side = {}

    def ffn(h, xn, lp):
        nxt = ffn_order.index(lp) + 1
        specs = list(ffn_specs[ffn_order[nxt]]) if nxt < len(ffn_order) else []
        if lp in extra_rides:
            specs.append(extra_rides[lp][1])
        h, cast = _ffn(h, xn, ffn_weights, specs)
        if lp in extra_rides:
            side[extra_rides[lp][0]] = cast[-1]
        return h, cast[:3]

    for l in range(depth):
        if l == n_a:
            gains = jnp.stack([kv_src_norm] * N_GROUPS + [ffn_norm[l, 0]])
            *xs, xn = _rmsnorm(h, gains, dils + (1,), bsz, seq)
            wkv = side["w_kv_shared"] if "w_kv_shared" in side else _cast_bf16(w_kv_shared)
            k_sh, v_sh = [], []
            for g, dil in enumerate(dils):
                a_g = xs[g].reshape(m, d)
                gk = jnp.tile(k_norm_shared[g], DIL_HEADS)[None, :]
                k_sh.append(gathered(_mm_headnorm(a_g, wkv, gk, DIL_HEAD_DIM, col0=g * hd), dil))
                v_sh.append(gathered(_mm_plain(a_g, wkv, BF16, n=hd, col0=(N_GROUPS + g) * hd), dil))
            bias_all = _band_bias(rel_bias, _band_buckets(), DIL_HEADS)
        else:
            (xn,) = _rmsnorm(h, ffn_norm[l, 0][None, :], (1,), bsz, seq)
        h, ffn_weights = ffn(h, xn, (l, 0))

        if l < n_a:
            (xn,) = _rmsnorm(h, attn_norm[l][None, :], (1,), bsz, seq)
            a = l
            q_lora = mla_wdq.shape[2]
            kv_lora = mla_wdkv.shape[2] - QK_ROPE
            w_rope = mla_wdkv[a][:, kv_lora:]
            wcat = jnp.concatenate([mla_wdq[a], mla_wdkv[a][:, :kv_lora], _pairs(w_rope, (0, 0, 1, 1)),
                                    _pairs(w_rope, (1, 1, 0, 0))], axis=1).astype(BF16)
            cq, ckv, k_rope = _mla_down(xn, wcat, mla_q_lora_norm[a][None, :],
                                        mla_kv_lora_norm[a][None, :], q_lora, kv_lora)
            wuq_h = mla_wuq[a].reshape(q_lora, MLA_HEADS, QK_HEAD)
            wuq_p = jnp.concatenate([wuq_h[..., :QK_NOPE], _pairs(wuq_h[..., QK_NOPE:], (0, 0, 1, 1))],
                                    axis=-1).reshape(q_lora, -1).astype(BF16)
            gq = mla_q_norm[a] * (QK_HEAD ** -0.5 * math.log2(math.e))
            q = _mla_q(cq, wuq_p, gq[None, :QK_NOPE], _pairs(gq[QK_NOPE:], (0, 0, 1, 1))[None, :], t_a, seq)
            gk = mla_k_norm[a]
            k, vt = _mla_kv(ckv, _cast_bf16(mla_wukv, (a,)), k_rope, gk[None, :QK_NOPE],
                            _pairs(gk[QK_NOPE:], (0, 0, 1, 1))[None, :], _pairs(gk[QK_NOPE:], (1, 1, 0, 0))[None, :],
                            t_a, t_b, bsz, seq, bk)
            o = _mla_flash(q.reshape(bsz, seq, -1), k.reshape(bsz, seq, -1), vt, bq, bk)
            h = _mm_residual(o.reshape(m, -1), _cast_bf16(mla_wo, (a,)), h, 1.0)
        else:
            bl = l - n_a
            xq = _rmsnorm(h, jnp.stack([attn_norm[l]] * N_GROUPS), dils, bsz, seq)
            wq = side["dil_wq"] if bl == 0 and "dil_wq" in side else _cast_bf16(dil_wq, (bl,))
            outs, lses = [], []
            for g, dil in enumerate(dils):
                gq = jnp.tile(dil_q_norm[bl][g], DIL_HEADS)[None, :]
                q_g = gathered(_mm_headnorm(xq[g].reshape(m, d), wq, gq, DIL_HEAD_DIM, col0=g * hd), dil)
                o_g, l_g = _dilated_group(q_g, k_sh[g], v_sh[g], bias_all[g])
                outs.append(o_g)
                lses.append(l_g)
            o = _combine_groups(outs, lses, seq)
            h = _mm_residual(o, _cast_bf16(dil_wo, (bl,)), h, 1.0)

        (xn,) = _rmsnorm(h, ffn_norm[l, 1][None, :], (1,), bsz, seq)
        h, ffn_weights = ffn(h, xn, (l, 1))

    return h.reshape(bsz, seq, d)
```

```python
import functools
import math

import jax
import jax.numpy as jnp
import numpy as np
from jax import lax
from jax.experimental import pallas as pl
from jax.experimental.pallas import tpu as pltpu

F32 = jnp.float32
BF16 = jnp.bfloat16

RMS_EPS = 1e-6
MLA_HEADS = 32
QK_NOPE = 128
QK_ROPE = 64
QK_HEAD = QK_NOPE + QK_ROPE
V_HEAD = 128
ROPE_THETA = 10000.0
DIL_GROUPS = ((128, 1), (512, 4), (2048, 16))
N_GROUPS = 3
DIL_HEADS = 32
DIL_HEAD_DIM = 128
NUM_BUCKETS = 32
MAX_DISTANCE = 2048
Q_BLOCK = 128

LANES = 128
QK_PAD = 2 * LANES
V7X_VMEM_BYTES = 64 * 1024 * 1024
VMEM_HEADROOM_BYTES = 8 * 1024 * 1024
MASK_VALUE = -1e30


def _nbytes(shape, dtype):
    return int(np.prod(shape)) * jnp.dtype(dtype).itemsize


def _params(semantics, block_bytes, scratch_bytes=0):
    need = 2 * block_bytes + scratch_bytes + VMEM_HEADROOM_BYTES
    limit = min(max(need, 32 * 1024 * 1024), V7X_VMEM_BYTES - 4 * 1024 * 1024)
    return pltpu.CompilerParams(dimension_semantics=semantics, vmem_limit_bytes=int(limit))


def _tile(dim, pref):
    if dim <= pref:
        return dim
    t = pref
    while dim % t:
        t //= 2
    return t


def _gather_perm(tm, dil):
    n = tm // dil
    i = np.arange(tm)
    p = np.zeros((tm, tm), np.float32)
    p[i, (i % n) * dil + i // n] = 1.0
    return jnp.asarray(p, BF16)


def _rmsnorm_kernel(x_ref, g_ref, *refs, dilations):
    n_perm = sum(dil > 1 for dil in dilations)
    p_refs, o_refs = refs[:n_perm], refs[n_perm:]
    x = x_ref[...]
    y = x * lax.rsqrt(jnp.mean(x * x, axis=-1, keepdims=True) + RMS_EPS)
    tm = x.shape[0]
    ip = 0
    for i, (o_ref, dil) in enumerate(zip(o_refs, dilations)):
        yg = (y * g_ref[i:i + 1, :]).astype(o_ref.dtype)
        if dil == 1:
            o_ref[...] = yg
        else:
            yp = _dot(p_refs[ip][...], yg).astype(o_ref.dtype)
            ip += 1
            n = tm // dil
            for r in range(dil):
                o_ref[r] = yp[r * n:(r + 1) * n, :]


def _rmsnorm(x, gains, dilations, bsz, seq, tm=256):
    m, d = x.shape
    n = gains.shape[0]
    tm = _tile(seq, tm)
    nsb = seq // tm
    out_specs, out_shape, perms = [], [], []
    for dil in dilations:
        if dil == 1:
            out_specs.append(pl.BlockSpec((tm, d), lambda i: (i, 0)))
            out_shape.append(jax.ShapeDtypeStruct((m, d), BF16))
        else:
            assert tm % (16 * dil) == 0
            perms.append(_gather_perm(tm, dil))
            out_specs.append(pl.BlockSpec((None, dil, tm // dil, d), lambda i: (i // nsb, 0, i % nsb, 0)))
            out_shape.append(jax.ShapeDtypeStruct((bsz, dil, seq // dil, d), BF16))
    blocks = (_nbytes((tm, d), F32) + n * _nbytes((tm, d), BF16) + _nbytes((n, d), F32)
              + len(perms) * _nbytes((tm, tm), BF16))
    return pl.pallas_call(
        functools.partial(_rmsnorm_kernel, dilations=tuple(dilations)),
        grid=(m // tm,),
        in_specs=[pl.BlockSpec((tm, d), lambda i: (i, 0)),
                  pl.BlockSpec((n, d), lambda i: (0, 0))]
                 + [pl.BlockSpec((tm, tm), lambda i: (0, 0)) for _ in perms],
        out_specs=out_specs,
        out_shape=out_shape,
        compiler_params=_params(("parallel",), blocks, scratch_bytes=3 * _nbytes((tm, d), F32)),
        name="rmsnorm",
    )(x, gains, *perms)


CAST_BLOCK_BYTES = 12 * 1024 * 1024


def _cast_block(x_ref, o_ref, block, rows_in, pad_rows):
    tr, cols_in = x_ref.shape
    x = x_ref[...].astype(o_ref.dtype)
    if o_ref.shape[1] > cols_in:
        o_ref[:, cols_in:] = jnp.zeros((tr, o_ref.shape[1] - cols_in), o_ref.dtype)
    if pad_rows:
        row = block * tr + lax.broadcasted_iota(jnp.int32, x.shape, 0)
        x = jnp.where(row < rows_in, x, jnp.zeros_like(x))
    o_ref[:, :cols_in] = x


def _cast_kernel(x_ref, o_ref, *, rows_in, pad_rows):
    _cast_block(x_ref, o_ref, pl.program_id(0), rows_in, pad_rows)


def _cast_bf16(w, lead=(), rows_out=None, cols_out=None):
    r, c = w.shape[-2:]
    rows_out = rows_out or r
    cols_out = cols_out or c
    tr = 8
    while tr * 2 * c * 4 <= CAST_BLOCK_BYTES and r % (tr * 2) == 0 and rows_out % (tr * 2) == 0:
        tr *= 2
    assert r % tr == 0 and rows_out % tr == 0 and c % LANES == 0
    n_in = r // tr
    squeeze = (None,) * len(lead)
    blocks = _nbytes((tr, c), F32) + _nbytes((tr, cols_out), BF16)
    return pl.pallas_call(
        functools.partial(_cast_kernel, rows_in=r, pad_rows=rows_out > r),
        grid=(rows_out // tr,),
        in_specs=[pl.BlockSpec(squeeze + (tr, c), lambda i: tuple(lead) + (jnp.minimum(i, n_in - 1), 0))],
        out_specs=pl.BlockSpec((tr, cols_out), lambda i: (i, 0)),
        out_shape=jax.ShapeDtypeStruct((rows_out, cols_out), BF16),
        compiler_params=_params(("parallel",), blocks),
        name="cast_bf16",
    )(w)


class _RidingCast:
    def __init__(self, w, lead, rows_out, cols_out, n_steps):
        self.w, self.lead = w, tuple(lead)
        self.rows_in, self.cols_in = w.shape[-2:]
        self.rows_out, self.cols_out = rows_out or self.rows_in, cols_out or self.cols_in
        tr = 16
        while (self.rows_out % tr or self.rows_in % tr or self.rows_out // tr > n_steps):
            tr += 16
        self.tr = tr
        self.n_out, self.n_in = self.rows_out // tr, self.rows_in // tr

    def specs(self, step_of):
        squeeze = (None,) * len(self.lead)
        blk = lambda *g: jnp.minimum(step_of(*g), self.n_out - 1)
        return (pl.BlockSpec(squeeze + (self.tr, self.cols_in),
                             lambda *g: self.lead + (jnp.minimum(blk(*g), self.n_in - 1), 0)),
                pl.BlockSpec((self.tr, self.cols_out), lambda *g: (blk(*g), 0)),
                jax.ShapeDtypeStruct((self.rows_out, self.cols_out), BF16))

    def block_bytes(self):
        return _nbytes((self.tr, self.cols_in), F32) + _nbytes((self.tr, self.cols_out), BF16)

    def run(self, x_ref, o_ref, step):
        _cast_block(x_ref, o_ref, jnp.minimum(step, self.n_out - 1), self.rows_in,
                    self.rows_out > self.rows_in)


def _dot(a, b):
    return jnp.dot(a, b, preferred_element_type=F32)


def _swiglu_kernel(a_ref, wg_ref, wu_ref, *refs, rides):
    nr = len(rides)
    o_ref = refs[nr]
    step = pl.program_id(0) * pl.num_programs(1) + pl.program_id(1)
    for ride, src_ref, dst_ref in zip(rides, refs[:nr], refs[nr + 1:]):
        ride.run(src_ref, dst_ref, step)
    a = a_ref[...]
    g = _dot(a, wg_ref[...])
    u = _dot(a, wu_ref[...])
    o_ref[...] = (g * jax.nn.sigmoid(g) * u).astype(o_ref.dtype)


def _mm_swiglu(a, wg, wu, tm, tn, ride_specs=()):
    m, k = a.shape
    n = wg.shape[1]
    tm = _tile(m, tm)
    assert n % tn == 0
    nj = n // tn
    rides = [_RidingCast(*spec, n_steps=(m // tm) * nj) for spec in ride_specs]
    ride_io = [ride.specs(lambda i, j: i * nj + j) for ride in rides]
    blocks = (_nbytes((tm, k), BF16) + 2 * _nbytes((k, tn), BF16) + _nbytes((tm, tn), BF16)
              + sum(ride.block_bytes() for ride in rides))
    out = pl.pallas_call(
        functools.partial(_swiglu_kernel, rides=rides),
        grid=(m // tm, nj),
        in_specs=[pl.BlockSpec((tm, k), lambda i, j: (i, 0)),
                  pl.BlockSpec((k, tn), lambda i, j: (0, j)),
                  pl.BlockSpec((k, tn), lambda i, j: (0, j))] + [io[0] for io in ride_io],
        out_specs=[pl.BlockSpec((tm, tn), lambda i, j: (i, j))] + [io[1] for io in ride_io],
        out_shape=[jax.ShapeDtypeStruct((m, n), BF16)] + [io[2] for io in ride_io],
        compiler_params=_params(("arbitrary", "arbitrary") if rides else ("parallel", "parallel"), blocks,
                                scratch_bytes=3 * _nbytes((tm, tn), F32)),
        name="ffn_gate_up",
    )(a, wg, wu, *[ride.w for ride in rides])
    return out[0], out[1:]


def _residual_kernel(a_ref, b_ref, r_ref, o_ref, *, scale):
    o_ref[...] = r_ref[...] + scale * _dot(a_ref[...], b_ref[...])


def _mm_residual(a, b, res, scale, tm=1024, tn=1024, tk_max=4096):
    m, kdim = a.shape
    n = b.shape[1]
    tm, tn = _tile(m, tm), _tile(n, tn)
    nk = 1
    while kdim // nk > tk_max or kdim % nk or (kdim // nk) % LANES:
        nk += 1
    tk = kdim // nk
    blocks = (_nbytes((tm, tk), BF16) + _nbytes((tk, tn), BF16) + 2 * _nbytes((tm, tn), F32))
    out = res
    for kc in range(nk):
        out = pl.pallas_call(
            functools.partial(_residual_kernel, scale=scale),
            grid=(m // tm, n // tn),
            in_specs=[pl.BlockSpec((tm, tk), lambda i, j, kc=kc: (i, kc)),
                      pl.BlockSpec((tk, tn), lambda i, j, kc=kc: (kc, j)),
                      pl.BlockSpec((tm, tn), lambda i, j: (i, j))],
            out_specs=pl.BlockSpec((tm, tn), lambda i, j: (i, j)),
            out_shape=jax.ShapeDtypeStruct((m, n), F32),
            compiler_params=_params(("parallel", "parallel"), blocks,
                                    scratch_bytes=_nbytes((tm, tn), F32)),
            name="matmul_residual",
        )(a, b, out)
    return out


def _plain_kernel(a_ref, b_ref, o_ref):
    o_ref[...] = _dot(a_ref[...], b_ref[...]).astype(o_ref.dtype)


def _mm_plain(a, b, out_dtype, n=None, col0=0, tm=1024, tn=1024):
    m, k = a.shape
    n = n or b.shape[1]
    tm, tn = _tile(m, tm), _tile(n, tn)
    assert col0 % tn == 0
    joff = col0 // tn
    blocks = _nbytes((tm, k), BF16) + _nbytes((k, tn), BF16) + _nbytes((tm, tn), out_dtype)
    return pl.pallas_call(
        _plain_kernel,
        grid=(m // tm, n // tn),
        in_specs=[pl.BlockSpec((tm, k), lambda i, j: (i, 0)),
                  pl.BlockSpec((k, tn), lambda i, j: (0, j + joff))],
        out_specs=pl.BlockSpec((tm, tn), lambda i, j: (i, j)),
        out_shape=jax.ShapeDtypeStruct((m, n), out_dtype),
        compiler_params=_params(("parallel", "parallel"), blocks,
                                scratch_bytes=_nbytes((tm, tn), F32)),
        name="matmul_plain",
    )(a, b)


def _headnorm_kernel(a_ref, b_ref, g_ref, o_ref, *, head_dim):
    x = _dot(a_ref[...], b_ref[...])
    tn = x.shape[1]
    for h in range(tn // head_dim):
        sl = slice(h * head_dim, (h + 1) * head_dim)
        xh = x[:, sl]
        y = xh * lax.rsqrt(jnp.mean(xh * xh, axis=-1, keepdims=True) + RMS_EPS)
        o_ref[:, sl] = (y * g_ref[:, sl]).astype(o_ref.dtype)


def _mm_headnorm(a, b, gain_row, head_dim, col0=0, tm=1024, tn=1024):
    m, k = a.shape
    n = gain_row.shape[1]
    tm, tn = _tile(m, tm), _tile(n, tn)
    assert col0 % tn == 0
    joff = col0 // tn
    blocks = (_nbytes((tm, k), BF16) + _nbytes((k, tn), BF16) + _nbytes((tm, tn), BF16)
              + _nbytes((8, tn), F32))
    return pl.pallas_call(
        functools.partial(_headnorm_kernel, head_dim=head_dim),
        grid=(m // tm, n // tn),
        in_specs=[pl.BlockSpec((tm, k), lambda i, j: (i, 0)),
                  pl.BlockSpec((k, tn), lambda i, j: (0, j + joff)),
                  pl.BlockSpec((1, tn), lambda i, j: (0, j))],
        out_specs=pl.BlockSpec((tm, tn), lambda i, j: (i, j)),
        out_shape=jax.ShapeDtypeStruct((m, n), BF16),
        compiler_params=_params(("parallel", "parallel"), blocks,
                                scratch_bytes=_nbytes((tm, tn), F32)),
        name="matmul_headnorm",
    )(a, b, gain_row)


def _mla_down_kernel(a_ref, w_ref, gq_ref, gkv_ref, cq_ref, ckv_ref, kr_ref, *, q_lora, kv_lora):
    x = _dot(a_ref[...], w_ref[...])
    xq = x[:, :q_lora]
    cq = xq * lax.rsqrt(jnp.mean(xq * xq, axis=-1, keepdims=True) + RMS_EPS) * gq_ref[...]
    cq_ref[...] = cq.astype(cq_ref.dtype)
    xkv = x[:, q_lora:q_lora + kv_lora]
    ckv = xkv * lax.rsqrt(jnp.mean(xkv * xkv, axis=-1, keepdims=True) + RMS_EPS) * gkv_ref[...]
    ckv_ref[...] = ckv.astype(ckv_ref.dtype)
    kr_ref[...] = x[:, q_lora + kv_lora:]


def _mla_down(a, wcat, gq, gkv, q_lora, kv_lora, tm=512):
    m, k = a.shape
    n = wcat.shape[1]
    nr = n - q_lora - kv_lora
    tm = _tile(m, tm)
    blocks = (_nbytes((tm, k), BF16) + _nbytes((k, n), BF16) + _nbytes((tm, q_lora), BF16)
              + _nbytes((tm, kv_lora), BF16) + _nbytes((tm, nr), F32))
    return pl.pallas_call(
        functools.partial(_mla_down_kernel, q_lora=q_lora, kv_lora=kv_lora),
        grid=(m // tm,),
        in_specs=[pl.BlockSpec((tm, k), lambda i: (i, 0)),
                  pl.BlockSpec((k, n), lambda i: (0, 0)),
                  pl.BlockSpec((1, q_lora), lambda i: (0, 0)),
                  pl.BlockSpec((1, kv_lora), lambda i: (0, 0))],
        out_specs=[pl.BlockSpec((tm, q_lora), lambda i: (i, 0)),
                   pl.BlockSpec((tm, kv_lora), lambda i: (i, 0)),
                   pl.BlockSpec((tm, nr), lambda i: (i, 0))],
        out_shape=[jax.ShapeDtypeStruct((m, q_lora), BF16),
                   jax.ShapeDtypeStruct((m, kv_lora), BF16),
                   jax.ShapeDtypeStruct((m, nr), F32)],
        compiler_params=_params(("parallel",), blocks, scratch_bytes=_nbytes((tm, n), F32)),
        name="mla_down",
    )(a, wcat, gq, gkv)


def _mla_q_kernel(a_ref, w_ref, gn_ref, gr_ref, ta_ref, o_ref):
    x = _dot(a_ref[...], w_ref[...])
    rope_scale = gr_ref[...] * ta_ref[...]
    for h in range(x.shape[1] // QK_PAD):
        xn = x[:, h * QK_PAD:h * QK_PAD + LANES]
        xr = x[:, h * QK_PAD + LANES:(h + 1) * QK_PAD]
        ms = jnp.sum(xn * xn + 0.5 * (xr * xr), axis=-1, keepdims=True) * (1.0 / QK_HEAD)
        rs = lax.rsqrt(ms + RMS_EPS)
        o_ref[:, h * QK_PAD:h * QK_PAD + LANES] = (xn * rs * gn_ref[...]).astype(o_ref.dtype)
        o_ref[:, h * QK_PAD + LANES:(h + 1) * QK_PAD] = (xr * rs * rope_scale).astype(o_ref.dtype)


def _mla_q(cq, wuq_p, g_nope, g_rope, t_a, seq, tm=1024, tn=1024):
    m, k = cq.shape
    n = wuq_p.shape[1]
    tm, tn = _tile(seq, tm), _tile(n, tn)
    nsb = seq // tm
    row_spec = pl.BlockSpec((1, LANES), lambda i, j: (0, 0))
    blocks = (_nbytes((tm, k), BF16) + _nbytes((k, tn), BF16) + _nbytes((tm, tn), BF16)
              + _nbytes((tm, LANES), F32))
    return pl.pallas_call(
        _mla_q_kernel,
        grid=(m // tm, n // tn),
        in_specs=[pl.BlockSpec((tm, k), lambda i, j: (i, 0)),
                  pl.BlockSpec((k, tn), lambda i, j: (0, j)),
                  row_spec, row_spec,
                  pl.BlockSpec((tm, LANES), lambda i, j: (i % nsb, 0))],
        out_specs=pl.BlockSpec((tm, tn), lambda i, j: (i, j)),
        out_shape=jax.ShapeDtypeStruct((m, n), BF16),
        compiler_params=_params(("parallel", "parallel"), blocks,
                                scratch_bytes=_nbytes((tm, tn), F32)),
        name="mla_q_proj",
    )(cq, wuq_p, g_nope, g_rope, t_a)


FLASH_BQ = 2048
FLASH_BK = 1024
FLASH_COL_CHUNK = 512


def _mla_kv_kernel(a_ref, w_ref, kr_ref, gn_ref, ga_ref, gb_ref, ta_ref, tb_ref, k_ref, vt_ref, *, bk):
    x = _dot(a_ref[...], w_ref[...])
    kra = kr_ref[:, :LANES]
    krb = kr_ref[:, LANES:]
    ss_rope = 0.5 * jnp.sum(kra * kra, axis=-1, keepdims=True)
    kr_roped = kra * (ga_ref[...] * ta_ref[...]) + krb * (gb_ref[...] * tb_ref[...])
    width = QK_NOPE + V_HEAD
    for h in range(x.shape[1] // width):
        kn = x[:, h * width:h * width + QK_NOPE]
        ms = (jnp.sum(kn * kn, axis=-1, keepdims=True) + ss_rope) * (1.0 / QK_HEAD)
        rs = lax.rsqrt(ms + RMS_EPS)
        k_ref[:, h * QK_PAD:h * QK_PAD + LANES] = (kn * rs * gn_ref[...]).astype(k_ref.dtype)
        k_ref[:, h * QK_PAD + LANES:(h + 1) * QK_PAD] = (kr_roped * rs).astype(k_ref.dtype)
        for c in range(x.shape[0] // bk):
            v = x[c * bk:(c + 1) * bk, h * width + QK_NOPE:(h + 1) * width]
            vt_ref[h, c] = v.T.astype(vt_ref.dtype)


def _mla_kv(ckv, wukv, k_rope, g_nope, g_a, g_b, t_a, t_b, bsz, seq, bk, tm=1024, tn=1024):
    m, k = ckv.shape
    n = wukv.shape[1]
    width = QK_NOPE + V_HEAD
    tm, tn = _tile(seq, tm), _tile(n, tn)
    assert tm % bk == 0
    hpt = tn // width
    nsb = seq // tm
    tab_spec = pl.BlockSpec((tm, LANES), lambda i, j: (i % nsb, 0))
    row_spec = pl.BlockSpec((1, LANES), lambda i, j: (0, 0))
    blocks = (_nbytes((tm, k), BF16) + _nbytes((k, tn), BF16) + _nbytes((tm, hpt * QK_PAD), BF16)
              + _nbytes((tm, hpt * V_HEAD), BF16) + 4 * _nbytes((tm, LANES), F32))
    return pl.pallas_call(
        functools.partial(_mla_kv_kernel, bk=bk),
        grid=(m // tm, n // tn),
        in_specs=[pl.BlockSpec((tm, k), lambda i, j: (i, 0)),
                  pl.BlockSpec((k, tn), lambda i, j: (0, j)),
                  pl.BlockSpec((tm, 2 * LANES), lambda i, j: (i, 0)),
                  row_spec, row_spec, row_spec,
                  tab_spec, tab_spec],
        out_specs=[pl.BlockSpec((tm, hpt * QK_PAD), lambda i, j: (i, j)),
                   pl.BlockSpec((None, hpt, tm // bk, V_HEAD, bk), lambda i, j: (i // nsb, j, i % nsb, 0, 0))],
        out_shape=[jax.ShapeDtypeStruct((m, (n // width) * QK_PAD), BF16),
                   jax.ShapeDtypeStruct((bsz, n // width, seq // bk, V_HEAD, bk), BF16)],
        compiler_params=_params(("parallel", "parallel"), blocks,
                                scratch_bytes=2 * _nbytes((tm, tn), F32)),
        name="mla_kv_proj",
    )(ckv, wukv, k_rope, g_nope, g_a, g_b, t_a, t_b)


def _dot_nt(a, b):
    return lax.dot_general(a, b, (((1,), (1,)), ((), ())), preferred_element_type=F32)


def _flash_kernel(q_ref, k_ref, vt_ref, o_ref, m_ref, l_ref, acc_ref, s_ref, *, bq, bk):
    qi = pl.program_id(2)
    half = bk // 2
    m_ref[...] = jnp.full(m_ref.shape, MASK_VALUE, F32)
    l_ref[...] = jnp.zeros(l_ref.shape, F32)
    acc_ref[...] = jnp.zeros(acc_ref.shape, F32)

    def scores(j, slot, c0=0):
        kb = k_ref[pl.ds(pl.multiple_of(j * bk, bk), bk), :]
        s_ref[slot, :, c0:] = _dot_nt(kb, q_ref[c0:, :])

    def softmax_pv(j, slot, r0=0, nr=bk, c0=0, diagonal=False):
        st = s_ref[slot, r0:r0 + nr, c0:]
        if diagonal:
            row = lax.broadcasted_iota(jnp.int32, (nr, nr), 0)
            col = lax.broadcasted_iota(jnp.int32, (nr, nr), 1)
            tri = jnp.where(row <= col, st[:, :nr], MASK_VALUE)
            st = tri if st.shape[1] == nr else jnp.concatenate([tri, st[:, nr:]], axis=1)
        m_old = m_ref[:, c0:]
        m_new = jnp.maximum(m_old, jnp.max(st, axis=0, keepdims=True))
        p = jnp.exp2(st - m_new)
        alpha = jnp.exp2(m_old - m_new)
        l_ref[:, c0:] = alpha * l_ref[:, c0:] + jnp.sum(p, axis=0, keepdims=True)
        acc_ref[:, c0:] = alpha * acc_ref[:, c0:] + _dot(vt_ref[j, :, r0:r0 + nr], p.astype(BF16))
        m_ref[:, c0:] = m_new

    def softmax_pv_chunked(j, slot):
        for c0 in range(0, bq, FLASH_COL_CHUNK):
            c1 = c0 + FLASH_COL_CHUNK
            st = s_ref[slot, :, c0:c1]
            m_old = m_ref[:, c0:c1]
            m_new = jnp.maximum(m_old, jnp.max(st, axis=0, keepdims=True))
            p = jnp.exp2(st - m_new)
            alpha = jnp.exp2(m_old - m_new)
            l_ref[:, c0:c1] = alpha * l_ref[:, c0:c1] + jnp.sum(p, axis=0, keepdims=True)
            acc_ref[:, c0:c1] = alpha * acc_ref[:, c0:c1] + _dot(vt_ref[j], p.astype(BF16))
            m_ref[:, c0:c1] = m_new

    def body(i, carry):
        scores(2 * i + 1, 1)
        softmax_pv_chunked(2 * i, 0)
        scores(2 * i + 2, 0)
        softmax_pv_chunked(2 * i + 1, 1)
        return carry

    scores(0, 0)
    lax.fori_loop(0, qi, body, 0)
    scores(2 * qi + 1, 1, c0=bk)
    for g in range(4):
        softmax_pv(2 * qi + g // 2, g // 2, r0=(g % 2) * half, nr=half, c0=g * half, diagonal=True)
    o_ref[...] = (acc_ref[...] / l_ref[...]).T.astype(o_ref.dtype)


def _mla_flash(q, k, vt, bq, bk):
    b, s, _ = q.shape
    h = q.shape[2] // QK_PAD
    assert s % bq == 0 and bq == 2 * bk
    blocks = (_nbytes((bq, QK_PAD), BF16) + _nbytes((s, QK_PAD), BF16) + _nbytes((s, V_HEAD), BF16)
              + _nbytes((bq, V_HEAD), BF16))
    scratch = 2 * _nbytes((8, bq), F32) + _nbytes((V_HEAD, bq), F32) + 6 * _nbytes((bk, bq), F32)
    return pl.pallas_call(
        functools.partial(_flash_kernel, bq=bq, bk=bk),
        grid=(b, h, s // bq),
        in_specs=[pl.BlockSpec((None, bq, QK_PAD), lambda bi, hi, qi: (bi, qi, hi)),
                  pl.BlockSpec((None, s, QK_PAD), lambda bi, hi, qi: (bi, 0, hi)),
                  pl.BlockSpec((None, None, s // bk, V_HEAD, bk), lambda bi, hi, qi: (bi, hi, 0, 0, 0))],
        out_specs=pl.BlockSpec((None, bq, V_HEAD), lambda bi, hi, qi: (bi, qi, hi)),
        out_shape=jax.ShapeDtypeStruct((b, s, h * V_HEAD), BF16),
        scratch_shapes=[pltpu.VMEM((1, bq), F32), pltpu.VMEM((1, bq), F32),
                        pltpu.VMEM((V_HEAD, bq), F32), pltpu.VMEM((2, bk, bq), F32)],
        compiler_params=_params(("parallel", "parallel", "arbitrary"), blocks, scratch_bytes=scratch),
        name="mla_flash_attention",
    )(q, k, vt)


def _t5_causal_bucket(dist):
    max_exact = NUM_BUCKETS // 2
    n = jnp.maximum(dist, 0)
    nf = jnp.maximum(n, 1).astype(F32)
    large = max_exact + (jnp.log(nf / max_exact) / math.log(MAX_DISTANCE / max_exact)
                         * (NUM_BUCKETS - max_exact)).astype(jnp.int32)
    large = jnp.minimum(large, NUM_BUCKETS - 1)
    return jnp.where(n < max_exact, n, large)


def _band_bias_kernel(tab_ref, bucket_ref, o_ref, *, heads):
    g = pl.program_id(0)
    h = pl.program_id(1)
    bucket = bucket_ref[...]
    acc = jnp.full(bucket.shape, MASK_VALUE, F32)
    for b in range(NUM_BUCKETS):
        acc = jnp.where(bucket == b, tab_ref[b, g * heads + h], acc)
    o_ref[...] = acc


def _band_bias(rel_bias, buckets, heads):
    g = buckets.shape[0]
    blk = buckets.shape[1:]
    blocks = _nbytes(blk, jnp.int32) + _nbytes(blk, F32)
    return pl.pallas_call(
        functools.partial(_band_bias_kernel, heads=heads),
        grid=(g, heads),
        in_specs=[pl.BlockSpec(memory_space=pltpu.SMEM),
                  pl.BlockSpec((None,) + blk, lambda gi, hi: (gi, 0, 0))],
        out_specs=pl.BlockSpec((None, None) + blk, lambda gi, hi: (gi, hi, 0, 0)),
        out_shape=jax.ShapeDtypeStruct((g, heads) + blk, F32),
        compiler_params=_params(("parallel", "parallel"), blocks),
        name="dilated_band_bias",
    )(rel_bias, buckets)


DIL_HEAD_GROUP = 4


def _dilated_kernel(q_ref, kp_ref, kc_ref, vp_ref, vc_ref, bias_ref, o_ref, lse_ref, s_ref, *, scale, heads):
    n = pl.program_id(2)
    has_prev = n > 0
    d = DIL_HEAD_DIM
    hg = s_ref.shape[1]
    lane = lax.broadcasted_iota(jnp.int32, (Q_BLOCK, LANES), 1)
    col = lax.broadcasted_iota(jnp.int32, (Q_BLOCK, 2 * Q_BLOCK), 1)
    keep = jnp.logical_or(has_prev, col >= Q_BLOCK)

    def scores(grp, slot):
        for i in range(hg):
            sl = slice((grp * hg + i) * d, (grp * hg + i + 1) * d)
            kcat = jnp.concatenate([kp_ref[:, sl], kc_ref[:, sl]], axis=0)
            s_ref[slot, i] = _dot_nt(q_ref[:, sl], kcat)

    def softmax_pv(grp, slot, lse_tile):
        for i in range(hg):
            h = grp * hg + i
            sl = slice(h * d, (h + 1) * d)
            s = jnp.where(keep, s_ref[slot, i] * scale + bias_ref[h], MASK_VALUE)
            m = jnp.max(s, axis=-1, keepdims=True)
            p = jnp.exp(s - m)
            den = jnp.sum(p, axis=-1, keepdims=True)
            vcat = jnp.concatenate([vp_ref[:, sl], vc_ref[:, sl]], axis=0)
            o_ref[:, sl] = _dot((p / den).astype(BF16), vcat).astype(o_ref.dtype)
            lse_tile = jnp.where(lane == h, m + jnp.log(den), lse_tile)
        return lse_tile

    lse_tile = jnp.zeros((Q_BLOCK, LANES), F32)
    n_grp = heads // hg
    scores(0, 0)
    for grp in range(n_grp):
        if grp + 1 < n_grp:
            scores(grp + 1, (grp + 1) % 2)
        lse_tile = softmax_pv(grp, grp % 2, lse_tile)
    lse_ref[...] = lse_tile


def _dilated_group(q, k, v, bias_g):
    b, dilation, length, hd = q.shape
    hg = math.gcd(DIL_HEAD_GROUP, DIL_HEADS)
    nb = length // Q_BLOCK
    cur = lambda bi, r, n: (bi, r, n, 0)
    prev = lambda bi, r, n: (bi, r, jnp.maximum(n - 1, 0), 0)
    blk = (None, None, Q_BLOCK, hd)
    blocks = 6 * _nbytes((Q_BLOCK, hd), BF16) + _nbytes(bias_g.shape, F32) + _nbytes((Q_BLOCK, LANES), F32)
    return pl.pallas_call(
        functools.partial(_dilated_kernel, scale=DIL_HEAD_DIM ** -0.5, heads=DIL_HEADS),
        grid=(b, dilation, nb),
        in_specs=[pl.BlockSpec(blk, cur),
                  pl.BlockSpec(blk, prev), pl.BlockSpec(blk, cur),
                  pl.BlockSpec(blk, prev), pl.BlockSpec(blk, cur),
                  pl.BlockSpec(bias_g.shape, lambda bi, r, n: (0, 0, 0))],
        out_specs=[pl.BlockSpec(blk, cur),
                   pl.BlockSpec((None, None, Q_BLOCK, LANES), cur)],
        out_shape=[jax.ShapeDtypeStruct((b, dilation, length, hd), BF16),
                   jax.ShapeDtypeStruct((b, dilation, length, LANES), F32)],
        scratch_shapes=[pltpu.VMEM((2, hg, Q_BLOCK, 2 * Q_BLOCK), F32)],
        compiler_params=_params(("parallel", "parallel", "arbitrary"), blocks,
                                scratch_bytes=2 * hg * _nbytes((Q_BLOCK, 2 * Q_BLOCK), F32)),
        name=f"dilated_attention_d{dilation}",
    )(q, k, k, v, v, bias_g)


def _combine_kernel(*refs, heads, dilations):
    ng = len(dilations)
    n_perm = sum(dil > 1 for dil in dilations)
    o_refs, l_refs = refs[:ng], refs[ng:2 * ng]
    p_refs = refs[2 * ng:2 * ng + n_perm]
    o_ref, lse_scr, o_scr = refs[2 * ng + n_perm:]
    tm = o_ref.shape[0]
    ip = 0
    for g, dil in enumerate(dilations):
        for r in range(dil):
            rows = pl.ds(r, tm // dil, stride=dil) if dil > 1 else slice(None)
            lse_scr[g, rows, :] = l_refs[g][r]
        og = o_refs[g][...].reshape(tm, o_ref.shape[1])
        if dil > 1:
            o_scr[g] = _dot(p_refs[ip][...], og)
            ip += 1
        else:
            o_scr[g] = og.astype(F32)
    lse = [lse_scr[g] for g in range(ng)]
    m = functools.reduce(jnp.maximum, lse)
    e = [jnp.exp(l - m) for l in lse]
    tot = functools.reduce(jnp.add, e)
    w = [x / tot for x in e]
    d = DIL_HEAD_DIM
    for h in range(heads):
        sl = slice(h * d, (h + 1) * d)
        o = functools.reduce(jnp.add, [w[g][:, h:h + 1] * o_scr[g, :, sl] for g in range(ng)])
        o_ref[:, sl] = o.astype(o_ref.dtype)


def _combine_groups(outs, lses, seq, tm=256):
    dilations = tuple(o.shape[1] for o in outs)
    bsz, hd = outs[0].shape[0], outs[0].shape[3]
    ng = len(outs)
    tm = _tile(seq, tm)
    nsb = seq // tm
    m = bsz * seq
    idx = lambda i: (i // nsb, 0, i % nsb, 0)
    o_specs = [pl.BlockSpec((None, dil, tm // dil, hd), idx) for dil in dilations]
    l_specs = [pl.BlockSpec((None, dil, tm // dil, LANES), idx) for dil in dilations]
    for dil in dilations:
        assert tm % (16 * dil) == 0
    perms = [_gather_perm(tm, dil).T for dil in dilations if dil > 1]
    p_specs = [pl.BlockSpec((tm, tm), lambda i: (0, 0)) for _ in perms]
    blocks = ((ng + 1) * _nbytes((tm, hd), BF16) + ng * _nbytes((tm, LANES), F32)
              + len(perms) * _nbytes((tm, tm), BF16))
    scratch = ng * (_nbytes((tm, LANES), F32) + _nbytes((tm, hd), F32))
    return pl.pallas_call(
        functools.partial(_combine_kernel, heads=DIL_HEADS, dilations=dilations),
        grid=(m // tm,),
        in_specs=o_specs + l_specs + p_specs,
        out_specs=pl.BlockSpec((tm, hd), lambda i: (i, 0)),
        out_shape=jax.ShapeDtypeStruct((m, hd), BF16),
        scratch_shapes=[pltpu.VMEM((ng, tm, LANES), F32), pltpu.VMEM((ng, tm, hd), F32)],
        compiler_params=_params(("parallel",), blocks, scratch_bytes=2 * scratch),
        name="dilated_combine",
    )(*outs, *lses, *perms)


GATE_UP_TILE = 512
GATE_UP_ROWS = 1024
DOWN_TILE = 512
DOWN_K_MAX = 6144


def _rope_tables(seq):
    half = QK_ROPE // 2
    inv = ROPE_THETA ** (-jnp.arange(half, dtype=F32) / half)
    ang = jnp.arange(seq).astype(F32)[:, None] * inv[None, :]
    cos, sin = jnp.cos(ang), jnp.sin(ang)
    t_a = jnp.concatenate([cos, sin, cos, sin], axis=1)
    t_b = jnp.concatenate([-sin, cos, sin, -cos], axis=1)
    return t_a, t_b


def _pairs(x, order):
    half = QK_ROPE // 2
    parts = (x[..., :half], x[..., half:])
    return jnp.concatenate([parts[i] for i in order], axis=-1)


def _band_buckets():
    r = jnp.arange(Q_BLOCK)[:, None]
    c = jnp.arange(2 * Q_BLOCK)[None, :]
    steps = Q_BLOCK + r - c
    out = []
    for window, dilation in DIL_GROUPS:
        band = (steps >= 0) & (steps <= window // dilation)
        out.append(jnp.where(band, _t5_causal_bucket(steps * dilation), -1))
    return jnp.stack(out).astype(jnp.int32)


def _ffn_cast_specs(ffn_wg, ffn_wu, ffn_wd, lead):
    f = ffn_wg.shape[-1]
    fp = -(-f // GATE_UP_TILE) * GATE_UP_TILE
    return [(ffn_wg, lead, None, fp), (ffn_wu, lead, None, fp), (ffn_wd, lead, fp, None)]


def _ffn(h, xn, weights, cast_specs=()):
    wg, wu, wd = weights
    act, cast = _mm_swiglu(xn, wg, wu, tm=GATE_UP_ROWS, tn=GATE_UP_TILE, ride_specs=cast_specs)
    if wd is None:
        wd = cast[-1]
    return _mm_residual(act, wd, h, 0.5, tn=DOWN_TILE, tk_max=DOWN_K_MAX), cast


def kernel(x, ffn_norm, ffn_wg, ffn_wu, ffn_wd, attn_norm, mla_wdq, mla_q_lora_norm, mla_wuq, mla_wdkv, mla_kv_lora_norm, mla_wukv, mla_q_norm, mla_k_norm, mla_wo, kv_src_norm, w_kv_shared, k_norm_shared, rel_bias, dil_wq, dil_q_norm, dil_wo):
    bsz, seq, d = x.shape
    m = bsz * seq
    depth = ffn_norm.shape[0]
    n_a = depth // 2
    h = x.reshape(m, d)
    t_a, t_b = _rope_tables(seq)
    hd = DIL_HEADS * DIL_HEAD_DIM
    k_sh = v_sh = None
    bias_all = None

    for window, dilation in DIL_GROUPS:
        assert seq % (dilation * Q_BLOCK) == 0 and window // dilation <= Q_BLOCK
    bq, bk = min(FLASH_BQ, seq), min(FLASH_BK, seq)

    dils = tuple(dilation for _, dilation in DIL_GROUPS)
    gathered = lambda t, dil: t.reshape(bsz, dil, seq // dil, hd)

    ffn_order = [(l, p) for l in range(depth) for p in (0, 1)]
    ffn_specs = {lp: _ffn_cast_specs(ffn_wg, ffn_wu, ffn_wd, lp) for lp in ffn_order}
    ffn_weights = [_cast_bf16(*spec) for spec in ffn_specs[ffn_order[0]][:2]] + [None]

    extra_rides = {ffn_order[0]: ("w_kv_shared", (w_kv_shared, (), None, None))}
    if depth > n_a and len(ffn_order) > 1:
        extra_rides[ffn_order[1]] = ("dil_wq", (dil_wq, (0,), None, None))
    side = {}

    def ffn(h, xn, lp):
        nxt = ffn_order.index(lp) + 1
        specs = list(ffn_specs[ffn_order[nxt]]) if nxt < len(ffn_order) else []
        n_next = len(specs)
        if lp in extra_rides:
            specs.append(extra_rides[lp][1])
        if ffn_weights[2] is None:
            specs.append(ffn_specs[lp][2])
        h, cast = _ffn(h, xn, ffn_weights, specs)
        if lp in extra_rides:
            side[extra_rides[lp][0]] = cast[n_next]
        return h, cast[:n_next]

    for l in range(depth):
        if l == n_a:
            gains = jnp.stack([kv_src_norm] * N_GROUPS + [ffn_norm[l, 0]])
            *xs, xn = _rmsnorm(h, gains, dils + (1,), bsz, seq)
            wkv = side["w_kv_shared"] if "w_kv_shared" in side else _cast_bf16(w_kv_shared)
            k_sh, v_sh = [], []
            for g, dil in enumerate(dils):
                a_g = xs[g].reshape(m, d)
                gk = jnp.tile(k_norm_shared[g], DIL_HEADS)[None, :]
                k_sh.append(gathered(_mm_headnorm(a_g, wkv, gk, DIL_HEAD_DIM, col0=g * hd), dil))
                v_sh.append(gathered(_mm_plain(a_g, wkv, BF16, n=hd, col0=(N_GROUPS + g) * hd), dil))
            bias_all = _band_bias(rel_bias, _band_buckets(), DIL_HEADS)
        else:
            (xn,) = _rmsnorm(h, ffn_norm[l, 0][None, :], (1,), bsz, seq)
        h, ffn_weights = ffn(h, xn, (l, 0))

        if l < n_a:
            (xn,) = _rmsnorm(h, attn_norm[l][None, :], (1,), bsz, seq)
            a = l
            q_lora = mla_wdq.shape[2]
            kv_lora = mla_wdkv.shape[2] - QK_ROPE
            w_rope = mla_wdkv[a][:, kv_lora:]
            wcat = jnp.concatenate([mla_wdq[a], mla_wdkv[a][:, :kv_lora], _pairs(w_rope, (0, 0, 1, 1)),
                                    _pairs(w_rope, (1, 1, 0, 0))], axis=1).astype(BF16)
            cq, ckv, k_rope = _mla_down(xn, wcat, mla_q_lora_norm[a][None, :],
                                        mla_kv_lora_norm[a][None, :], q_lora, kv_lora)
            wuq_h = mla_wuq[a].reshape(q_lora, MLA_HEADS, QK_HEAD)
            wuq_p = jnp.concatenate([wuq_h[..., :QK_NOPE], _pairs(wuq_h[..., QK_NOPE:], (0, 0, 1, 1))],
                                    axis=-1).reshape(q_lora, -1).astype(BF16)
            gq = mla_q_norm[a] * (QK_HEAD ** -0.5 * math.log2(math.e))
            q = _mla_q(cq, wuq_p, gq[None, :QK_NOPE], _pairs(gq[QK_NOPE:], (0, 0, 1, 1))[None, :], t_a, seq)
            gk = mla_k_norm[a]
            k, vt = _mla_kv(ckv, _cast_bf16(mla_wukv, (a,)), k_rope, gk[None, :QK_NOPE],
                            _pairs(gk[QK_NOPE:], (0, 0, 1, 1))[None, :], _pairs(gk[QK_NOPE:], (1, 1, 0, 0))[None, :],
                            t_a, t_b, bsz, seq, bk)
            o = _mla_flash(q.reshape(bsz, seq, -1), k.reshape(bsz, seq, -1), vt, bq, bk)
            h = _mm_residual(o.reshape(m, -1), _cast_bf16(mla_wo, (a,)), h, 1.0)
        else:
            bl = l - n_a
            xq = _rmsnorm(h, jnp.stack([attn_norm[l]] * N_GROUPS), dils, bsz, seq)
            wq = side["dil_wq"] if bl == 0 and "dil_wq" in side else _cast_bf16(dil_wq, (bl,))
            outs, lses = [], []
            for g, dil in enumerate(dils):
                gq = jnp.tile(dil_q_norm[bl][g], DIL_HEADS)[None, :]
                q_g = gathered(_mm_headnorm(xq[g].reshape(m, d), wq, gq, DIL_HEAD_DIM, col0=g * hd), dil)
                o_g, l_g = _dilated_group(q_g, k_sh[g], v_sh[g], bias_all[g])
                outs.append(o_g)
                lses.append(l_g)
            o = _combine_groups(outs, lses, seq)
            h = _mm_residual(o, _cast_bf16(dil_wo, (bl,)), h, 1.0)

        (xn,) = _rmsnorm(h, ffn_norm[l, 1][None, :], (1,), bsz, seq)
        h, ffn_weights = ffn(h, xn, (l, 1))

    return h.reshape(bsz, seq, d)
```

```python
import functools
import math

import jax
import jax.numpy as jnp
import numpy as np
from jax import lax
from jax.experimental import pallas as pl
from jax.experimental.pallas import tpu as pltpu

F32 = jnp.float32
BF16 = jnp.bfloat16

RMS_EPS = 1e-6
MLA_HEADS = 32
QK_NOPE = 128
QK_ROPE = 64
QK_HEAD = QK_NOPE + QK_ROPE
V_HEAD = 128
ROPE_THETA = 10000.0
DIL_GROUPS = ((128, 1), (512, 4), (2048, 16))
N_GROUPS = 3
DIL_HEADS = 32
DIL_HEAD_DIM = 128
NUM_BUCKETS = 32
MAX_DISTANCE = 2048
Q_BLOCK = 128

LANES = 128
QK_PAD = 2 * LANES
V7X_VMEM_BYTES = 64 * 1024 * 1024
VMEM_HEADROOM_BYTES = 8 * 1024 * 1024
MASK_VALUE = -1e30


def _nbytes(shape, dtype):
    return int(np.prod(shape)) * jnp.dtype(dtype).itemsize


def _params(semantics, block_bytes, scratch_bytes=0):
    need = 2 * block_bytes + scratch_bytes + VMEM_HEADROOM_BYTES
    limit = min(max(need, 32 * 1024 * 1024), V7X_VMEM_BYTES - 4 * 1024 * 1024)
    return pltpu.CompilerParams(dimension_semantics=semantics, vmem_limit_bytes=int(limit))


def _tile(dim, pref):
    if dim <= pref:
        return dim
    t = pref
    while dim % t:
        t //= 2
    return t


def _gather_perm(tm, dil):
    n = tm // dil
    i = np.arange(tm)
    p = np.zeros((tm, tm), np.float32)
    p[i, (i % n) * dil + i // n] = 1.0
    return jnp.asarray(p, BF16)


def _rmsnorm_kernel(x_ref, g_ref, *refs, dilations):
    n_perm = sum(dil > 1 for dil in dilations)
    p_refs, o_refs = refs[:n_perm], refs[n_perm:]
    x = x_ref[...]
    y = x * lax.rsqrt(jnp.mean(x * x, axis=-1, keepdims=True) + RMS_EPS)
    tm = x.shape[0]
    ip = 0
    for i, (o_ref, dil) in enumerate(zip(o_refs, dilations)):
        yg = (y * g_ref[i:i + 1, :]).astype(o_ref.dtype)
        if dil == 1:
            o_ref[...] = yg
        else:
            yp = _dot(p_refs[ip][...], yg).astype(o_ref.dtype)
            ip += 1
            n = tm // dil
            for r in range(dil):
                o_ref[r] = yp[r * n:(r + 1) * n, :]


def _rmsnorm(x, gains, dilations, bsz, seq, tm=256):
    m, d = x.shape
    n = gains.shape[0]
    tm = _tile(seq, tm)
    nsb = seq // tm
    out_specs, out_shape, perms = [], [], []
    for dil in dilations:
        if dil == 1:
            out_specs.append(pl.BlockSpec((tm, d), lambda i: (i, 0)))
            out_shape.append(jax.ShapeDtypeStruct((m, d), BF16))
        else:
            assert tm % (16 * dil) == 0
            perms.append(_gather_perm(tm, dil))
            out_specs.append(pl.BlockSpec((None, dil, tm // dil, d), lambda i: (i // nsb, 0, i % nsb, 0)))
            out_shape.append(jax.ShapeDtypeStruct((bsz, dil, seq // dil, d), BF16))
    blocks = (_nbytes((tm, d), F32) + n * _nbytes((tm, d), BF16) + _nbytes((n, d), F32)
              + len(perms) * _nbytes((tm, tm), BF16))
    return pl.pallas_call(
        functools.partial(_rmsnorm_kernel, dilations=tuple(dilations)),
        grid=(m // tm,),
        in_specs=[pl.BlockSpec((tm, d), lambda i: (i, 0)),
                  pl.BlockSpec((n, d), lambda i: (0, 0))]
                 + [pl.BlockSpec((tm, tm), lambda i: (0, 0)) for _ in perms],
        out_specs=out_specs,
        out_shape=out_shape,
        compiler_params=_params(("parallel",), blocks, scratch_bytes=3 * _nbytes((tm, d), F32)),
        name="rmsnorm",
    )(x, gains, *perms)


CAST_BLOCK_BYTES = 12 * 1024 * 1024


def _cast_block(x_ref, o_ref, block, rows_in, pad_rows):
    tr, cols_in = x_ref.shape
    x = x_ref[...].astype(o_ref.dtype)
    if o_ref.shape[1] > cols_in:
        o_ref[:, cols_in:] = jnp.zeros((tr, o_ref.shape[1] - cols_in), o_ref.dtype)
    if pad_rows:
        row = block * tr + lax.broadcasted_iota(jnp.int32, x.shape, 0)
        x = jnp.where(row < rows_in, x, jnp.zeros_like(x))
    o_ref[:, :cols_in] = x


def _cast_kernel(x_ref, o_ref, *, rows_in, pad_rows):
    _cast_block(x_ref, o_ref, pl.program_id(0), rows_in, pad_rows)


def _cast_bf16(w, lead=(), rows_out=None, cols_out=None):
    r, c = w.shape[-2:]
    rows_out = rows_out or r
    cols_out = cols_out or c
    tr = 8
    while tr * 2 * c * 4 <= CAST_BLOCK_BYTES and r % (tr * 2) == 0 and rows_out % (tr * 2) == 0:
        tr *= 2
    assert r % tr == 0 and rows_out % tr == 0 and c % LANES == 0
    n_in = r // tr
    squeeze = (None,) * len(lead)
    blocks = _nbytes((tr, c), F32) + _nbytes((tr, cols_out), BF16)
    return pl.pallas_call(
        functools.partial(_cast_kernel, rows_in=r, pad_rows=rows_out > r),
        grid=(rows_out // tr,),
        in_specs=[pl.BlockSpec(squeeze + (tr, c), lambda i: tuple(lead) + (jnp.minimum(i, n_in - 1), 0))],
        out_specs=pl.BlockSpec((tr, cols_out), lambda i: (i, 0)),
        out_shape=jax.ShapeDtypeStruct((rows_out, cols_out), BF16),
        compiler_params=_params(("parallel",), blocks),
        name="cast_bf16",
    )(w)


class _RidingCast:
    def __init__(self, w, lead, rows_out, cols_out, n_steps):
        self.w, self.lead = w, tuple(lead)
        self.rows_in, self.cols_in = w.shape[-2:]
        self.rows_out, self.cols_out = rows_out or self.rows_in, cols_out or self.cols_in
        tr = 16
        while (self.rows_out % tr or self.rows_in % tr or self.rows_out // tr > n_steps):
            tr += 16
        self.tr = tr
        self.n_out, self.n_in = self.rows_out // tr, self.rows_in // tr

    def specs(self, step_of):
        squeeze = (None,) * len(self.lead)
        blk = lambda *g: jnp.minimum(step_of(*g), self.n_out - 1)
        return (pl.BlockSpec(squeeze + (self.tr, self.cols_in),
                             lambda *g: self.lead + (jnp.minimum(blk(*g), self.n_in - 1), 0)),
                pl.BlockSpec((self.tr, self.cols_out), lambda *g: (blk(*g), 0)),
                jax.ShapeDtypeStruct((self.rows_out, self.cols_out), BF16))

    def block_bytes(self):
        return _nbytes((self.tr, self.cols_in), F32) + _nbytes((self.tr, self.cols_out), BF16)

    def run(self, x_ref, o_ref, step):
        _cast_block(x_ref, o_ref, jnp.minimum(step, self.n_out - 1), self.rows_in,
                    self.rows_out > self.rows_in)


def _dot(a, b):
    return jnp.dot(a, b, preferred_element_type=F32)


def _swiglu_kernel(a_ref, wg_ref, wu_ref, *refs, rides):
    nr = len(rides)
    o_ref = refs[nr]
    step = pl.program_id(0) * pl.num_programs(1) + pl.program_id(1)
    for ride, src_ref, dst_ref in zip(rides, refs[:nr], refs[nr + 1:]):
        ride.run(src_ref, dst_ref, step)
    a = a_ref[...]
    g = _dot(a, wg_ref[...])
    u = _dot(a, wu_ref[...])
    o_ref[...] = (g * jax.nn.sigmoid(g) * u).astype(o_ref.dtype)


def _mm_swiglu(a, wg, wu, tm, tn, ride_specs=()):
    m, k = a.shape
    n = wg.shape[1]
    tm = _tile(m, tm)
    assert n % tn == 0
    nj = n // tn
    rides = [_RidingCast(*spec, n_steps=(m // tm) * nj) for spec in ride_specs]
    ride_io = [ride.specs(lambda i, j: i * nj + j) for ride in rides]
    blocks = (_nbytes((tm, k), BF16) + 2 * _nbytes((k, tn), BF16) + _nbytes((tm, tn), BF16)
              + sum(ride.block_bytes() for ride in rides))
    out = pl.pallas_call(
        functools.partial(_swiglu_kernel, rides=rides),
        grid=(m // tm, nj),
        in_specs=[pl.BlockSpec((tm, k), lambda i, j: (i, 0)),
                  pl.BlockSpec((k, tn), lambda i, j: (0, j)),
                  pl.BlockSpec((k, tn), lambda i, j: (0, j))] + [io[0] for io in ride_io],
        out_specs=[pl.BlockSpec((tm, tn), lambda i, j: (i, j))] + [io[1] for io in ride_io],
        out_shape=[jax.ShapeDtypeStruct((m, n), BF16)] + [io[2] for io in ride_io],
        compiler_params=_params(("arbitrary", "arbitrary") if rides else ("parallel", "parallel"), blocks,
                                scratch_bytes=3 * _nbytes((tm, tn), F32)),
        name="ffn_gate_up",
    )(a, wg, wu, *[ride.w for ride in rides])
    return out[0], out[1:]


def _residual_kernel(a_ref, b_ref, r_ref, o_ref, *, scale):
    o_ref[...] = r_ref[...] + scale * _dot(a_ref[...], b_ref[...])


def _mm_residual(a, b, res, scale, tm=1024, tn=1024, tk_max=4096):
    m, kdim = a.shape
    n = b.shape[1]
    tm, tn = _tile(m, tm), _tile(n, tn)
    nk = 1
    while kdim // nk > tk_max or kdim % nk or (kdim // nk) % LANES:
        nk += 1
    tk = kdim // nk
    blocks = (_nbytes((tm, tk), BF16) + _nbytes((tk, tn), BF16) + 2 * _nbytes((tm, tn), F32))
    out = res
    for kc in range(nk):
        out = pl.pallas_call(
            functools.partial(_residual_kernel, scale=scale),
            grid=(m // tm, n // tn),
            in_specs=[pl.BlockSpec((tm, tk), lambda i, j, kc=kc: (i, kc)),
                      pl.BlockSpec((tk, tn), lambda i, j, kc=kc: (kc, j)),
                      pl.BlockSpec((tm, tn), lambda i, j: (i, j))],
            out_specs=pl.BlockSpec((tm, tn), lambda i, j: (i, j)),
            out_shape=jax.ShapeDtypeStruct((m, n), F32),
            compiler_params=_params(("parallel", "parallel"), blocks,
                                    scratch_bytes=_nbytes((tm, tn), F32)),
            name="matmul_residual",
        )(a, b, out)
    return out


def _plain_kernel(a_ref, b_ref, o_ref):
    o_ref[...] = _dot(a_ref[...], b_ref[...]).astype(o_ref.dtype)


def _mm_plain(a, b, out_dtype, n=None, col0=0, tm=1024, tn=1024):
    m, k = a.shape
    n = n or b.shape[1]
    tm, tn = _tile(m, tm), _tile(n, tn)
    assert col0 % tn == 0
    joff = col0 // tn
    blocks = _nbytes((tm, k), BF16) + _nbytes((k, tn), BF16) + _nbytes((tm, tn), out_dtype)
    return pl.pallas_call(
        _plain_kernel,
        grid=(m // tm, n // tn),
        in_specs=[pl.BlockSpec((tm, k), lambda i, j: (i, 0)),
                  pl.BlockSpec((k, tn), lambda i, j: (0, j + joff))],
        out_specs=pl.BlockSpec((tm, tn), lambda i, j: (i, j)),
        out_shape=jax.ShapeDtypeStruct((m, n), out_dtype),
        compiler_params=_params(("parallel", "parallel"), blocks,
                                scratch_bytes=_nbytes((tm, tn), F32)),
        name="matmul_plain",
    )(a, b)


def _headnorm_kernel(a_ref, b_ref, g_ref, o_ref, *, head_dim):
    x = _dot(a_ref[...], b_ref[...])
    tn = x.shape[1]
    for h in range(tn // head_dim):
        sl = slice(h * head_dim, (h + 1) * head_dim)
        xh = x[:, sl]
        y = xh * lax.rsqrt(jnp.mean(xh * xh, axis=-1, keepdims=True) + RMS_EPS)
        o_ref[:, sl] = (y * g_ref[:, sl]).astype(o_ref.dtype)


def _mm_headnorm(a, b, gain_row, head_dim, col0=0, tm=1024, tn=1024):
    m, k = a.shape
    n = gain_row.shape[1]
    tm, tn = _tile(m, tm), _tile(n, tn)
    assert col0 % tn == 0
    joff = col0 // tn
    blocks = (_nbytes((tm, k), BF16) + _nbytes((k, tn), BF16) + _nbytes((tm, tn), BF16)
              + _nbytes((8, tn), F32))
    return pl.pallas_call(
        functools.partial(_headnorm_kernel, head_dim=head_dim),
        grid=(m // tm, n // tn),
        in_specs=[pl.BlockSpec((tm, k), lambda i, j: (i, 0)),
                  pl.BlockSpec((k, tn), lambda i, j: (0, j + joff)),
                  pl.BlockSpec((1, tn), lambda i, j: (0, j))],
        out_specs=pl.BlockSpec((tm, tn), lambda i, j: (i, j)),
        out_shape=jax.ShapeDtypeStruct((m, n), BF16),
        compiler_params=_params(("parallel", "parallel"), blocks,
                                scratch_bytes=_nbytes((tm, tn), F32)),
        name="matmul_headnorm",
    )(a, b, gain_row)


def _mla_down_kernel(a_ref, w_ref, gq_ref, gkv_ref, cq_ref, ckv_ref, kr_ref, *, q_lora, kv_lora):
    x = _dot(a_ref[...], w_ref[...])
    xq = x[:, :q_lora]
    cq = xq * lax.rsqrt(jnp.mean(xq * xq, axis=-1, keepdims=True) + RMS_EPS) * gq_ref[...]
    cq_ref[...] = cq.astype(cq_ref.dtype)
    xkv = x[:, q_lora:q_lora + kv_lora]
    ckv = xkv * lax.rsqrt(jnp.mean(xkv * xkv, axis=-1, keepdims=True) + RMS_EPS) * gkv_ref[...]
    ckv_ref[...] = ckv.astype(ckv_ref.dtype)
    kr_ref[...] = x[:, q_lora + kv_lora:]


def _mla_down(a, wcat, gq, gkv, q_lora, kv_lora, tm=512):
    m, k = a.shape
    n = wcat.shape[1]
    nr = n - q_lora - kv_lora
    tm = _tile(m, tm)
    blocks = (_nbytes((tm, k), BF16) + _nbytes((k, n), BF16) + _nbytes((tm, q_lora), BF16)
              + _nbytes((tm, kv_lora), BF16) + _nbytes((tm, nr), F32))
    return pl.pallas_call(
        functools.partial(_mla_down_kernel, q_lora=q_lora, kv_lora=kv_lora),
        grid=(m // tm,),
        in_specs=[pl.BlockSpec((tm, k), lambda i: (i, 0)),
                  pl.BlockSpec((k, n), lambda i: (0, 0)),
                  pl.BlockSpec((1, q_lora), lambda i: (0, 0)),
                  pl.BlockSpec((1, kv_lora), lambda i: (0, 0))],
        out_specs=[pl.BlockSpec((tm, q_lora), lambda i: (i, 0)),
                   pl.BlockSpec((tm, kv_lora), lambda i: (i, 0)),
                   pl.BlockSpec((tm, nr), lambda i: (i, 0))],
        out_shape=[jax.ShapeDtypeStruct((m, q_lora), BF16),
                   jax.ShapeDtypeStruct((m, kv_lora), BF16),
                   jax.ShapeDtypeStruct((m, nr), F32)],
        compiler_params=_params(("parallel",), blocks, scratch_bytes=_nbytes((tm, n), F32)),
        name="mla_down",
    )(a, wcat, gq, gkv)


def _mla_q_kernel(a_ref, w_ref, gn_ref, gr_ref, ta_ref, o_ref):
    x = _dot(a_ref[...], w_ref[...])
    rope_scale = gr_ref[...] * ta_ref[...]
    for h in range(x.shape[1] // QK_PAD):
        xn = x[:, h * QK_PAD:h * QK_PAD + LANES]
        xr = x[:, h * QK_PAD + LANES:(h + 1) * QK_PAD]
        ms = jnp.sum(xn * xn + 0.5 * (xr * xr), axis=-1, keepdims=True) * (1.0 / QK_HEAD)
        rs = lax.rsqrt(ms + RMS_EPS)
        o_ref[:, h * QK_PAD:h * QK_PAD + LANES] = (xn * rs * gn_ref[...]).astype(o_ref.dtype)
        o_ref[:, h * QK_PAD + LANES:(h + 1) * QK_PAD] = (xr * rs * rope_scale).astype(o_ref.dtype)


def _mla_q(cq, wuq_p, g_nope, g_rope, t_a, seq, tm=1024, tn=2048):
    m, k = cq.shape
    n = wuq_p.shape[1]
    tm, tn = _tile(seq, tm), _tile(n, tn)
    nsb = seq // tm
    row_spec = pl.BlockSpec((1, LANES), lambda i, j: (0, 0))
    blocks = (_nbytes((tm, k), BF16) + _nbytes((k, tn), BF16) + _nbytes((tm, tn), BF16)
              + _nbytes((tm, LANES), F32))
    return pl.pallas_call(
        _mla_q_kernel,
        grid=(m // tm, n // tn),
        in_specs=[pl.BlockSpec((tm, k), lambda i, j: (i, 0)),
                  pl.BlockSpec((k, tn), lambda i, j: (0, j)),
                  row_spec, row_spec,
                  pl.BlockSpec((tm, LANES), lambda i, j: (i % nsb, 0))],
        out_specs=pl.BlockSpec((tm, tn), lambda i, j: (i, j)),
        out_shape=jax.ShapeDtypeStruct((m, n), BF16),
        compiler_params=_params(("parallel", "parallel"), blocks,
                                scratch_bytes=_nbytes((tm, tn), F32)),
        name="mla_q_proj",
    )(cq, wuq_p, g_nope, g_rope, t_a)


FLASH_BQ = 2048
FLASH_BK = 1024
FLASH_COL_CHUNK = 512


def _mla_kv_kernel(a_ref, w_ref, kr_ref, gn_ref, ga_ref, gb_ref, ta_ref, tb_ref, k_ref, vt_ref, *, bk):
    x = _dot(a_ref[...], w_ref[...])
    kra = kr_ref[:, :LANES]
    krb = kr_ref[:, LANES:]
    ss_rope = 0.5 * jnp.sum(kra * kra, axis=-1, keepdims=True)
    kr_roped = kra * (ga_ref[...] * ta_ref[...]) + krb * (gb_ref[...] * tb_ref[...])
    width = QK_NOPE + V_HEAD
    for h in range(x.shape[1] // width):
        kn = x[:, h * width:h * width + QK_NOPE]
        ms = (jnp.sum(kn * kn, axis=-1, keepdims=True) + ss_rope) * (1.0 / QK_HEAD)
        rs = lax.rsqrt(ms + RMS_EPS)
        k_ref[:, h * QK_PAD:h * QK_PAD + LANES] = (kn * rs * gn_ref[...]).astype(k_ref.dtype)
        k_ref[:, h * QK_PAD + LANES:(h + 1) * QK_PAD] = (kr_roped * rs).astype(k_ref.dtype)
        for c in range(x.shape[0] // bk):
            v = x[c * bk:(c + 1) * bk, h * width + QK_NOPE:(h + 1) * width]
            vt_ref[h, c] = v.T.astype(vt_ref.dtype)


def _mla_kv(ckv, wukv, k_rope, g_nope, g_a, g_b, t_a, t_b, bsz, seq, bk, tm=1024, tn=2048):
    m, k = ckv.shape
    n = wukv.shape[1]
    width = QK_NOPE + V_HEAD
    tm, tn = _tile(seq, tm), _tile(n, tn)
    assert tm % bk == 0
    hpt = tn // width
    nsb = seq // tm
    tab_spec = pl.BlockSpec((tm, LANES), lambda i, j: (i % nsb, 0))
    row_spec = pl.BlockSpec((1, LANES), lambda i, j: (0, 0))
    blocks = (_nbytes((tm, k), BF16) + _nbytes((k, tn), BF16) + _nbytes((tm, hpt * QK_PAD), BF16)
              + _nbytes((tm, hpt * V_HEAD), BF16) + 4 * _nbytes((tm, LANES), F32))
    return pl.pallas_call(
        functools.partial(_mla_kv_kernel, bk=bk),
        grid=(m // tm, n // tn),
        in_specs=[pl.BlockSpec((tm, k), lambda i, j: (i, 0)),
                  pl.BlockSpec((k, tn), lambda i, j: (0, j)),
                  pl.BlockSpec((tm, 2 * LANES), lambda i, j: (i, 0)),
                  row_spec, row_spec, row_spec,
                  tab_spec, tab_spec],
        out_specs=[pl.BlockSpec((tm, hpt * QK_PAD), lambda i, j: (i, j)),
                   pl.BlockSpec((None, hpt, tm // bk, V_HEAD, bk), lambda i, j: (i // nsb, j, i % nsb, 0, 0))],
        out_shape=[jax.ShapeDtypeStruct((m, (n // width) * QK_PAD), BF16),
                   jax.ShapeDtypeStruct((bsz, n // width, seq // bk, V_HEAD, bk), BF16)],
        compiler_params=_params(("parallel", "parallel"), blocks,
                                scratch_bytes=2 * _nbytes((tm, tn), F32)),
        name="mla_kv_proj",
    )(ckv, wukv, k_rope, g_nope, g_a, g_b, t_a, t_b)


def _dot_nt(a, b):
    return lax.dot_general(a, b, (((1,), (1,)), ((), ())), preferred_element_type=F32)


def _flash_kernel(q_ref, k_ref, vt_ref, o_ref, m_ref, l_ref, acc_ref, s_ref, *, bq, bk):
    qi = pl.program_id(2)
    half = bk // 2
    m_ref[...] = jnp.full(m_ref.shape, MASK_VALUE, F32)
    l_ref[...] = jnp.zeros(l_ref.shape, F32)
    acc_ref[...] = jnp.zeros(acc_ref.shape, F32)

    def scores(j, slot, c0=0):
        kb = k_ref[pl.ds(pl.multiple_of(j * bk, bk), bk), :]
        s_ref[slot, :, c0:] = _dot_nt(kb, q_ref[c0:, :])

    def softmax_pv(j, slot, r0=0, nr=bk, c0=0, diagonal=False):
        st = s_ref[slot, r0:r0 + nr, c0:]
        if diagonal:
            row = lax.broadcasted_iota(jnp.int32, (nr, nr), 0)
            col = lax.broadcasted_iota(jnp.int32, (nr, nr), 1)
            tri = jnp.where(row <= col, st[:, :nr], MASK_VALUE)
            st = tri if st.shape[1] == nr else jnp.concatenate([tri, st[:, nr:]], axis=1)
        m_old = m_ref[:, c0:]
        m_new = jnp.maximum(m_old, jnp.max(st, axis=0, keepdims=True))
        p = jnp.exp2(st - m_new)
        alpha = jnp.exp2(m_old - m_new)
        l_ref[:, c0:] = alpha * l_ref[:, c0:] + jnp.sum(p, axis=0, keepdims=True)
        acc_ref[:, c0:] = alpha * acc_ref[:, c0:] + _dot(vt_ref[j, :, r0:r0 + nr], p.astype(BF16))
        m_ref[:, c0:] = m_new

    def softmax_pv_chunked(j, slot):
        for c0 in range(0, bq, FLASH_COL_CHUNK):
            c1 = c0 + FLASH_COL_CHUNK
            st = s_ref[slot, :, c0:c1]
            m_old = m_ref[:, c0:c1]
            m_new = jnp.maximum(m_old, jnp.max(st, axis=0, keepdims=True))
            p = jnp.exp2(st - m_new)
            alpha = jnp.exp2(m_old - m_new)
            l_ref[:, c0:c1] = alpha * l_ref[:, c0:c1] + jnp.sum(p, axis=0, keepdims=True)
            acc_ref[:, c0:c1] = alpha * acc_ref[:, c0:c1] + _dot(vt_ref[j], p.astype(BF16))
            m_ref[:, c0:c1] = m_new

    def body(i, carry):
        scores(2 * i + 1, 1)
        softmax_pv_chunked(2 * i, 0)
        scores(2 * i + 2, 0)
        softmax_pv_chunked(2 * i + 1, 1)
        return carry

    scores(0, 0)
    lax.fori_loop(0, qi, body, 0)
    scores(2 * qi + 1, 1, c0=bk)
    for g in range(4):
        softmax_pv(2 * qi + g // 2, g // 2, r0=(g % 2) * half, nr=half, c0=g * half, diagonal=True)
    o_ref[...] = (acc_ref[...] / l_ref[...]).T.astype(o_ref.dtype)


def _mla_flash(q, k, vt, bq, bk):
    b, s, _ = q.shape
    h = q.shape[2] // QK_PAD
    assert s % bq == 0 and bq == 2 * bk
    blocks = (_nbytes((bq, QK_PAD), BF16) + _nbytes((s, QK_PAD), BF16) + _nbytes((s, V_HEAD), BF16)
              + _nbytes((bq, V_HEAD), BF16))
    scratch = 2 * _nbytes((8, bq), F32) + _nbytes((V_HEAD, bq), F32) + 6 * _nbytes((bk, bq), F32)
    return pl.pallas_call(
        functools.partial(_flash_kernel, bq=bq, bk=bk),
        grid=(b, h, s // bq),
        in_specs=[pl.BlockSpec((None, bq, QK_PAD), lambda bi, hi, qi: (bi, qi, hi)),
                  pl.BlockSpec((None, s, QK_PAD), lambda bi, hi, qi: (bi, 0, hi)),
                  pl.BlockSpec((None, None, s // bk, V_HEAD, bk), lambda bi, hi, qi: (bi, hi, 0, 0, 0))],
        out_specs=pl.BlockSpec((None, bq, V_HEAD), lambda bi, hi, qi: (bi, qi, hi)),
        out_shape=jax.ShapeDtypeStruct((b, s, h * V_HEAD), BF16),
        scratch_shapes=[pltpu.VMEM((1, bq), F32), pltpu.VMEM((1, bq), F32),
                        pltpu.VMEM((V_HEAD, bq), F32), pltpu.VMEM((2, bk, bq), F32)],
        compiler_params=_params(("parallel", "parallel", "arbitrary"), blocks, scratch_bytes=scratch),
        name="mla_flash_attention",
    )(q, k, vt)


def _t5_causal_bucket(dist):
    max_exact = NUM_BUCKETS // 2
    n = jnp.maximum(dist, 0)
    nf = jnp.maximum(n, 1).astype(F32)
    large = max_exact + (jnp.log(nf / max_exact) / math.log(MAX_DISTANCE / max_exact)
                         * (NUM_BUCKETS - max_exact)).astype(jnp.int32)
    large = jnp.minimum(large, NUM_BUCKETS - 1)
    return jnp.where(n < max_exact, n, large)


def _band_bias_kernel(tab_ref, bucket_ref, o_ref, *, heads):
    g = pl.program_id(0)
    h = pl.program_id(1)
    bucket = bucket_ref[...]
    acc = jnp.full(bucket.shape, MASK_VALUE, F32)
    for b in range(NUM_BUCKETS):
        acc = jnp.where(bucket == b, tab_ref[b, g * heads + h], acc)
    o_ref[...] = acc


def _band_bias(rel_bias, buckets, heads):
    g = buckets.shape[0]
    blk = buckets.shape[1:]
    blocks = _nbytes(blk, jnp.int32) + _nbytes(blk, F32)
    return pl.pallas_call(
        functools.partial(_band_bias_kernel, heads=heads),
        grid=(g, heads),
        in_specs=[pl.BlockSpec(memory_space=pltpu.SMEM),
                  pl.BlockSpec((None,) + blk, lambda gi, hi: (gi, 0, 0))],
        out_specs=pl.BlockSpec((None, None) + blk, lambda gi, hi: (gi, hi, 0, 0)),
        out_shape=jax.ShapeDtypeStruct((g, heads) + blk, F32),
        compiler_params=_params(("parallel", "parallel"), blocks),
        name="dilated_band_bias",
    )(rel_bias, buckets)


DIL_HEAD_GROUP = 4


def _dilated_kernel(q_ref, kp_ref, kc_ref, vp_ref, vc_ref, bias_ref, o_ref, lse_ref, s_ref, *, scale, heads):
    n = pl.program_id(2)
    has_prev = n > 0
    d = DIL_HEAD_DIM
    hg = s_ref.shape[1]
    lane = lax.broadcasted_iota(jnp.int32, (Q_BLOCK, LANES), 1)
    col = lax.broadcasted_iota(jnp.int32, (Q_BLOCK, 2 * Q_BLOCK), 1)
    keep = jnp.logical_or(has_prev, col >= Q_BLOCK)

    def scores(grp, slot):
        for i in range(hg):
            sl = slice((grp * hg + i) * d, (grp * hg + i + 1) * d)
            kcat = jnp.concatenate([kp_ref[:, sl], kc_ref[:, sl]], axis=0)
            s_ref[slot, i] = _dot_nt(q_ref[:, sl], kcat)

    def softmax_pv(grp, slot, lse_tile):
        for i in range(hg):
            h = grp * hg + i
            sl = slice(h * d, (h + 1) * d)
            s = jnp.where(keep, s_ref[slot, i] * scale + bias_ref[h], MASK_VALUE)
            m = jnp.max(s, axis=-1, keepdims=True)
            p = jnp.exp(s - m)
            den = jnp.sum(p, axis=-1, keepdims=True)
            vcat = jnp.concatenate([vp_ref[:, sl], vc_ref[:, sl]], axis=0)
            o_ref[:, sl] = _dot((p / den).astype(BF16), vcat).astype(o_ref.dtype)
            lse_tile = jnp.where(lane == h, m + jnp.log(den), lse_tile)
        return lse_tile

    lse_tile = jnp.zeros((Q_BLOCK, LANES), F32)
    n_grp = heads // hg
    scores(0, 0)
    for grp in range(n_grp):
        if grp + 1 < n_grp:
            scores(grp + 1, (grp + 1) % 2)
        lse_tile = softmax_pv(grp, grp % 2, lse_tile)
    lse_ref[...] = lse_tile


def _dilated_group(q, k, v, bias_g):
    b, dilation, length, hd = q.shape
    hg = math.gcd(DIL_HEAD_GROUP, DIL_HEADS)
    nb = length // Q_BLOCK
    cur = lambda bi, r, n: (bi, r, n, 0)
    prev = lambda bi, r, n: (bi, r, jnp.maximum(n - 1, 0), 0)
    blk = (None, None, Q_BLOCK, hd)
    blocks = 6 * _nbytes((Q_BLOCK, hd), BF16) + _nbytes(bias_g.shape, F32) + _nbytes((Q_BLOCK, LANES), F32)
    return pl.pallas_call(
        functools.partial(_dilated_kernel, scale=DIL_HEAD_DIM ** -0.5, heads=DIL_HEADS),
        grid=(b, dilation, nb),
        in_specs=[pl.BlockSpec(blk, cur),
                  pl.BlockSpec(blk, prev), pl.BlockSpec(blk, cur),
                  pl.BlockSpec(blk, prev), pl.BlockSpec(blk, cur),
                  pl.BlockSpec(bias_g.shape, lambda bi, r, n: (0, 0, 0))],
        out_specs=[pl.BlockSpec(blk, cur),
                   pl.BlockSpec((None, None, Q_BLOCK, LANES), cur)],
        out_shape=[jax.ShapeDtypeStruct((b, dilation, length, hd), BF16),
                   jax.ShapeDtypeStruct((b, dilation, length, LANES), F32)],
        scratch_shapes=[pltpu.VMEM((2, hg, Q_BLOCK, 2 * Q_BLOCK), F32)],
        compiler_params=_params(("parallel", "parallel", "arbitrary"), blocks,
                                scratch_bytes=2 * hg * _nbytes((Q_BLOCK, 2 * Q_BLOCK), F32)),
        name=f"dilated_attention_d{dilation}",
    )(q, k, k, v, v, bias_g)


def _combine_kernel(*refs, heads, dilations):
    ng = len(dilations)
    n_perm = sum(dil > 1 for dil in dilations)
    o_refs, l_refs = refs[:ng], refs[ng:2 * ng]
    p_refs = refs[2 * ng:2 * ng + n_perm]
    o_ref, lse_scr, o_scr = refs[2 * ng + n_perm:]
    tm = o_ref.shape[0]
    ip = 0
    for g, dil in enumerate(dilations):
        for r in range(dil):
            rows = pl.ds(r, tm // dil, stride=dil) if dil > 1 else slice(None)
            lse_scr[g, rows, :] = l_refs[g][r]
        og = o_refs[g][...].reshape(tm, o_ref.shape[1])
        if dil > 1:
            o_scr[g] = _dot(p_refs[ip][...], og)
            ip += 1
        else:
            o_scr[g] = og.astype(F32)
    lse = [lse_scr[g] for g in range(ng)]
    m = functools.reduce(jnp.maximum, lse)
    e = [jnp.exp(l - m) for l in lse]
    tot = functools.reduce(jnp.add, e)
    w = [x / tot for x in e]
    d = DIL_HEAD_DIM
    for h in range(heads):
        sl = slice(h * d, (h + 1) * d)
        o = functools.reduce(jnp.add, [w[g][:, h:h + 1] * o_scr[g, :, sl] for g in range(ng)])
        o_ref[:, sl] = o.astype(o_ref.dtype)


def _combine_groups(outs, lses, seq, tm=256):
    dilations = tuple(o.shape[1] for o in outs)
    bsz, hd = outs[0].shape[0], outs[0].shape[3]
    ng = len(outs)
    tm = _tile(seq, tm)
    nsb = seq // tm
    m = bsz * seq
    idx = lambda i: (i // nsb, 0, i % nsb, 0)
    o_specs = [pl.BlockSpec((None, dil, tm // dil, hd), idx) for dil in dilations]
    l_specs = [pl.BlockSpec((None, dil, tm // dil, LANES), idx) for dil in dilations]
    for dil in dilations:
        assert tm % (16 * dil) == 0
    perms = [_gather_perm(tm, dil).T for dil in dilations if dil > 1]
    p_specs = [pl.BlockSpec((tm, tm), lambda i: (0, 0)) for _ in perms]
    blocks = ((ng + 1) * _nbytes((tm, hd), BF16) + ng * _nbytes((tm, LANES), F32)
              + len(perms) * _nbytes((tm, tm), BF16))
    scratch = ng * (_nbytes((tm, LANES), F32) + _nbytes((tm, hd), F32))
    return pl.pallas_call(
        functools.partial(_combine_kernel, heads=DIL_HEADS, dilations=dilations),
        grid=(m // tm,),
        in_specs=o_specs + l_specs + p_specs,
        out_specs=pl.BlockSpec((tm, hd), lambda i: (i, 0)),
        out_shape=jax.ShapeDtypeStruct((m, hd), BF16),
        scratch_shapes=[pltpu.VMEM((ng, tm, LANES), F32), pltpu.VMEM((ng, tm, hd), F32)],
        compiler_params=_params(("parallel",), blocks, scratch_bytes=2 * scratch),
        name="dilated_combine",
    )(*outs, *lses, *perms)


GATE_UP_TILE = 512
GATE_UP_ROWS = 1024
DOWN_TILE = 512
DOWN_K_MAX = 6144


def _rope_tables(seq):
    half = QK_ROPE // 2
    inv = ROPE_THETA ** (-jnp.arange(half, dtype=F32) / half)
    ang = jnp.arange(seq).astype(F32)[:, None] * inv[None, :]
    cos, sin = jnp.cos(ang), jnp.sin(ang)
    t_a = jnp.concatenate([cos, sin, cos, sin], axis=1)
    t_b = jnp.concatenate([-sin, cos, sin, -cos], axis=1)
    return t_a, t_b


def _pairs(x, order):
    half = QK_ROPE // 2
    parts = (x[..., :half], x[..., half:])
    return jnp.concatenate([parts[i] for i in order], axis=-1)


def _band_buckets():
    r = jnp.arange(Q_BLOCK)[:, None]
    c = jnp.arange(2 * Q_BLOCK)[None, :]
    steps = Q_BLOCK + r - c
    out = []
    for window, dilation in DIL_GROUPS:
        band = (steps >= 0) & (steps <= window // dilation)
        out.append(jnp.where(band, _t5_causal_bucket(steps * dilation), -1))
    return jnp.stack(out).astype(jnp.int32)


def _ffn_cast_specs(ffn_wg, ffn_wu, ffn_wd, lead):
    f = ffn_wg.shape[-1]
    fp = -(-f // GATE_UP_TILE) * GATE_UP_TILE
    return [(ffn_wg, lead, None, fp), (ffn_wu, lead, None, fp), (ffn_wd, lead, fp, None)]


def _ffn(h, xn, weights, cast_specs=()):
    wg, wu, wd = weights
    act, cast = _mm_swiglu(xn, wg, wu, tm=GATE_UP_ROWS, tn=GATE_UP_TILE, ride_specs=cast_specs)
    if wd is None:
        wd = cast[-1]
    return _mm_residual(act, wd, h, 0.5, tn=DOWN_TILE, tk_max=DOWN_K_MAX), cast


def kernel(x, ffn_norm, ffn_wg, ffn_wu, ffn_wd, attn_norm, mla_wdq, mla_q_lora_norm, mla_wuq, mla_wdkv, mla_kv_lora_norm, mla_wukv, mla_q_norm, mla_k_norm, mla_wo, kv_src_norm, w_kv_shared, k_norm_shared, rel_bias, dil_wq, dil_q_norm, dil_wo):
    bsz, seq, d = x.shape
    m = bsz * seq
    depth = ffn_norm.shape[0]
    n_a = depth // 2
    h = x.reshape(m, d)
    t_a, t_b = _rope_tables(seq)
    hd = DIL_HEADS * DIL_HEAD_DIM
    k_sh = v_sh = None
    bias_all = None

    for window, dilation in DIL_GROUPS:
        assert seq % (dilation * Q_BLOCK) == 0 and window // dilation <= Q_BLOCK
    bq, bk = min(FLASH_BQ, seq), min(FLASH_BK, seq)

    dils = tuple(dilation for _, dilation in DIL_GROUPS)
    gathered = lambda t, dil: t.reshape(bsz, dil, seq // dil, hd)

    ffn_order = [(l, p) for l in range(depth) for p in (0, 1)]
    ffn_specs = {lp: _ffn_cast_specs(ffn_wg, ffn_wu, ffn_wd, lp) for lp in ffn_order}
    ffn_weights = [_cast_bf16(*spec) for spec in ffn_specs[ffn_order[0]][:2]] + [None]

    extra_rides = {ffn_order[0]: ("w_kv_shared", (w_kv_shared, (), None, None))}
    if depth > n_a and len(ffn_order) > 1:
        extra_rides[ffn_order[1]] = ("dil_wq", (dil_wq, (0,), None, None))
    side = {}

    def ffn(h, xn, lp):
        nxt = ffn_order.index(lp) + 1
        specs = list(ffn_specs[ffn_order[nxt]]) if nxt < len(ffn_order) else []
        n_next = len(specs)
        if lp in extra_rides:
            specs.append(extra_rides[lp][1])
        if ffn_weights[2] is None:
            specs.append(ffn_specs[lp][2])
        h, cast = _ffn(h, xn, ffn_weights, specs)
        if lp in extra_rides:
            side[extra_rides[lp][0]] = cast[n_next]
        return h, cast[:n_next]

    for l in range(depth):
        if l == n_a:
            gains = jnp.stack([kv_src_norm] * N_GROUPS + [ffn_norm[l, 0]])
            *xs, xn = _rmsnorm(h, gains, dils + (1,), bsz, seq)
            wkv = side["w_kv_shared"] if "w_kv_shared" in side else _cast_bf16(w_kv_shared)
            k_sh, v_sh = [], []
            for g, dil in enumerate(dils):
                a_g = xs[g].reshape(m, d)
                gk = jnp.tile(k_norm_shared[g], DIL_HEADS)[None, :]
                k_sh.append(gathered(_mm_headnorm(a_g, wkv, gk, DIL_HEAD_DIM, col0=g * hd), dil))
                v_sh.append(gathered(_mm_plain(a_g, wkv, BF16, n=hd, col0=(N_GROUPS + g) * hd), dil))
            bias_all = _band_bias(rel_bias, _band_buckets(), DIL_HEADS)
        else:
            (xn,) = _rmsnorm(h, ffn_norm[l, 0][None, :], (1,), bsz, seq)
        h, ffn_weights = ffn(h, xn, (l, 0))

        if l < n_a:
            (xn,) = _rmsnorm(h, attn_norm[l][None, :], (1,), bsz, seq)
            a = l
            q_lora = mla_wdq.shape[2]
            kv_lora = mla_wdkv.shape[2] - QK_ROPE
            w_rope = mla_wdkv[a][:, kv_lora:]
            wcat = jnp.concatenate([mla_wdq[a], mla_wdkv[a][:, :kv_lora], _pairs(w_rope, (0, 0, 1, 1)),
                                    _pairs(w_rope, (1, 1, 0, 0))], axis=1).astype(BF16)
            cq, ckv, k_rope = _mla_down(xn, wcat, mla_q_lora_norm[a][None, :],
                                        mla_kv_lora_norm[a][None, :], q_lora, kv_lora)
            wuq_h = mla_wuq[a].reshape(q_lora, MLA_HEADS, QK_HEAD)
            wuq_p = jnp.concatenate([wuq_h[..., :QK_NOPE], _pairs(wuq_h[..., QK_NOPE:], (0, 0, 1, 1))],
                                    axis=-1).reshape(q_lora, -1).astype(BF16)
            gq = mla_q_norm[a] * (QK_HEAD ** -0.5 * math.log2(math.e))
            q = _mla_q(cq, wuq_p, gq[None, :QK_NOPE], _pairs(gq[QK_NOPE:], (0, 0, 1, 1))[None, :], t_a, seq)
            gk = mla_k_norm[a]
            k, vt = _mla_kv(ckv, _cast_bf16(mla_wukv, (a,)), k_rope, gk[None, :QK_NOPE],
                            _pairs(gk[QK_NOPE:], (0, 0, 1, 1))[None, :], _pairs(gk[QK_NOPE:], (1, 1, 0, 0))[None, :],
                            t_a, t_b, bsz, seq, bk)
            o = _mla_flash(q.reshape(bsz, seq, -1), k.reshape(bsz, seq, -1), vt, bq, bk)
            h = _mm_residual(o.reshape(m, -1), _cast_bf16(mla_wo, (a,)), h, 1.0)
        else:
            bl = l - n_a
            xq = _rmsnorm(h, jnp.stack([attn_norm[l]] * N_GROUPS), dils, bsz, seq)
            wq = side["dil_wq"] if bl == 0 and "dil_wq" in side else _cast_bf16(dil_wq, (bl,))
            outs, lses = [], []
            for g, dil in enumerate(dils):
                gq = jnp.tile(dil_q_norm[bl][g], DIL_HEADS)[None, :]
                q_g = gathered(_mm_headnorm(xq[g].reshape(m, d), wq, gq, DIL_HEAD_DIM, col0=g * hd), dil)
                o_g, l_g = _dilated_group(q_g, k_sh[g], v_sh[g], bias_all[g])
                outs.append(o_g)
                lses.append(l_g)
            o = _combine_groups(outs, lses, seq)
            h = _mm_residual(o, _cast_bf16(dil_wo, (bl,)), h, 1.0)

        (xn,) = _rmsnorm(h, ffn_norm[l, 1][None, :], (1,), bsz, seq)
        h, ffn_weights = ffn(h, xn, (l, 1))

    return h.reshape(bsz, seq, d)
```

```python
import functools
import math

import jax
import jax.numpy as jnp
import numpy as np
from jax import lax
from jax.experimental import pallas as pl
from jax.experimental.pallas import tpu as pltpu

F32 = jnp.float32
BF16 = jnp.bfloat16

RMS_EPS = 1e-6
MLA_HEADS = 32
QK_NOPE = 128
QK_ROPE = 64
QK_HEAD = QK_NOPE + QK_ROPE
V_HEAD = 128
ROPE_THETA = 10000.0
DIL_GROUPS = ((128, 1), (512, 4), (2048, 16))
N_GROUPS = 3
DIL_HEADS = 32
DIL_HEAD_DIM = 128
NUM_BUCKETS = 32
MAX_DISTANCE = 2048
Q_BLOCK = 128

LANES = 128
QK_PAD = 2 * LANES
V7X_VMEM_BYTES = 64 * 1024 * 1024
VMEM_HEADROOM_BYTES = 8 * 1024 * 1024
MASK_VALUE = -1e30


def _nbytes(shape, dtype):
    return int(np.prod(shape)) * jnp.dtype(dtype).itemsize


def _params(semantics, block_bytes, scratch_bytes=0):
    need = 2 * block_bytes + scratch_bytes + VMEM_HEADROOM_BYTES
    limit = min(max(need, 32 * 1024 * 1024), V7X_VMEM_BYTES - 4 * 1024 * 1024)
    return pltpu.CompilerParams(dimension_semantics=semantics, vmem_limit_bytes=int(limit))


def _tile(dim, pref):
    if dim <= pref:
        return dim
    t = pref
    while dim % t:
        t //= 2
    return t


def _gather_perm(tm, dil):
    n = tm // dil
    i = np.arange(tm)
    p = np.zeros((tm, tm), np.float32)
    p[i, (i % n) * dil + i // n] = 1.0
    return jnp.asarray(p, BF16)


def _rmsnorm_kernel(x_ref, g_ref, *refs, dilations):
    n_perm = sum(dil > 1 for dil in dilations)
    p_refs, o_refs = refs[:n_perm], refs[n_perm:]
    x = x_ref[...]
    y = x * lax.rsqrt(jnp.mean(x * x, axis=-1, keepdims=True) + RMS_EPS)
    tm = x.shape[0]
    ip = 0
    for i, (o_ref, dil) in enumerate(zip(o_refs, dilations)):
        yg = (y * g_ref[i:i + 1, :]).astype(o_ref.dtype)
        if dil == 1:
            o_ref[...] = yg
        else:
            yp = _dot(p_refs[ip][...], yg).astype(o_ref.dtype)
            ip += 1
            n = tm // dil
            for r in range(dil):
                o_ref[r] = yp[r * n:(r + 1) * n, :]


def _rmsnorm(x, gains, dilations, bsz, seq, tm=256):
    m, d = x.shape
    n = gains.shape[0]
    tm = _tile(seq, tm if any(dil > 1 for dil in dilations) else 2 * tm)
    nsb = seq // tm
    out_specs, out_shape, perms = [], [], []
    for dil in dilations:
        if dil == 1:
            out_specs.append(pl.BlockSpec((tm, d), lambda i: (i, 0)))
            out_shape.append(jax.ShapeDtypeStruct((m, d), BF16))
        else:
            assert tm % (16 * dil) == 0
            perms.append(_gather_perm(tm, dil))
            out_specs.append(pl.BlockSpec((None, dil, tm // dil, d), lambda i: (i // nsb, 0, i % nsb, 0)))
            out_shape.append(jax.ShapeDtypeStruct((bsz, dil, seq // dil, d), BF16))
    blocks = (_nbytes((tm, d), F32) + n * _nbytes((tm, d), BF16) + _nbytes((n, d), F32)
              + len(perms) * _nbytes((tm, tm), BF16))
    return pl.pallas_call(
        functools.partial(_rmsnorm_kernel, dilations=tuple(dilations)),
        grid=(m // tm,),
        in_specs=[pl.BlockSpec((tm, d), lambda i: (i, 0)),
                  pl.BlockSpec((n, d), lambda i: (0, 0))]
                 + [pl.BlockSpec((tm, tm), lambda i: (0, 0)) for _ in perms],
        out_specs=out_specs,
        out_shape=out_shape,
        compiler_params=_params(("parallel",), blocks, scratch_bytes=3 * _nbytes((tm, d), F32)),
        name="rmsnorm",
    )(x, gains, *perms)


CAST_BLOCK_BYTES = 12 * 1024 * 1024


def _cast_block(x_ref, o_ref, block, rows_in, pad_rows):
    tr, cols_in = x_ref.shape
    x = x_ref[...].astype(o_ref.dtype)
    if o_ref.shape[1] > cols_in:
        o_ref[:, cols_in:] = jnp.zeros((tr, o_ref.shape[1] - cols_in), o_ref.dtype)
    if pad_rows:
        row = block * tr + lax.broadcasted_iota(jnp.int32, x.shape, 0)
        x = jnp.where(row < rows_in, x, jnp.zeros_like(x))
    o_ref[:, :cols_in] = x


def _cast_kernel(x_ref, o_ref, *, rows_in, pad_rows):
    _cast_block(x_ref, o_ref, pl.program_id(0), rows_in, pad_rows)


def _cast_bf16(w, lead=(), rows_out=None, cols_out=None):
    r, c = w.shape[-2:]
    rows_out = rows_out or r
    cols_out = cols_out or c
    tr = 8
    while tr * 2 * c * 4 <= CAST_BLOCK_BYTES and r % (tr * 2) == 0 and rows_out % (tr * 2) == 0:
        tr *= 2
    assert r % tr == 0 and rows_out % tr == 0 and c % LANES == 0
    n_in = r // tr
    squeeze = (None,) * len(lead)
    blocks = _nbytes((tr, c), F32) + _nbytes((tr, cols_out), BF16)
    return pl.pallas_call(
        functools.partial(_cast_kernel, rows_in=r, pad_rows=rows_out > r),
        grid=(rows_out // tr,),
        in_specs=[pl.BlockSpec(squeeze + (tr, c), lambda i: tuple(lead) + (jnp.minimum(i, n_in - 1), 0))],
        out_specs=pl.BlockSpec((tr, cols_out), lambda i: (i, 0)),
        out_shape=jax.ShapeDtypeStruct((rows_out, cols_out), BF16),
        compiler_params=_params(("parallel",), blocks),
        name="cast_bf16",
    )(w)


class _RidingCast:
    def __init__(self, w, lead, rows_out, cols_out, n_steps):
        self.w, self.lead = w, tuple(lead)
        self.rows_in, self.cols_in = w.shape[-2:]
        self.rows_out, self.cols_out = rows_out or self.rows_in, cols_out or self.cols_in
        tr = 16
        while (self.rows_out % tr or self.rows_in % tr or self.rows_out // tr > n_steps):
            tr += 16
        self.tr = tr
        self.n_out, self.n_in = self.rows_out // tr, self.rows_in // tr

    def specs(self, step_of):
        squeeze = (None,) * len(self.lead)
        blk = lambda *g: jnp.minimum(step_of(*g), self.n_out - 1)
        return (pl.BlockSpec(squeeze + (self.tr, self.cols_in),
                             lambda *g: self.lead + (jnp.minimum(blk(*g), self.n_in - 1), 0)),
                pl.BlockSpec((self.tr, self.cols_out), lambda *g: (blk(*g), 0)),
                jax.ShapeDtypeStruct((self.rows_out, self.cols_out), BF16))

    def block_bytes(self):
        return _nbytes((self.tr, self.cols_in), F32) + _nbytes((self.tr, self.cols_out), BF16)

    def run(self, x_ref, o_ref, step):
        _cast_block(x_ref, o_ref, jnp.minimum(step, self.n_out - 1), self.rows_in,
                    self.rows_out > self.rows_in)


def _dot(a, b):
    return jnp.dot(a, b, preferred_element_type=F32)


def _swiglu_kernel(a_ref, wg_ref, wu_ref, *refs, rides):
    nr = len(rides)
    o_ref = refs[nr]
    step = pl.program_id(0) * pl.num_programs(1) + pl.program_id(1)
    for ride, src_ref, dst_ref in zip(rides, refs[:nr], refs[nr + 1:]):
        ride.run(src_ref, dst_ref, step)
    a = a_ref[...]
    g = _dot(a, wg_ref[...])
    u = _dot(a, wu_ref[...])
    o_ref[...] = (g * jax.nn.sigmoid(g) * u).astype(o_ref.dtype)


def _mm_swiglu(a, wg, wu, tm, tn, ride_specs=()):
    m, k = a.shape
    n = wg.shape[1]
    tm = _tile(m, tm)
    assert n % tn == 0
    nj = n // tn
    rides = [_RidingCast(*spec, n_steps=(m // tm) * nj) for spec in ride_specs]
    ride_io = [ride.specs(lambda i, j: i * nj + j) for ride in rides]
    blocks = (_nbytes((tm, k), BF16) + 2 * _nbytes((k, tn), BF16) + _nbytes((tm, tn), BF16)
              + sum(ride.block_bytes() for ride in rides))
    out = pl.pallas_call(
        functools.partial(_swiglu_kernel, rides=rides),
        grid=(m // tm, nj),
        in_specs=[pl.BlockSpec((tm, k), lambda i, j: (i, 0)),
                  pl.BlockSpec((k, tn), lambda i, j: (0, j)),
                  pl.BlockSpec((k, tn), lambda i, j: (0, j))] + [io[0] for io in ride_io],
        out_specs=[pl.BlockSpec((tm, tn), lambda i, j: (i, j))] + [io[1] for io in ride_io],
        out_shape=[jax.ShapeDtypeStruct((m, n), BF16)] + [io[2] for io in ride_io],
        compiler_params=_params(("arbitrary", "arbitrary") if rides else ("parallel", "parallel"), blocks,
                                scratch_bytes=3 * _nbytes((tm, tn), F32)),
        name="ffn_gate_up",
    )(a, wg, wu, *[ride.w for ride in rides])
    return out[0], out[1:]


def _residual_kernel(a_ref, b_ref, r_ref, o_ref, *, scale):
    o_ref[...] = r_ref[...] + scale * _dot(a_ref[...], b_ref[...])


def _mm_residual(a, b, res, scale, tm=1024, tn=1024, tk_max=4096):
    m, kdim = a.shape
    n = b.shape[1]
    tm, tn = _tile(m, tm), _tile(n, tn)
    nk = 1
    while kdim // nk > tk_max or kdim % nk or (kdim // nk) % LANES:
        nk += 1
    tk = kdim // nk
    blocks = (_nbytes((tm, tk), BF16) + _nbytes((tk, tn), BF16) + 2 * _nbytes((tm, tn), F32))
    out = res
    for kc in range(nk):
        out = pl.pallas_call(
            functools.partial(_residual_kernel, scale=scale),
            grid=(m // tm, n // tn),
            in_specs=[pl.BlockSpec((tm, tk), lambda i, j, kc=kc: (i, kc)),
                      pl.BlockSpec((tk, tn), lambda i, j, kc=kc: (kc, j)),
                      pl.BlockSpec((tm, tn), lambda i, j: (i, j))],
            out_specs=pl.BlockSpec((tm, tn), lambda i, j: (i, j)),
            out_shape=jax.ShapeDtypeStruct((m, n), F32),
            compiler_params=_params(("parallel", "parallel"), blocks,
                                    scratch_bytes=_nbytes((tm, tn), F32)),
            name="matmul_residual",
        )(a, b, out)
    return out


def _plain_kernel(a_ref, b_ref, o_ref):
    o_ref[...] = _dot(a_ref[...], b_ref[...]).astype(o_ref.dtype)


def _mm_plain(a, b, out_dtype, n=None, col0=0, tm=1024, tn=1024):
    m, k = a.shape
    n = n or b.shape[1]
    tm, tn = _tile(m, tm), _tile(n, tn)
    assert col0 % tn == 0
    joff = col0 // tn
    blocks = _nbytes((tm, k), BF16) + _nbytes((k, tn), BF16) + _nbytes((tm, tn), out_dtype)
    return pl.pallas_call(
        _plain_kernel,
        grid=(m // tm, n // tn),
        in_specs=[pl.BlockSpec((tm, k), lambda i, j: (i, 0)),
                  pl.BlockSpec((k, tn), lambda i, j: (0, j + joff))],
        out_specs=pl.BlockSpec((tm, tn), lambda i, j: (i, j)),
        out_shape=jax.ShapeDtypeStruct((m, n), out_dtype),
        compiler_params=_params(("parallel", "parallel"), blocks,
                                scratch_bytes=_nbytes((tm, tn), F32)),
        name="matmul_plain",
    )(a, b)


def _headnorm_kernel(a_ref, b_ref, g_ref, o_ref, *, head_dim):
    x = _dot(a_ref[...], b_ref[...])
    tn = x.shape[1]
    for h in range(tn // head_dim):
        sl = slice(h * head_dim, (h + 1) * head_dim)
        xh = x[:, sl]
        y = xh * lax.rsqrt(jnp.mean(xh * xh, axis=-1, keepdims=True) + RMS_EPS)
        o_ref[:, sl] = (y * g_ref[:, sl]).astype(o_ref.dtype)


def _mm_headnorm(a, b, gain_row, head_dim, col0=0, tm=1024, tn=1024):
    m, k = a.shape
    n = gain_row.shape[1]
    tm, tn = _tile(m, tm), _tile(n, tn)
    assert col0 % tn == 0
    joff = col0 // tn
    blocks = (_nbytes((tm, k), BF16) + _nbytes((k, tn), BF16) + _nbytes((tm, tn), BF16)
              + _nbytes((8, tn), F32))
    return pl.pallas_call(
        functools.partial(_headnorm_kernel, head_dim=head_dim),
        grid=(m // tm, n // tn),
        in_specs=[pl.BlockSpec((tm, k), lambda i, j: (i, 0)),
                  pl.BlockSpec((k, tn), lambda i, j: (0, j + joff)),
                  pl.BlockSpec((1, tn), lambda i, j: (0, j))],
        out_specs=pl.BlockSpec((tm, tn), lambda i, j: (i, j)),
        out_shape=jax.ShapeDtypeStruct((m, n), BF16),
        compiler_params=_params(("parallel", "parallel"), blocks,
                                scratch_bytes=_nbytes((tm, tn), F32)),
        name="matmul_headnorm",
    )(a, b, gain_row)


def _mla_down_kernel(a_ref, w_ref, gq_ref, gkv_ref, cq_ref, ckv_ref, kr_ref, *, q_lora, kv_lora):
    x = _dot(a_ref[...], w_ref[...])
    xq = x[:, :q_lora]
    cq = xq * lax.rsqrt(jnp.mean(xq * xq, axis=-1, keepdims=True) + RMS_EPS) * gq_ref[...]
    cq_ref[...] = cq.astype(cq_ref.dtype)
    xkv = x[:, q_lora:q_lora + kv_lora]
    ckv = xkv * lax.rsqrt(jnp.mean(xkv * xkv, axis=-1, keepdims=True) + RMS_EPS) * gkv_ref[...]
    ckv_ref[...] = ckv.astype(ckv_ref.dtype)
    kr_ref[...] = x[:, q_lora + kv_lora:]


def _mla_down(a, wcat, gq, gkv, q_lora, kv_lora, tm=1024):
    m, k = a.shape
    n = wcat.shape[1]
    nr = n - q_lora - kv_lora
    tm = _tile(m, tm)
    blocks = (_nbytes((tm, k), BF16) + _nbytes((k, n), BF16) + _nbytes((tm, q_lora), BF16)
              + _nbytes((tm, kv_lora), BF16) + _nbytes((tm, nr), F32))
    return pl.pallas_call(
        functools.partial(_mla_down_kernel, q_lora=q_lora, kv_lora=kv_lora),
        grid=(m // tm,),
        in_specs=[pl.BlockSpec((tm, k), lambda i: (i, 0)),
                  pl.BlockSpec((k, n), lambda i: (0, 0)),
                  pl.BlockSpec((1, q_lora), lambda i: (0, 0)),
                  pl.BlockSpec((1, kv_lora), lambda i: (0, 0))],
        out_specs=[pl.BlockSpec((tm, q_lora), lambda i: (i, 0)),
                   pl.BlockSpec((tm, kv_lora), lambda i: (i, 0)),
                   pl.BlockSpec((tm, nr), lambda i: (i, 0))],
        out_shape=[jax.ShapeDtypeStruct((m, q_lora), BF16),
                   jax.ShapeDtypeStruct((m, kv_lora), BF16),
                   jax.ShapeDtypeStruct((m, nr), F32)],
        compiler_params=_params(("parallel",), blocks, scratch_bytes=_nbytes((tm, n), F32)),
        name="mla_down",
    )(a, wcat, gq, gkv)


def _mla_q_kernel(a_ref, w_ref, gn_ref, gr_ref, ta_ref, o_ref):
    x = _dot(a_ref[...], w_ref[...])
    rope_scale = gr_ref[...] * ta_ref[...]
    for h in range(x.shape[1] // QK_PAD):
        xn = x[:, h * QK_PAD:h * QK_PAD + LANES]
        xr = x[:, h * QK_PAD + LANES:(h + 1) * QK_PAD]
        ms = jnp.sum(xn * xn + 0.5 * (xr * xr), axis=-1, keepdims=True) * (1.0 / QK_HEAD)
        rs = lax.rsqrt(ms + RMS_EPS)
        o_ref[:, h * QK_PAD:h * QK_PAD + LANES] = (xn * rs * gn_ref[...]).astype(o_ref.dtype)
        o_ref[:, h * QK_PAD + LANES:(h + 1) * QK_PAD] = (xr * rs * rope_scale).astype(o_ref.dtype)


def _mla_q(cq, wuq_p, g_nope, g_rope, t_a, seq, tm=1024, tn=2048):
    m, k = cq.shape
    n = wuq_p.shape[1]
    tm, tn = _tile(seq, tm), _tile(n, tn)
    nsb = seq // tm
    row_spec = pl.BlockSpec((1, LANES), lambda i, j: (0, 0))
    blocks = (_nbytes((tm, k), BF16) + _nbytes((k, tn), BF16) + _nbytes((tm, tn), BF16)
              + _nbytes((tm, LANES), F32))
    return pl.pallas_call(
        _mla_q_kernel,
        grid=(m // tm, n // tn),
        in_specs=[pl.BlockSpec((tm, k), lambda i, j: (i, 0)),
                  pl.BlockSpec((k, tn), lambda i, j: (0, j)),
                  row_spec, row_spec,
                  pl.BlockSpec((tm, LANES), lambda i, j: (i % nsb, 0))],
        out_specs=pl.BlockSpec((tm, tn), lambda i, j: (i, j)),
        out_shape=jax.ShapeDtypeStruct((m, n), BF16),
        compiler_params=_params(("parallel", "parallel"), blocks,
                                scratch_bytes=_nbytes((tm, tn), F32)),
        name="mla_q_proj",
    )(cq, wuq_p, g_nope, g_rope, t_a)


FLASH_BQ = 2048
FLASH_BK = 1024
FLASH_COL_CHUNK = 512


def _mla_kv_kernel(a_ref, w_ref, kr_ref, gn_ref, ga_ref, gb_ref, ta_ref, tb_ref, k_ref, vt_ref, *, bk):
    x = _dot(a_ref[...], w_ref[...])
    kra = kr_ref[:, :LANES]
    krb = kr_ref[:, LANES:]
    ss_rope = 0.5 * jnp.sum(kra * kra, axis=-1, keepdims=True)
    kr_roped = kra * (ga_ref[...] * ta_ref[...]) + krb * (gb_ref[...] * tb_ref[...])
    width = QK_NOPE + V_HEAD
    for h in range(x.shape[1] // width):
        kn = x[:, h * width:h * width + QK_NOPE]
        ms = (jnp.sum(kn * kn, axis=-1, keepdims=True) + ss_rope) * (1.0 / QK_HEAD)
        rs = lax.rsqrt(ms + RMS_EPS)
        k_ref[:, h * QK_PAD:h * QK_PAD + LANES] = (kn * rs * gn_ref[...]).astype(k_ref.dtype)
        k_ref[:, h * QK_PAD + LANES:(h + 1) * QK_PAD] = (kr_roped * rs).astype(k_ref.dtype)
        for c in range(x.shape[0] // bk):
            v = x[c * bk:(c + 1) * bk, h * width + QK_NOPE:(h + 1) * width]
            vt_ref[h, c] = v.T.astype(vt_ref.dtype)


def _mla_kv(ckv, wukv, k_rope, g_nope, g_a, g_b, t_a, t_b, bsz, seq, bk, tm=1024, tn=2048):
    m, k = ckv.shape
    n = wukv.shape[1]
    width = QK_NOPE + V_HEAD
    tm, tn = _tile(seq, tm), _tile(n, tn)
    assert tm % bk == 0
    hpt = tn // width
    nsb = seq // tm
    tab_spec = pl.BlockSpec((tm, LANES), lambda i, j: (i % nsb, 0))
    row_spec = pl.BlockSpec((1, LANES), lambda i, j: (0, 0))
    blocks = (_nbytes((tm, k), BF16) + _nbytes((k, tn), BF16) + _nbytes((tm, hpt * QK_PAD), BF16)
              + _nbytes((tm, hpt * V_HEAD), BF16) + 4 * _nbytes((tm, LANES), F32))
    return pl.pallas_call(
        functools.partial(_mla_kv_kernel, bk=bk),
        grid=(m // tm, n // tn),
        in_specs=[pl.BlockSpec((tm, k), lambda i, j: (i, 0)),
                  pl.BlockSpec((k, tn), lambda i, j: (0, j)),
                  pl.BlockSpec((tm, 2 * LANES), lambda i, j: (i, 0)),
                  row_spec, row_spec, row_spec,
                  tab_spec, tab_spec],
        out_specs=[pl.BlockSpec((tm, hpt * QK_PAD), lambda i, j: (i, j)),
                   pl.BlockSpec((None, hpt, tm // bk, V_HEAD, bk), lambda i, j: (i // nsb, j, i % nsb, 0, 0))],
        out_shape=[jax.ShapeDtypeStruct((m, (n // width) * QK_PAD), BF16),
                   jax.ShapeDtypeStruct((bsz, n // width, seq // bk, V_HEAD, bk), BF16)],
        compiler_params=_params(("parallel", "parallel"), blocks,
                                scratch_bytes=2 * _nbytes((tm, tn), F32)),
        name="mla_kv_proj",
    )(ckv, wukv, k_rope, g_nope, g_a, g_b, t_a, t_b)


def _dot_nt(a, b):
    return lax.dot_general(a, b, (((1,), (1,)), ((), ())), preferred_element_type=F32)


def _flash_kernel(q_ref, k_ref, vt_ref, o_ref, m_ref, l_ref, acc_ref, s_ref, *, bq, bk):
    qi = pl.program_id(2)
    half = bk // 2
    m_ref[...] = jnp.full(m_ref.shape, MASK_VALUE, F32)
    l_ref[...] = jnp.zeros(l_ref.shape, F32)
    acc_ref[...] = jnp.zeros(acc_ref.shape, F32)

    def scores(j, slot, c0=0):
        kb = k_ref[pl.ds(pl.multiple_of(j * bk, bk), bk), :]
        s_ref[slot, :, c0:] = _dot_nt(kb, q_ref[c0:, :])

    def softmax_pv(j, slot, r0=0, nr=bk, c0=0, diagonal=False):
        st = s_ref[slot, r0:r0 + nr, c0:]
        if diagonal:
            row = lax.broadcasted_iota(jnp.int32, (nr, nr), 0)
            col = lax.broadcasted_iota(jnp.int32, (nr, nr), 1)
            tri = jnp.where(row <= col, st[:, :nr], MASK_VALUE)
            st = tri if st.shape[1] == nr else jnp.concatenate([tri, st[:, nr:]], axis=1)
        m_old = m_ref[:, c0:]
        m_new = jnp.maximum(m_old, jnp.max(st, axis=0, keepdims=True))
        p = jnp.exp2(st - m_new)
        alpha = jnp.exp2(m_old - m_new)
        l_ref[:, c0:] = alpha * l_ref[:, c0:] + jnp.sum(p, axis=0, keepdims=True)
        acc_ref[:, c0:] = alpha * acc_ref[:, c0:] + _dot(vt_ref[j, :, r0:r0 + nr], p.astype(BF16))
        m_ref[:, c0:] = m_new

    def softmax_pv_chunked(j, slot):
        for c0 in range(0, bq, FLASH_COL_CHUNK):
            c1 = c0 + FLASH_COL_CHUNK
            st = s_ref[slot, :, c0:c1]
            m_old = m_ref[:, c0:c1]
            m_new = jnp.maximum(m_old, jnp.max(st, axis=0, keepdims=True))
            p = jnp.exp2(st - m_new)
            alpha = jnp.exp2(m_old - m_new)
            l_ref[:, c0:c1] = alpha * l_ref[:, c0:c1] + jnp.sum(p, axis=0, keepdims=True)
            acc_ref[:, c0:c1] = alpha * acc_ref[:, c0:c1] + _dot(vt_ref[j], p.astype(BF16))
            m_ref[:, c0:c1] = m_new

    def body(i, carry):
        scores(2 * i + 1, 1)
        softmax_pv_chunked(2 * i, 0)
        scores(2 * i + 2, 0)
        softmax_pv_chunked(2 * i + 1, 1)
        return carry

    scores(0, 0)
    lax.fori_loop(0, qi, body, 0)
    scores(2 * qi + 1, 1, c0=bk)
    for g in range(4):
        softmax_pv(2 * qi + g // 2, g // 2, r0=(g % 2) * half, nr=half, c0=g * half, diagonal=True)
    o_ref[...] = (acc_ref[...] / l_ref[...]).T.astype(o_ref.dtype)


def _mla_flash(q, k, vt, bq, bk):
    b, s, _ = q.shape
    h = q.shape[2] // QK_PAD
    assert s % bq == 0 and bq == 2 * bk
    blocks = (_nbytes((bq, QK_PAD), BF16) + _nbytes((s, QK_PAD), BF16) + _nbytes((s, V_HEAD), BF16)
              + _nbytes((bq, V_HEAD), BF16))
    scratch = 2 * _nbytes((8, bq), F32) + _nbytes((V_HEAD, bq), F32) + 6 * _nbytes((bk, bq), F32)
    return pl.pallas_call(
        functools.partial(_flash_kernel, bq=bq, bk=bk),
        grid=(b, h, s // bq),
        in_specs=[pl.BlockSpec((None, bq, QK_PAD), lambda bi, hi, qi: (bi, qi, hi)),
                  pl.BlockSpec((None, s, QK_PAD), lambda bi, hi, qi: (bi, 0, hi)),
                  pl.BlockSpec((None, None, s // bk, V_HEAD, bk), lambda bi, hi, qi: (bi, hi, 0, 0, 0))],
        out_specs=pl.BlockSpec((None, bq, V_HEAD), lambda bi, hi, qi: (bi, qi, hi)),
        out_shape=jax.ShapeDtypeStruct((b, s, h * V_HEAD), BF16),
        scratch_shapes=[pltpu.VMEM((1, bq), F32), pltpu.VMEM((1, bq), F32),
                        pltpu.VMEM((V_HEAD, bq), F32), pltpu.VMEM((2, bk, bq), F32)],
        compiler_params=_params(("parallel", "parallel", "arbitrary"), blocks, scratch_bytes=scratch),
        name="mla_flash_attention",
    )(q, k, vt)


def _t5_causal_bucket(dist):
    max_exact = NUM_BUCKETS // 2
    n = jnp.maximum(dist, 0)
    nf = jnp.maximum(n, 1).astype(F32)
    large = max_exact + (jnp.log(nf / max_exact) / math.log(MAX_DISTANCE / max_exact)
                         * (NUM_BUCKETS - max_exact)).astype(jnp.int32)
    large = jnp.minimum(large, NUM_BUCKETS - 1)
    return jnp.where(n < max_exact, n, large)


def _band_bias_kernel(tab_ref, bucket_ref, o_ref, *, heads):
    g = pl.program_id(0)
    h = pl.program_id(1)
    bucket = bucket_ref[...]
    acc = jnp.full(bucket.shape, MASK_VALUE, F32)
    for b in range(NUM_BUCKETS):
        acc = jnp.where(bucket == b, tab_ref[b, g * heads + h], acc)
    o_ref[...] = acc


def _band_bias(rel_bias, buckets, heads):
    g = buckets.shape[0]
    blk = buckets.shape[1:]
    blocks = _nbytes(blk, jnp.int32) + _nbytes(blk, F32)
    return pl.pallas_call(
        functools.partial(_band_bias_kernel, heads=heads),
        grid=(g, heads),
        in_specs=[pl.BlockSpec(memory_space=pltpu.SMEM),
                  pl.BlockSpec((None,) + blk, lambda gi, hi: (gi, 0, 0))],
        out_specs=pl.BlockSpec((None, None) + blk, lambda gi, hi: (gi, hi, 0, 0)),
        out_shape=jax.ShapeDtypeStruct((g, heads) + blk, F32),
        compiler_params=_params(("parallel", "parallel"), blocks),
        name="dilated_band_bias",
    )(rel_bias, buckets)


DIL_HEAD_GROUP = 4


def _dilated_kernel(q_ref, kp_ref, kc_ref, vp_ref, vc_ref, bias_ref, o_ref, lse_ref, s_ref, *, scale, heads):
    n = pl.program_id(2)
    has_prev = n > 0
    d = DIL_HEAD_DIM
    hg = s_ref.shape[1]
    lane = lax.broadcasted_iota(jnp.int32, (Q_BLOCK, LANES), 1)
    col = lax.broadcasted_iota(jnp.int32, (Q_BLOCK, 2 * Q_BLOCK), 1)
    keep = jnp.logical_or(has_prev, col >= Q_BLOCK)

    def scores(grp, slot):
        for i in range(hg):
            sl = slice((grp * hg + i) * d, (grp * hg + i + 1) * d)
            kcat = jnp.concatenate([kp_ref[:, sl], kc_ref[:, sl]], axis=0)
            s_ref[slot, i] = _dot_nt(q_ref[:, sl], kcat)

    def softmax_pv(grp, slot, lse_tile):
        for i in range(hg):
            h = grp * hg + i
            sl = slice(h * d, (h + 1) * d)
            s = jnp.where(keep, s_ref[slot, i] * scale + bias_ref[h], MASK_VALUE)
            m = jnp.max(s, axis=-1, keepdims=True)
            p = jnp.exp(s - m)
            den = jnp.sum(p, axis=-1, keepdims=True)
            vcat = jnp.concatenate([vp_ref[:, sl], vc_ref[:, sl]], axis=0)
            o_ref[:, sl] = _dot((p / den).astype(BF16), vcat).astype(o_ref.dtype)
            lse_tile = jnp.where(lane == h, m + jnp.log(den), lse_tile)
        return lse_tile

    lse_tile = jnp.zeros((Q_BLOCK, LANES), F32)
    n_grp = heads // hg
    scores(0, 0)
    for grp in range(n_grp):
        if grp + 1 < n_grp:
            scores(grp + 1, (grp + 1) % 2)
        lse_tile = softmax_pv(grp, grp % 2, lse_tile)
    lse_ref[...] = lse_tile


def _dilated_group(q, k, v, bias_g):
    b, dilation, length, hd = q.shape
    hg = math.gcd(DIL_HEAD_GROUP, DIL_HEADS)
    nb = length // Q_BLOCK
    cur = lambda bi, r, n: (bi, r, n, 0)
    prev = lambda bi, r, n: (bi, r, jnp.maximum(n - 1, 0), 0)
    blk = (None, None, Q_BLOCK, hd)
    blocks = 6 * _nbytes((Q_BLOCK, hd), BF16) + _nbytes(bias_g.shape, F32) + _nbytes((Q_BLOCK, LANES), F32)
    return pl.pallas_call(
        functools.partial(_dilated_kernel, scale=DIL_HEAD_DIM ** -0.5, heads=DIL_HEADS),
        grid=(b, dilation, nb),
        in_specs=[pl.BlockSpec(blk, cur),
                  pl.BlockSpec(blk, prev), pl.BlockSpec(blk, cur),
                  pl.BlockSpec(blk, prev), pl.BlockSpec(blk, cur),
                  pl.BlockSpec(bias_g.shape, lambda bi, r, n: (0, 0, 0))],
        out_specs=[pl.BlockSpec(blk, cur),
                   pl.BlockSpec((None, None, Q_BLOCK, LANES), cur)],
        out_shape=[jax.ShapeDtypeStruct((b, dilation, length, hd), BF16),
                   jax.ShapeDtypeStruct((b, dilation, length, LANES), F32)],
        scratch_shapes=[pltpu.VMEM((2, hg, Q_BLOCK, 2 * Q_BLOCK), F32)],
        compiler_params=_params(("parallel", "parallel", "arbitrary"), blocks,
                                scratch_bytes=2 * hg * _nbytes((Q_BLOCK, 2 * Q_BLOCK), F32)),
        name=f"dilated_attention_d{dilation}",
    )(q, k, k, v, v, bias_g)


def _combine_kernel(*refs, heads, dilations):
    ng = len(dilations)
    n_perm = sum(dil > 1 for dil in dilations)
    o_refs, l_refs = refs[:ng], refs[ng:2 * ng]
    p_refs = refs[2 * ng:2 * ng + n_perm]
    o_ref, lse_scr, o_scr = refs[2 * ng + n_perm:]
    tm = o_ref.shape[0]
    ip = 0
    for g, dil in enumerate(dilations):
        for r in range(dil):
            rows = pl.ds(r, tm // dil, stride=dil) if dil > 1 else slice(None)
            lse_scr[g, rows, :] = l_refs[g][r]
        og = o_refs[g][...].reshape(tm, o_ref.shape[1])
        if dil > 1:
            o_scr[g] = _dot(p_refs[ip][...], og)
            ip += 1
        else:
            o_scr[g] = og.astype(F32)
    lse = [lse_scr[g] for g in range(ng)]
    m = functools.reduce(jnp.maximum, lse)
    e = [jnp.exp(l - m) for l in lse]
    tot = functools.reduce(jnp.add, e)
    w = [x / tot for x in e]
    d = DIL_HEAD_DIM
    for h in range(heads):
        sl = slice(h * d, (h + 1) * d)
        o = functools.reduce(jnp.add, [w[g][:, h:h + 1] * o_scr[g, :, sl] for g in range(ng)])
        o_ref[:, sl] = o.astype(o_ref.dtype)


def _combine_groups(outs, lses, seq, tm=256):
    dilations = tuple(o.shape[1] for o in outs)
    bsz, hd = outs[0].shape[0], outs[0].shape[3]
    ng = len(outs)
    tm = _tile(seq, tm)
    nsb = seq // tm
    m = bsz * seq
    idx = lambda i: (i // nsb, 0, i % nsb, 0)
    o_specs = [pl.BlockSpec((None, dil, tm // dil, hd), idx) for dil in dilations]
    l_specs = [pl.BlockSpec((None, dil, tm // dil, LANES), idx) for dil in dilations]
    for dil in dilations:
        assert tm % (16 * dil) == 0
    perms = [_gather_perm(tm, dil).T for dil in dilations if dil > 1]
    p_specs = [pl.BlockSpec((tm, tm), lambda i: (0, 0)) for _ in perms]
    blocks = ((ng + 1) * _nbytes((tm, hd), BF16) + ng * _nbytes((tm, LANES), F32)
              + len(perms) * _nbytes((tm, tm), BF16))
    scratch = ng * (_nbytes((tm, LANES), F32) + _nbytes((tm, hd), F32))
    return pl.pallas_call(
        functools.partial(_combine_kernel, heads=DIL_HEADS, dilations=dilations),
        grid=(m // tm,),
        in_specs=o_specs + l_specs + p_specs,
        out_specs=pl.BlockSpec((tm, hd), lambda i: (i, 0)),
        out_shape=jax.ShapeDtypeStruct((m, hd), BF16),
        scratch_shapes=[pltpu.VMEM((ng, tm, LANES), F32), pltpu.VMEM((ng, tm, hd), F32)],
        compiler_params=_params(("parallel",), blocks, scratch_bytes=2 * scratch),
        name="dilated_combine",
    )(*outs, *lses, *perms)


GATE_UP_TILE = 512
GATE_UP_ROWS = 1024
DOWN_TILE = 512
DOWN_K_MAX = 6144


def _rope_tables(seq):
    half = QK_ROPE // 2
    inv = ROPE_THETA ** (-jnp.arange(half, dtype=F32) / half)
    ang = jnp.arange(seq).astype(F32)[:, None] * inv[None, :]
    cos, sin = jnp.cos(ang), jnp.sin(ang)
    t_a = jnp.concatenate([cos, sin, cos, sin], axis=1)
    t_b = jnp.concatenate([-sin, cos, sin, -cos], axis=1)
    return t_a, t_b


def _pairs(x, order):
    half = QK_ROPE // 2
    parts = (x[..., :half], x[..., half:])
    return jnp.concatenate([parts[i] for i in order], axis=-1)


def _band_buckets():
    r = jnp.arange(Q_BLOCK)[:, None]
    c = jnp.arange(2 * Q_BLOCK)[None, :]
    steps = Q_BLOCK + r - c
    out = []
    for window, dilation in DIL_GROUPS:
        band = (steps >= 0) & (steps <= window // dilation)
        out.append(jnp.where(band, _t5_causal_bucket(steps * dilation), -1))
    return jnp.stack(out).astype(jnp.int32)


def _ffn_cast_specs(ffn_wg, ffn_wu, ffn_wd, lead):
    f = ffn_wg.shape[-1]
    fp = -(-f // GATE_UP_TILE) * GATE_UP_TILE
    return [(ffn_wg, lead, None, fp), (ffn_wu, lead, None, fp), (ffn_wd, lead, fp, None)]


def _ffn(h, xn, weights, cast_specs=()):
    wg, wu, wd = weights
    act, cast = _mm_swiglu(xn, wg, wu, tm=GATE_UP_ROWS, tn=GATE_UP_TILE, ride_specs=cast_specs)
    if wd is None:
        wd = cast[-1]
    return _mm_residual(act, wd, h, 0.5, tn=DOWN_TILE, tk_max=DOWN_K_MAX), cast


def kernel(x, ffn_norm, ffn_wg, ffn_wu, ffn_wd, attn_norm, mla_wdq, mla_q_lora_norm, mla_wuq, mla_wdkv, mla_kv_lora_norm, mla_wukv, mla_q_norm, mla_k_norm, mla_wo, kv_src_norm, w_kv_shared, k_norm_shared, rel_bias, dil_wq, dil_q_norm, dil_wo):
    bsz, seq, d = x.shape
    m = bsz * seq
    depth = ffn_norm.shape[0]
    n_a = depth // 2
    h = x.reshape(m, d)
    t_a, t_b = _rope_tables(seq)
    hd = DIL_HEADS * DIL_HEAD_DIM
    k_sh = v_sh = None
    bias_all = None

    for window, dilation in DIL_GROUPS:
        assert seq % (dilation * Q_BLOCK) == 0 and window // dilation <= Q_BLOCK
    bq, bk = min(FLASH_BQ, seq), min(FLASH_BK, seq)

    dils = tuple(dilation for _, dilation in DIL_GROUPS)
    gathered = lambda t, dil: t.reshape(bsz, dil, seq // dil, hd)

    ffn_order = [(l, p) for l in range(depth) for p in (0, 1)]
    ffn_specs = {lp: _ffn_cast_specs(ffn_wg, ffn_wu, ffn_wd, lp) for lp in ffn_order}
    ffn_weights = [_cast_bf16(*spec) for spec in ffn_specs[ffn_order[0]][:2]] + [None]

    extra_rides = {ffn_order[0]: ("w_kv_shared", (w_kv_shared, (), None, None))}
    if depth > n_a and len(ffn_order) > 1:
        extra_rides[ffn_order[1]] = ("dil_wq", (dil_wq, (0,), None, None))
    side = {}

    def ffn(h, xn, lp):
        nxt = ffn_order.index(lp) + 1
        specs = list(ffn_specs[ffn_order[nxt]]) if nxt < len(ffn_order) else []
        n_next = len(specs)
        if lp in extra_rides:
            specs.append(extra_rides[lp][1])
        if ffn_weights[2] is None:
            specs.append(ffn_specs[lp][2])
        h, cast = _ffn(h, xn, ffn_weights, specs)
        if lp in extra_rides:
            side[extra_rides[lp][0]] = cast[n_next]
        return h, cast[:n_next]

    for l in range(depth):
        if l == n_a:
            gains = jnp.stack([kv_src_norm] * N_GROUPS + [ffn_norm[l, 0]])
            *xs, xn = _rmsnorm(h, gains, dils + (1,), bsz, seq)
            wkv = side["w_kv_shared"] if "w_kv_shared" in side else _cast_bf16(w_kv_shared)
            k_sh, v_sh = [], []
            for g, dil in enumerate(dils):
                a_g = xs[g].reshape(m, d)
                gk = jnp.tile(k_norm_shared[g], DIL_HEADS)[None, :]
                k_sh.append(gathered(_mm_headnorm(a_g, wkv, gk, DIL_HEAD_DIM, col0=g * hd), dil))
                v_sh.append(gathered(_mm_plain(a_g, wkv, BF16, n=hd, col0=(N_GROUPS + g) * hd), dil))
            bias_all = _band_bias(rel_bias, _band_buckets(), DIL_HEADS)
        else:
            (xn,) = _rmsnorm(h, ffn_norm[l, 0][None, :], (1,), bsz, seq)
        h, ffn_weights = ffn(h, xn, (l, 0))

        if l < n_a:
            (xn,) = _rmsnorm(h, attn_norm[l][None, :], (1,), bsz, seq)
            a = l
            q_lora = mla_wdq.shape[2]
            kv_lora = mla_wdkv.shape[2] - QK_ROPE
            w_rope = mla_wdkv[a][:, kv_lora:]
            wcat = jnp.concatenate([mla_wdq[a], mla_wdkv[a][:, :kv_lora], _pairs(w_rope, (0, 0, 1, 1)),
                                    _pairs(w_rope, (1, 1, 0, 0))], axis=1).astype(BF16)
            cq, ckv, k_rope = _mla_down(xn, wcat, mla_q_lora_norm[a][None, :],
                                        mla_kv_lora_norm[a][None, :], q_lora, kv_lora)
            wuq_h = mla_wuq[a].reshape(q_lora, MLA_HEADS, QK_HEAD)
            wuq_p = jnp.concatenate([wuq_h[..., :QK_NOPE], _pairs(wuq_h[..., QK_NOPE:], (0, 0, 1, 1))],
                                    axis=-1).reshape(q_lora, -1).astype(BF16)
            gq = mla_q_norm[a] * (QK_HEAD ** -0.5 * math.log2(math.e))
            q = _mla_q(cq, wuq_p, gq[None, :QK_NOPE], _pairs(gq[QK_NOPE:], (0, 0, 1, 1))[None, :], t_a, seq)
            gk = mla_k_norm[a]
            k, vt = _mla_kv(ckv, _cast_bf16(mla_wukv, (a,)), k_rope, gk[None, :QK_NOPE],
                            _pairs(gk[QK_NOPE:], (0, 0, 1, 1))[None, :], _pairs(gk[QK_NOPE:], (1, 1, 0, 0))[None, :],
                            t_a, t_b, bsz, seq, bk)
            o = _mla_flash(q.reshape(bsz, seq, -1), k.reshape(bsz, seq, -1), vt, bq, bk)
            h = _mm_residual(o.reshape(m, -1), _cast_bf16(mla_wo, (a,)), h, 1.0)
        else:
            bl = l - n_a
            xq = _rmsnorm(h, jnp.stack([attn_norm[l]] * N_GROUPS), dils, bsz, seq)
            wq = side["dil_wq"] if bl == 0 and "dil_wq" in side else _cast_bf16(dil_wq, (bl,))
            outs, lses = [], []
            for g, dil in enumerate(dils):
                gq = jnp.tile(dil_q_norm[bl][g], DIL_HEADS)[None, :]
                q_g = gathered(_mm_headnorm(xq[g].reshape(m, d), wq, gq, DIL_HEAD_DIM, col0=g * hd), dil)
                o_g, l_g = _dilated_group(q_g, k_sh[g], v_sh[g], bias_all[g])
                outs.append(o_g)
                lses.append(l_g)
            o = _combine_groups(outs, lses, seq)
            h = _mm_residual(o, _cast_bf16(dil_wo, (bl,)), h, 1.0)

        (xn,) = _rmsnorm(h, ffn_norm[l, 1][None, :], (1,), bsz, seq)
        h, ffn_weights = ffn(h, xn, (l, 1))

    return h.reshape(bsz, seq, d)
```
